```python
import math
import jax, jax.numpy as jnp
from jax import lax
import numpy as np

D_MODEL = 1024
BATCH = 1
SEQ = 16384
DEPTH = 1
DEC_BATCH = 16
DEC_SEQ = 32
PAST_LEN = 4096

CHUNK = 64
Q_BLOCK = 128
D_FF = 2816
D_POOL = D_MODEL // 2
POOL_WINDOWS = (2, 4, 8, 16)
N_POOL_GROUPS = 4
POOL_GROUP = D_POOL // N_POOL_GROUPS
POOL_STATE = max(POOL_WINDOWS) - 1
H_DIFF = 4
HEAD_DIM = D_MODEL // (4 * H_DIFF)
D_ATTN = H_DIFF * 2 * HEAD_DIM
ROT_DIM = HEAD_DIM // 4
ROPE_THETA = 500000.0
N_BRANCH = 2
IN_COLS = D_POOL + 3 * D_ATTN + N_BRANCH * D_MODEL
EPS = 1e-6

kernel_name = "hybrid_pool_diffattn_streaming_step"


def rmsnorm(x, g):
    xf = x.astype(jnp.float32)
    y = xf * lax.rsqrt(jnp.mean(xf * xf, axis=-1, keepdims=True) + EPS)
    return (y * g.astype(jnp.float32)).astype(x.dtype)


def swiglu(x, w_gu, w_down):
    gate, up = jnp.split(x @ w_gu, 2, axis=-1)
    return (jax.nn.silu(gate) * up) @ w_down


def rope_partial(x, pos):
    half = ROT_DIM // 2
    inv = ROPE_THETA ** (-jnp.arange(0, ROT_DIM, 2, dtype=jnp.float32) / ROT_DIM)
    ang = pos.astype(jnp.float32)[:, None] * inv[None, :]
    cos = jnp.cos(ang)[:, None, None, :]
    sin = jnp.sin(ang)[:, None, None, :]
    xr = x[..., :ROT_DIM].astype(jnp.float32)
    x1, x2 = xr[..., :half], xr[..., half:]
    rot = jnp.concatenate([x1 * cos - x2 * sin, x2 * cos + x1 * sin], axis=-1)
    return jnp.concatenate([rot.astype(x.dtype), x[..., ROT_DIM:]], axis=-1)


def pool_mix(ext, n_hist, w_group, scale):
    B, L, C = ext.shape
    T = L - n_hist
    xf = ext.astype(jnp.float32)
    cs = jnp.concatenate([jnp.zeros((B, 1, C), jnp.float32), jnp.cumsum(xf, axis=1)], axis=1)
    hi = jnp.arange(n_hist, L) + 1
    means = []
    for g, w in enumerate(POOL_WINDOWS):
        lo = jnp.maximum(hi - w, 0)
        csg = cs[..., g * POOL_GROUP:(g + 1) * POOL_GROUP]
        cnt = (hi - lo).astype(jnp.float32)[None, :, None]
        means.append((csg[:, hi] - csg[:, lo]) / cnt)
    mixed = jnp.concatenate(means, axis=-1) - xf[:, n_hist:]
    mixed = mixed.reshape(B, T, N_POOL_GROUPS, POOL_GROUP)
    y = jnp.einsum("btgc,gce->btge", mixed, w_group.astype(jnp.float32)).reshape(B, T, C)
    return (y * scale.astype(jnp.float32)).astype(ext.dtype)


def diff_attention(q, k_all, v_all, q_pos, k_pos, lam, q_block):
    B, T = q.shape[0], q.shape[1]
    nb = T // q_block
    qb = jnp.moveaxis(q.reshape(B, nb, q_block, H_DIFF, 2, HEAD_DIM), 1, 0)
    pb = q_pos.reshape(nb, q_block)
    k_chunk = k_pos // CHUNK
    scale = HEAD_DIM ** -0.5

    def one_block(args):
        qi, pi = args
        s = jnp.einsum("bqhcd,bkhcd->bhcqk", qi, k_all).astype(jnp.float32) * scale
        mask = k_chunk[None, :] <= (pi // CHUNK)[:, None]
        p = jax.nn.softmax(jnp.where(mask, s, -jnp.inf), axis=-1)
        a = p[:, :, 0] - lam * p[:, :, 1]
        return jnp.einsum("bhqk,bkhe->bqhe", a.astype(v_all.dtype), v_all)

    out = lax.map(one_block, (qb, pb))
    return jnp.moveaxis(out, 0, 1).reshape(B, T, H_DIFF, 2 * HEAD_DIM)


def encoder_layer(x, k_hist, v_hist, pool_hist, pos0, q_block, lam_init,
                  ffn1_pre_g, ffn1_post_g, ffn1_w_gu, ffn1_w_down,
                  mix_pre_g, mix_post_g, w_in, pool_w, pool_scale,
                  lambda_q1, lambda_k1, lambda_q2, lambda_k2, subln_g,
                  w_branch_pool, w_branch_attn, w_out,
                  ffn2_pre_g, ffn2_post_g, ffn2_w_gu, ffn2_w_down):
    B, T, _ = x.shape
    h = x + 0.5 * rmsnorm(swiglu(rmsnorm(x, ffn1_pre_g), ffn1_w_gu, ffn1_w_down), ffn1_post_g)
    u = rmsnorm(h, mix_pre_g)
    proj = u @ w_in
    u_pool, q, k, v, g = jnp.split(proj, [D_POOL, D_POOL + D_ATTN, D_POOL + 2 * D_ATTN, D_POOL + 3 * D_ATTN], axis=-1)
    q_pos = pos0 + jnp.arange(T)
    q = rope_partial(q.reshape(B, T, H_DIFF, 2, HEAD_DIM), q_pos)
    k = rope_partial(k.reshape(B, T, H_DIFF, 2, HEAD_DIM), q_pos)
    v = v.reshape(B, T, H_DIFF, 2 * HEAD_DIM)
    k_all = jnp.concatenate([k_hist, k], axis=1)
    v_all = jnp.concatenate([v_hist, v], axis=1)
    k_pos = jnp.arange(pos0 + T)
    lam = (jnp.exp(jnp.sum(lambda_q1.astype(jnp.float32) * lambda_k1.astype(jnp.float32)))
           - jnp.exp(jnp.sum(lambda_q2.astype(jnp.float32) * lambda_k2.astype(jnp.float32))) + lam_init)
    attn = diff_attention(q, k_all, v_all, q_pos, k_pos, lam, q_block)
    attn = (rmsnorm(attn, subln_g) * (1.0 - lam_init)).reshape(B, T, D_ATTN)
    ext = jnp.concatenate([pool_hist, u_pool], axis=1)
    pool_out = pool_mix(ext, pool_hist.shape[1], pool_w, pool_scale)
    gates = jax.nn.sigmoid(g.astype(jnp.float32)).reshape(B, T, N_BRANCH, D_MODEL)
    merged = gates[:, :, 0] * (pool_out @ w_branch_pool) + gates[:, :, 1] * (attn @ w_branch_attn)
    h = h + rmsnorm(merged.astype(x.dtype) @ w_out, mix_post_g)
    y = h + 0.5 * rmsnorm(swiglu(rmsnorm(h, ffn2_pre_g), ffn2_w_gu, ffn2_w_down), ffn2_post_g)
    return y, k, v, ext[:, -POOL_STATE:]


def setup_inputs(seed: int = 0) -> dict:
    key = jax.random.key(seed)
    ks = jax.random.split(key, 32)
    f32 = jnp.float32
    nrm = lambda k, shape, s: jax.random.normal(k, shape, f32) * s
    gain = lambda k, n: 1.0 + 0.05 * jax.random.normal(k, (DEPTH, n), f32)
    return {
        "x_prompt": nrm(ks[0], (BATCH, SEQ, D_MODEL), 1.0),
        "x_sample": nrm(ks[1], (DEC_BATCH, DEC_SEQ, D_MODEL), 1.0),
        "cache_k": nrm(ks[2], (DEPTH, DEC_BATCH, PAST_LEN, H_DIFF, 2, HEAD_DIM), 1.0),
        "cache_v": nrm(ks[3], (DEPTH, DEC_BATCH, PAST_LEN, H_DIFF, 2 * HEAD_DIM), 1.0),
        "state_pool": nrm(ks[4], (DEPTH, DEC_BATCH, POOL_STATE, D_POOL), 1.0),
        "ffn1_pre_g": gain(ks[5], D_MODEL),
        "ffn1_post_g": gain(ks[6], D_MODEL),
        "ffn1_w_gu": nrm(ks[7], (DEPTH, D_MODEL, 2 * D_FF), D_MODEL ** -0.5),
        "ffn1_w_down": nrm(ks[8], (DEPTH, D_FF, D_MODEL), D_FF ** -0.5),
        "mix_pre_g": gain(ks[9], D_MODEL),
        "mix_post_g": gain(ks[10], D_MODEL),
        "w_in": nrm(ks[11], (DEPTH, D_MODEL, IN_COLS), D_MODEL ** -0.5),
        "pool_w": nrm(ks[12], (DEPTH, N_POOL_GROUPS, POOL_GROUP, POOL_GROUP), POOL_GROUP ** -0.5),
        "pool_scale": gain(ks[13], D_POOL),
        "lambda_q1": nrm(ks[14], (DEPTH, HEAD_DIM), 0.1),
        "lambda_k1": nrm(ks[15], (DEPTH, HEAD_DIM), 0.1),
        "lambda_q2": nrm(ks[16], (DEPTH, HEAD_DIM), 0.1),
        "lambda_k2": nrm(ks[17], (DEPTH, HEAD_DIM), 0.1),
        "subln_g": gain(ks[18], 2 * HEAD_DIM),
        "w_branch_pool": nrm(ks[19], (DEPTH, D_POOL, D_MODEL), D_POOL ** -0.5),
        "w_branch_attn": nrm(ks[20], (DEPTH, D_ATTN, D_MODEL), D_ATTN ** -0.5),
        "w_out": nrm(ks[21], (DEPTH, D_MODEL, D_MODEL), D_MODEL ** -0.5),
        "ffn2_pre_g": gain(ks[22], D_MODEL),
        "ffn2_post_g": gain(ks[23], D_MODEL),
        "ffn2_w_gu": nrm(ks[24], (DEPTH, D_MODEL, 2 * D_FF), D_MODEL ** -0.5),
        "ffn2_w_down": nrm(ks[25], (DEPTH, D_FF, D_MODEL), D_FF ** -0.5),
    }


def reference(x_prompt, x_sample, cache_k, cache_v, state_pool,
              ffn1_pre_g, ffn1_post_g, ffn1_w_gu, ffn1_w_down,
              mix_pre_g, mix_post_g, w_in, pool_w, pool_scale,
              lambda_q1, lambda_k1, lambda_q2, lambda_k2, subln_g,
              w_branch_pool, w_branch_attn, w_out,
              ffn2_pre_g, ffn2_post_g, ffn2_w_gu, ffn2_w_down):
    yp, ys = x_prompt, x_sample
    kp_l, vp_l, sp_l, ks_l, vs_l, ss_l = [], [], [], [], [], []
    Bp = x_prompt.shape[0]
    for l in range(DEPTH):
        lam_init = 0.8 - 0.6 * math.exp(-0.3 * l)
        w = (ffn1_pre_g[l], ffn1_post_g[l], ffn1_w_gu[l], ffn1_w_down[l],
             mix_pre_g[l], mix_post_g[l], w_in[l], pool_w[l], pool_scale[l],
             lambda_q1[l], lambda_k1[l], lambda_q2[l], lambda_k2[l], subln_g[l],
             w_branch_pool[l], w_branch_attn[l], w_out[l],
             ffn2_pre_g[l], ffn2_post_g[l], ffn2_w_gu[l], ffn2_w_down[l])
        yp, kp, vp, sp = encoder_layer(
            yp,
            jnp.zeros((Bp, 0, H_DIFF, 2, HEAD_DIM), yp.dtype),
            jnp.zeros((Bp, 0, H_DIFF, 2 * HEAD_DIM), yp.dtype),
            jnp.zeros((Bp, 0, D_POOL), yp.dtype),
            0, Q_BLOCK, lam_init, *w)
        ys, k_s, v_s, s_s = encoder_layer(
            ys, cache_k[l], cache_v[l], state_pool[l],
            cache_k.shape[2], ys.shape[1], lam_init, *w)
        kp_l.append(kp); vp_l.append(vp); sp_l.append(sp)
        ks_l.append(k_s); vs_l.append(v_s); ss_l.append(s_s)
    return (yp, ys, jnp.stack(kp_l), jnp.stack(vp_l), jnp.stack(sp_l), jnp.stack(ks_l), jnp.stack(vs_l), jnp.stack(ss_l))
```

```python
import functools
import math

import numpy as np
import jax
import jax.numpy as jnp
from jax import lax
from jax.experimental import pallas as pl
from jax.experimental.pallas import tpu as pltpu

D_MODEL = 1024
D_FF = 2816
D_POOL = 512
POOL_WINDOWS = (2, 4, 8, 16)
POOL_GROUP = 128
POOL_STATE = 15
H_DIFF = 4
HEAD_DIM = 64
D_ATTN = 512
ROT_DIM = 16
ROPE_THETA = 500000.0
CHUNK = 64
EPS = 1e-6
LAM_INIT = 0.8 - 0.6 * math.exp(-0.3 * 0)

V7X_LANES = 128
POOL_HALO = 16
ROW_TILE = 512
FF_CHUNKS = ((0, 1024), (1024, 1024), (2048, 768))

F32 = jnp.float32
BF16 = jnp.bfloat16


def _mib(n):
    return int(n * 1024 * 1024)


def _rms(x, g):
    return x * lax.rsqrt(jnp.mean(x * x, axis=-1, keepdims=True) + EPS) * g


def _dot(a, b):
    return jnp.dot(a, b, preferred_element_type=F32)


def _const_spec(shape):
    nd = len(shape)
    return pl.BlockSpec(shape, lambda *_: (0,) * nd, pipeline_mode=pl.Buffered(1))


def _ffn_kernel(x_ref, pre_ref, post_ref, wgu_ref, wd_ref, o_ref):
    x = x_ref[...]
    xn = _rms(x, pre_ref[...]).astype(BF16)
    acc = None
    for c0, cw in FF_CHUNKS:
        gate = _dot(xn, wgu_ref[:, c0:c0 + cw])
        up = _dot(xn, wgu_ref[:, D_FF + c0:D_FF + c0 + cw])
        act = (gate * jax.nn.sigmoid(gate) * up).astype(BF16)
        part = _dot(act, wd_ref[c0:c0 + cw, :])
        acc = part if acc is None else acc + part
    o_ref[...] = x + 0.5 * _rms(acc, post_ref[...])


def _ffn(x, pre_g, post_g, w_gu, w_down):
    rows = x.shape[0]
    row_spec = pl.BlockSpec((ROW_TILE, D_MODEL), lambda i: (i, 0))
    return pl.pallas_call(
        _ffn_kernel,
        grid=(rows // ROW_TILE,),
        in_specs=[row_spec, _const_spec((1, D_MODEL)), _const_spec((1, D_MODEL)),
                  _const_spec((D_MODEL, 2 * D_FF)), _const_spec((D_FF, D_MODEL))],
        out_specs=row_spec,
        out_shape=jax.ShapeDtypeStruct((rows, D_MODEL), F32),
        compiler_params=pltpu.CompilerParams(
            dimension_semantics=("arbitrary",), vmem_limit_bytes=_mib(48)),
        name="ffn",
    )(x, pre_g, post_g, w_gu, w_down)


def _rope_tables(tile_pos0, row_pos):
    d = np.arange(V7X_LANES) % HEAD_DIM
    inv = np.where(d < ROT_DIM, ROPE_THETA ** (-(d % (ROT_DIM // 2)) / (ROT_DIM // 2)), 0.0)
    ang_a = np.asarray(tile_pos0, np.float64)[:, None] * inv[None, :]
    ang_b = np.asarray(row_pos, np.float64)[:, None] * inv[None, :]
    half = ROT_DIM // 2
    sel_lo = np.where(d < half, -1.0, 0.0)[None, :]
    sel_hi = np.where((d >= half) & (d < ROT_DIM), 1.0, 0.0)[None, :]
    f = lambda a: jnp.asarray(a, F32)
    return (f(np.cos(ang_a))[:, None, :], f(np.sin(ang_a))[:, None, :],
            f(np.cos(ang_b)), f(np.sin(ang_b)), f(sel_lo), f(sel_hi))


def _mix_in_kernel(h_ref, g_ref, win_ref, cos_a_ref, sin_a_ref, cos_b_ref, sin_b_ref,
                   sel_lo_ref, sel_hi_ref,
                   upool_ref, kf_ref, vf_ref, gates_ref, q_ref, kb_ref, v_ref, *, transposed):
    u = _rms(h_ref[...], g_ref[...]).astype(BF16)
    upool_ref[...] = _dot(u, win_ref[:, 0:D_POOL])

    ca, sa = cos_a_ref[0], sin_a_ref[0]
    cb, sb = cos_b_ref[...], sin_b_ref[...]
    cos = ca * cb - sa * sb
    sin = sa * cb + ca * sb
    rep = D_ATTN // V7X_LANES
    cos_f = jnp.concatenate([cos] * rep, axis=1)
    sin_lo = jnp.concatenate([sin * sel_lo_ref[...]] * rep, axis=1)
    sin_hi = jnp.concatenate([sin * sel_hi_ref[...]] * rep, axis=1)
    half = ROT_DIM // 2

    def rope(x):
        return (x * cos_f + pltpu.roll(x, D_ATTN - half, 1) * sin_lo
                + pltpu.roll(x, half, 1) * sin_hi)

    q = rope(_dot(u, win_ref[:, D_POOL:D_POOL + D_ATTN])) * (HEAD_DIM ** -0.5)
    k = rope(_dot(u, win_ref[:, D_POOL + D_ATTN:D_POOL + 2 * D_ATTN]))
    v = _dot(u, win_ref[:, D_POOL + 2 * D_ATTN:D_POOL + 3 * D_ATTN])
    kf_ref[...] = k
    vf_ref[...] = v
    kb_ref[...] = k.astype(BF16)
    if transposed:
        comp = lax.broadcasted_iota(jnp.int32, q.shape, 1) % (2 * HEAD_DIM)
        q_top = jnp.where(comp < HEAD_DIM, q, 0.0)
        q_bot = jnp.where(comp >= HEAD_DIM, q, 0.0)
        q_ref[0, :, 0:ROW_TILE] = q_top.T.astype(BF16)
        q_ref[0, :, ROW_TILE:2 * ROW_TILE] = q_bot.T.astype(BF16)
        v_ref[0] = v.T.astype(BF16)
    else:
        q_ref[...] = q
        v_ref[...] = v.astype(BF16)

    g0 = D_POOL + 3 * D_ATTN
    for c in range(2):
        z = _dot(u, win_ref[:, g0 + c * D_MODEL:g0 + (c + 1) * D_MODEL])
        gates_ref[:, c * D_MODEL:(c + 1) * D_MODEL] = jax.nn.sigmoid(z).astype(BF16)


def _mix_in(h, mix_pre_g, w_in, tile_pos0, row_pos, transposed):
    rows = h.shape[0]
    nt = rows // ROW_TILE
    tables = _rope_tables(tile_pos0, row_pos)
    row = lambda w: pl.BlockSpec((ROW_TILE, w), lambda i: (i, 0))
    if transposed:
        q_shape = jax.ShapeDtypeStruct((nt, D_ATTN, 2 * ROW_TILE), BF16)
        q_spec = pl.BlockSpec((1, D_ATTN, 2 * ROW_TILE), lambda i: (i, 0, 0))
        v_shape = jax.ShapeDtypeStruct((nt, D_ATTN, ROW_TILE), BF16)
        v_spec = pl.BlockSpec((1, D_ATTN, ROW_TILE), lambda i: (i, 0, 0))
    else:
        q_shape, q_spec = jax.ShapeDtypeStruct((rows, D_ATTN), F32), row(D_ATTN)
        v_shape, v_spec = jax.ShapeDtypeStruct((rows, D_ATTN), BF16), row(D_ATTN)
    return pl.pallas_call(
        functools.partial(_mix_in_kernel, transposed=transposed),
        grid=(nt,),
        in_specs=[row(D_MODEL), _const_spec((1, D_MODEL)), _const_spec(w_in.shape),
                  pl.BlockSpec((1, 1, V7X_LANES), lambda i: (i, 0, 0)),
                  pl.BlockSpec((1, 1, V7X_LANES), lambda i: (i, 0, 0)),
                  _const_spec((ROW_TILE, V7X_LANES)), _const_spec((ROW_TILE, V7X_LANES)),
                  _const_spec((1, V7X_LANES)), _const_spec((1, V7X_LANES))],
        out_specs=[row(D_POOL), row(D_ATTN), row(D_ATTN), row(2 * D_MODEL),
                   q_spec, row(D_ATTN), v_spec],
        out_shape=[jax.ShapeDtypeStruct((rows, D_POOL), F32),
                   jax.ShapeDtypeStruct((rows, D_ATTN), F32),
                   jax.ShapeDtypeStruct((rows, D_ATTN), F32),
                   jax.ShapeDtypeStruct((rows, 2 * D_MODEL), BF16),
                   q_shape,
                   jax.ShapeDtypeStruct((rows, D_ATTN), BF16),
                   v_shape],
        compiler_params=pltpu.CompilerParams(
            dimension_semantics=("arbitrary",), vmem_limit_bytes=_mib(48)),
        name="mix_in",
    )(h, mix_pre_g, w_in, *tables)


def _lambda(lam_ref):
    lq1, lk1, lq2, lk2 = (lam_ref[i:i + 1, :] for i in range(4))
    return (jnp.exp(jnp.sum(lq1 * lk1, axis=1, keepdims=True))
            - jnp.exp(jnp.sum(lq2 * lk2, axis=1, keepdims=True)) + LAM_INIT)


def _attn_prompt_kernel(lam_ref, g_ref, q_ref, k_ref, v_ref, o_ref, acc_ref):
    tb = ROW_TILE
    i = pl.program_id(1)
    w = q_ref[0]
    acc_ref[...] = jnp.zeros_like(acc_ref)

    def block(j, m, l, masked):
        kj = k_ref[pl.ds(pl.multiple_of(j * tb, tb), tb), :]
        s = _dot(kj, w)
        if masked:
            k_chunk = lax.broadcasted_iota(jnp.int32, s.shape, 0) // CHUNK
            q_chunk = (lax.broadcasted_iota(jnp.int32, s.shape, 1) % tb) // CHUNK
            s = jnp.where(k_chunk <= q_chunk, s, -jnp.inf)
        m_new = jnp.maximum(m, jnp.max(s, axis=0, keepdims=True))
        alpha = jnp.exp(m - m_new)
        e = jnp.exp(s - m_new)
        l_new = alpha * l + jnp.sum(e, axis=0, keepdims=True)
        acc_ref[...] = acc_ref[...] * alpha + _dot(v_ref[j], e.astype(BF16))
        return m_new, l_new

    m0 = jnp.full((1, 2 * tb), -jnp.inf, F32)
    l0 = jnp.zeros((1, 2 * tb), F32)
    m, l = lax.fori_loop(0, i, lambda j, c: block(j, c[0], c[1], False), (m0, l0))
    m, l = block(i, m, l, True)

    o = acc_ref[...] * (1.0 / l)
    o = o[:, 0:tb] - _lambda(lam_ref) * o[:, tb:2 * tb]
    ms = jnp.mean(o * o, axis=0, keepdims=True)
    o_ref[...] = o * lax.rsqrt(ms + EPS) * g_ref[...] * (1.0 - LAM_INIT)


def _attn_prompt(lam, subln_col, q_t, k_b, v_t):
    nt = q_t.shape[0]
    rows = nt * ROW_TILE
    hd2 = 2 * HEAD_DIM
    return pl.pallas_call(
        _attn_prompt_kernel,
        grid=(H_DIFF, nt),
        in_specs=[pl.BlockSpec((4, HEAD_DIM), lambda h, i: (0, 0)),
                  pl.BlockSpec((hd2, 1), lambda h, i: (0, 0)),
                  pl.BlockSpec((1, hd2, 2 * ROW_TILE), lambda h, i: (i, h, 0)),
                  pl.BlockSpec((rows, hd2), lambda h, i: (0, h)),
                  pl.BlockSpec((nt, hd2, ROW_TILE), lambda h, i: (0, h, 0))],
        out_specs=pl.BlockSpec((hd2, ROW_TILE), lambda h, i: (h, i)),
        out_shape=jax.ShapeDtypeStruct((D_ATTN, rows), F32),
        scratch_shapes=[pltpu.VMEM((hd2, 2 * ROW_TILE), F32)],
        compiler_params=pltpu.CompilerParams(
            dimension_semantics=("arbitrary", "arbitrary"), vmem_limit_bytes=_mib(48)),
        name="attn_prompt",
    )(lam, subln_col, q_t, k_b, v_t)


def _attn_decode_kernel(lam_ref, g_ref, q_ref, ck_ref, cv_ref, kn_ref, vn_ref, o_ref):
    lam = _lambda(lam_ref)
    hd2 = 2 * HEAD_DIM
    nt = (((1,), (1,)), ((), ()))
    for h in range(H_DIFF):
        sl = slice(h * hd2, (h + 1) * hd2)
        q = q_ref[0, :, sl]
        t = q.shape[0]
        comp = lax.broadcasted_iota(jnp.int32, q.shape, 1)
        lhs = jnp.concatenate([jnp.where(comp < HEAD_DIM, q, 0.0),
                               jnp.where(comp >= HEAD_DIM, q, 0.0)], axis=0).astype(BF16)
        s_c = lax.dot_general(lhs, ck_ref[0, :, sl].astype(BF16), nt, preferred_element_type=F32)
        s_n = lax.dot_general(lhs, kn_ref[0, :, sl].astype(BF16), nt, preferred_element_type=F32)
        m = jnp.maximum(jnp.max(s_c, axis=1, keepdims=True), jnp.max(s_n, axis=1, keepdims=True))
        e_c = jnp.exp(s_c - m)
        e_n = jnp.exp(s_n - m)
        inv_l = 1.0 / (jnp.sum(e_c, axis=1, keepdims=True) + jnp.sum(e_n, axis=1, keepdims=True))
        p_c = e_c * inv_l
        p_n = e_n * inv_l
        a_c = (p_c[0:t] - lam * p_c[t:2 * t]).astype(BF16)
        a_n = (p_n[0:t] - lam * p_n[t:2 * t]).astype(BF16)
        o = _dot(a_c, cv_ref[0, :, sl].astype(BF16)) + _dot(a_n, vn_ref[0, :, sl].astype(BF16))
        o_ref[0, :, sl] = _rms(o, g_ref[...]) * (1.0 - LAM_INIT)


def _attn_decode(lam, subln_row, q, cache_k, cache_v, k_new, v_new):
    b, past, _ = cache_k.shape
    t = q.shape[1]
    seq = lambda n: pl.BlockSpec((1, n, D_ATTN), lambda i: (i, 0, 0))
    return pl.pallas_call(
        _attn_decode_kernel,
        grid=(b,),
        in_specs=[pl.BlockSpec((4, HEAD_DIM), lambda i: (0, 0)),
                  pl.BlockSpec((1, 2 * HEAD_DIM), lambda i: (0, 0)),
                  seq(t), seq(past), seq(past), seq(t), seq(t)],
        out_specs=seq(t),
        out_shape=jax.ShapeDtypeStruct((b, t, D_ATTN), F32),
        compiler_params=pltpu.CompilerParams(
            dimension_semantics=("arbitrary",), vmem_limit_bytes=_mib(56)),
        name="attn_decode",
    )(lam, subln_row, q, cache_k, cache_v, k_new, v_new)


def _merge_kernel(h_ref, attn_ref, gates_ref, halo_ref, body_ref, poolw_ref, pscale_ref,
                  wbp_ref, wba_ref, wout_ref, postg_ref, o_ref, *, hist0, attn_transposed):
    nb, tl, _ = body_ref.shape
    hist = hist0 + pl.program_id(0) * tl
    hi = hist + 1 + lax.broadcasted_iota(jnp.int32, (tl, 1), 0)
    halo_row = lax.broadcasted_iota(jnp.int32, (POOL_HALO, 1), 0)
    halo_ok = halo_row >= POOL_HALO - jnp.minimum(hist, POOL_STATE)
    pooled = []
    for b in range(nb):
        halo = jnp.where(halo_ok, halo_ref[b], 0.0)
        ext = jnp.concatenate([halo, body_ref[b]], axis=0)
        groups = []
        for g, win in enumerate(POOL_WINDOWS):
            x = ext[:, g * POOL_GROUP:(g + 1) * POOL_GROUP]
            s, shift = x, 1
            while shift < win:
                s = s + pltpu.roll(s, shift, 0)
                shift *= 2
            inv_cnt = 1.0 / jnp.minimum(hi, win).astype(F32)
            mixed = s[POOL_HALO:] * inv_cnt - x[POOL_HALO:]
            groups.append(_dot(mixed.astype(BF16), poolw_ref[g]))
        pooled.append(jnp.concatenate(groups, axis=1) * pscale_ref[...])
    pool_out = pooled[0] if nb == 1 else jnp.concatenate(pooled, axis=0)

    attn = attn_ref[...].T if attn_transposed else attn_ref[...]
    branch_pool = _dot(pool_out.astype(BF16), wbp_ref[...])
    branch_attn = _dot(attn.astype(BF16), wba_ref[...])
    merged = (gates_ref[:, 0:D_MODEL].astype(F32) * branch_pool
              + gates_ref[:, D_MODEL:2 * D_MODEL].astype(F32) * branch_attn)
    o_ref[...] = h_ref[...] + _rms(_dot(merged.astype(BF16), wout_ref[...]), postg_ref[...])


def _merge(h, attn, gates, halo, body, halo_spec, body_spec, hist0, attn_transposed,
           pool_w, pool_scale, w_bp, w_ba, w_out, post_g):
    rows = h.shape[0]
    row = lambda w: pl.BlockSpec((ROW_TILE, w), lambda i: (i, 0))
    attn_spec = (pl.BlockSpec((D_ATTN, ROW_TILE), lambda i: (0, i)) if attn_transposed
                 else row(D_ATTN))
    return pl.pallas_call(
        functools.partial(_merge_kernel, hist0=hist0, attn_transposed=attn_transposed),
        grid=(rows // ROW_TILE,),
        in_specs=[row(D_MODEL), attn_spec, row(2 * D_MODEL), halo_spec, body_spec,
                  _const_spec(pool_w.shape), _const_spec(pool_scale.shape),
                  _const_spec(w_bp.shape), _const_spec(w_ba.shape), _const_spec(w_out.shape),
                  _const_spec(post_g.shape)],
        out_specs=row(D_MODEL),
        out_shape=jax.ShapeDtypeStruct((rows, D_MODEL), F32),
        compiler_params=pltpu.CompilerParams(
            dimension_semantics=("arbitrary",), vmem_limit_bytes=_mib(48)),
        name="merge",
    )(h, attn, gates, halo, body, pool_w, pool_scale, w_bp, w_ba, w_out, post_g)


def kernel(x_prompt, x_sample, cache_k, cache_v, state_pool, ffn1_pre_g, ffn1_post_g, ffn1_w_gu, ffn1_w_down, mix_pre_g, mix_post_g, w_in, pool_w, pool_scale, lambda_q1, lambda_k1, lambda_q2, lambda_k2, subln_g, w_branch_pool, w_branch_attn, w_out, ffn2_pre_g, ffn2_post_g, ffn2_w_gu, ffn2_w_down):
    bp, seq, _ = x_prompt.shape
    db, dseq, _ = x_sample.shape
    past = cache_k.shape[2]
    assert bp == 1 and seq % ROW_TILE == 0 and db * dseq == ROW_TILE and dseq >= POOL_STATE
    assert (past + dseq - 1) // CHUNK <= past // CHUNK and past % CHUNK == 0

    l = 0
    bf = lambda w: w[l].astype(BF16)
    g1pre, g1post, gmpre, gmpost = ffn1_pre_g[l:l + 1], ffn1_post_g[l:l + 1], mix_pre_g[l:l + 1], mix_post_g[l:l + 1]
    g2pre, g2post = ffn2_pre_g[l:l + 1], ffn2_post_g[l:l + 1]
    w1gu, w1d, w2gu, w2d = bf(ffn1_w_gu), bf(ffn1_w_down), bf(ffn2_w_gu), bf(ffn2_w_down)
    win, wpool, wbp, wba, wo = bf(w_in), bf(pool_w), bf(w_branch_pool), bf(w_branch_attn), bf(w_out)
    pscale = pool_scale[l:l + 1]
    lam = jnp.concatenate([lambda_q1[l:l + 1], lambda_k1[l:l + 1],
                           lambda_q2[l:l + 1], lambda_k2[l:l + 1]], axis=0)
    subln_row = subln_g[l:l + 1]
    subln_col = subln_g[l][:, None]
    merge_w = (wpool, pscale, wbp, wba, wo, gmpost)

    xp = x_prompt.reshape(seq, D_MODEL)
    nt = seq // ROW_TILE
    hp = _ffn(xp, g1pre, g1post, w1gu, w1d)
    upool, kf, vf, gates, q_t, k_b, v_t = _mix_in(
        hp, gmpre, win, np.arange(nt) * ROW_TILE, np.arange(ROW_TILE), transposed=True)
    attn_t = _attn_prompt(lam, subln_col, q_t, k_b, v_t)
    per_tile = ROW_TILE // POOL_HALO
    halo_spec = pl.BlockSpec((1, POOL_HALO, D_POOL),
                             lambda i: (jnp.maximum(i * per_tile - 1, 0), 0, 0))
    body_spec = pl.BlockSpec((1, ROW_TILE, D_POOL), lambda i: (i, 0, 0))
    hp = _merge(hp, attn_t, gates, upool.reshape(seq // POOL_HALO, POOL_HALO, D_POOL),
                upool.reshape(nt, ROW_TILE, D_POOL), halo_spec, body_spec, 0, True, *merge_w)
    y_prompt = _ffn(hp, g2pre, g2post, w2gu, w2d).reshape(bp, seq, D_MODEL)
    new_k_p = kf.reshape(1, bp, seq, H_DIFF, 2, HEAD_DIM)
    new_v_p = vf.reshape(1, bp, seq, H_DIFF, 2 * HEAD_DIM)
    new_pool_p = upool[seq - POOL_STATE:].reshape(1, bp, POOL_STATE, D_POOL)

    xs = x_sample.reshape(db * dseq, D_MODEL)
    hs = _ffn(xs, g1pre, g1post, w1gu, w1d)
    upool_s, kf_s, vf_s, gates_s, q_s, _, _ = _mix_in(
        hs, gmpre, win, np.array([past]), np.arange(ROW_TILE) % dseq, transposed=False)
    attn_s = _attn_decode(lam, subln_row, q_s.reshape(db, dseq, D_ATTN),
                          cache_k[l].reshape(db, past, D_ATTN), cache_v[l].reshape(db, past, D_ATTN),
                          kf_s.reshape(db, dseq, D_ATTN), vf_s.reshape(db, dseq, D_ATTN))
    hist = state_pool[l]
    halo_s = jnp.concatenate(
        [jnp.zeros((db, POOL_HALO - POOL_STATE, D_POOL), F32), hist], axis=1)
    body_s = upool_s.reshape(db, dseq, D_POOL)
    hs = _merge(hs, attn_s.reshape(db * dseq, D_ATTN), gates_s, halo_s, body_s,
                pl.BlockSpec((db, POOL_HALO, D_POOL), lambda i: (0, 0, 0)),
                pl.BlockSpec((db, dseq, D_POOL), lambda i: (0, 0, 0)),
                POOL_STATE, False, *merge_w)
    y_sample = _ffn(hs, g2pre, g2post, w2gu, w2d).reshape(db, dseq, D_MODEL)
    new_k_s = kf_s.reshape(1, db, dseq, H_DIFF, 2, HEAD_DIM)
    new_v_s = vf_s.reshape(1, db, dseq, H_DIFF, 2 * HEAD_DIM)
    new_pool_s = body_s[:, dseq - POOL_STATE:].reshape(1, db, POOL_STATE, D_POOL)

    return (y_prompt, y_sample, new_k_p, new_v_p, new_pool_p, new_k_s, new_v_s, new_pool_s)
```

```python
import functools
import math

import numpy as np
import jax
import jax.numpy as jnp
from jax import lax
from jax.experimental import pallas as pl
from jax.experimental.pallas import tpu as pltpu

D_MODEL = 1024
D_FF = 2816
D_POOL = 512
POOL_WINDOWS = (2, 4, 8, 16)
POOL_GROUP = 128
POOL_STATE = 15
H_DIFF = 4
HEAD_DIM = 64
D_ATTN = 512
ROT_DIM = 16
ROPE_THETA = 500000.0
CHUNK = 64
EPS = 1e-6
LAM_INIT = 0.8 - 0.6 * math.exp(-0.3 * 0)

V7X_LANES = 128
POOL_HALO = 16
ROW_TILE = 512
FF_CHUNKS = ((0, 1024), (1024, 1024), (2048, 768))
ATTN_STRIP = 256
ATTN_HEADS_PER_STEP = 2
QK_SCALE_LOG2 = HEAD_DIM ** -0.5 * math.log2(math.e)

F32 = jnp.float32
BF16 = jnp.bfloat16


def _mib(n):
    return int(n * 1024 * 1024)


def _rms(x, g):
    return x * lax.rsqrt(jnp.mean(x * x, axis=-1, keepdims=True) + EPS) * g


def _dot(a, b):
    return jnp.dot(a, b, preferred_element_type=F32)


def _const_spec(shape):
    nd = len(shape)
    return pl.BlockSpec(shape, lambda *_: (0,) * nd, pipeline_mode=pl.Buffered(1))


def _ffn_kernel(x_ref, pre_ref, post_ref, wgu_ref, wd_ref, o_ref):
    x = x_ref[...]
    xn = _rms(x, pre_ref[...]).astype(BF16)
    acc = None
    for c0, cw in FF_CHUNKS:
        gate = _dot(xn, wgu_ref[:, c0:c0 + cw])
        up = _dot(xn, wgu_ref[:, D_FF + c0:D_FF + c0 + cw])
        act = (gate * jax.nn.sigmoid(gate) * up).astype(BF16)
        part = _dot(act, wd_ref[c0:c0 + cw, :])
        acc = part if acc is None else acc + part
    o_ref[...] = x + 0.5 * _rms(acc, post_ref[...])


def _ffn(x, pre_g, post_g, w_gu, w_down):
    rows = x.shape[0]
    row_spec = pl.BlockSpec((ROW_TILE, D_MODEL), lambda i: (i, 0))
    return pl.pallas_call(
        _ffn_kernel,
        grid=(rows // ROW_TILE,),
        in_specs=[row_spec, _const_spec((1, D_MODEL)), _const_spec((1, D_MODEL)),
                  _const_spec((D_MODEL, 2 * D_FF)), _const_spec((D_FF, D_MODEL))],
        out_specs=row_spec,
        out_shape=jax.ShapeDtypeStruct((rows, D_MODEL), F32),
        compiler_params=pltpu.CompilerParams(
            dimension_semantics=("arbitrary",), vmem_limit_bytes=_mib(48)),
        name="ffn",
    )(x, pre_g, post_g, w_gu, w_down)


def _rope_tables(tile_pos0, row_pos):
    d = np.arange(V7X_LANES) % HEAD_DIM
    inv = np.where(d < ROT_DIM, ROPE_THETA ** (-(d % (ROT_DIM // 2)) / (ROT_DIM // 2)), 0.0)
    ang_a = np.asarray(tile_pos0, np.float64)[:, None] * inv[None, :]
    ang_b = np.asarray(row_pos, np.float64)[:, None] * inv[None, :]
    half = ROT_DIM // 2
    sel_lo = np.where(d < half, -1.0, 0.0)[None, :]
    sel_hi = np.where((d >= half) & (d < ROT_DIM), 1.0, 0.0)[None, :]
    f = lambda a: jnp.asarray(a, F32)
    return (f(np.cos(ang_a))[:, None, :], f(np.sin(ang_a))[:, None, :],
            f(np.cos(ang_b)), f(np.sin(ang_b)), f(sel_lo), f(sel_hi))


def _mix_in_kernel(h_ref, g_ref, win_ref, cos_a_ref, sin_a_ref, cos_b_ref, sin_b_ref,
                   sel_lo_ref, sel_hi_ref,
                   upool_ref, kf_ref, vf_ref, gates_ref, q_ref, kb_ref, v_ref, *, transposed):
    u = _rms(h_ref[...], g_ref[...]).astype(BF16)
    upool_ref[...] = _dot(u, win_ref[:, 0:D_POOL])

    ca, sa = cos_a_ref[0], sin_a_ref[0]
    cb, sb = cos_b_ref[...], sin_b_ref[...]
    cos = ca * cb - sa * sb
    sin = sa * cb + ca * sb
    rep = D_ATTN // V7X_LANES
    cos_f = jnp.concatenate([cos] * rep, axis=1)
    sin_lo = jnp.concatenate([sin * sel_lo_ref[...]] * rep, axis=1)
    sin_hi = jnp.concatenate([sin * sel_hi_ref[...]] * rep, axis=1)
    half = ROT_DIM // 2

    def rope(x):
        return (x * cos_f + pltpu.roll(x, D_ATTN - half, 1) * sin_lo
                + pltpu.roll(x, half, 1) * sin_hi)

    q = rope(_dot(u, win_ref[:, D_POOL:D_POOL + D_ATTN])) * QK_SCALE_LOG2
    k = rope(_dot(u, win_ref[:, D_POOL + D_ATTN:D_POOL + 2 * D_ATTN]))
    v = _dot(u, win_ref[:, D_POOL + 2 * D_ATTN:D_POOL + 3 * D_ATTN])
    kf_ref[...] = k
    vf_ref[...] = v
    kb_ref[...] = k.astype(BF16)
    if transposed:
        comp = lax.broadcasted_iota(jnp.int32, q.shape, 1) % (2 * HEAD_DIM)
        q_top = jnp.where(comp < HEAD_DIM, q, 0.0)
        q_bot = jnp.where(comp >= HEAD_DIM, q, 0.0)
        q_ref[0, :, 0:ROW_TILE] = q_top.T.astype(BF16)
        q_ref[0, :, ROW_TILE:2 * ROW_TILE] = q_bot.T.astype(BF16)
        v_ref[0] = v.T.astype(BF16)
    else:
        q_ref[...] = q
        v_ref[...] = v.astype(BF16)

    g0 = D_POOL + 3 * D_ATTN
    for c in range(2):
        z = _dot(u, win_ref[:, g0 + c * D_MODEL:g0 + (c + 1) * D_MODEL])
        gates_ref[:, c * D_MODEL:(c + 1) * D_MODEL] = jax.nn.sigmoid(z).astype(BF16)


def _mix_in(h, mix_pre_g, w_in, tile_pos0, row_pos, transposed):
    rows = h.shape[0]
    nt = rows // ROW_TILE
    tables = _rope_tables(tile_pos0, row_pos)
    row = lambda w: pl.BlockSpec((ROW_TILE, w), lambda i: (i, 0))
    if transposed:
        q_shape = jax.ShapeDtypeStruct((nt, D_ATTN, 2 * ROW_TILE), BF16)
        q_spec = pl.BlockSpec((1, D_ATTN, 2 * ROW_TILE), lambda i: (i, 0, 0))
        v_shape = jax.ShapeDtypeStruct((nt, D_ATTN, ROW_TILE), BF16)
        v_spec = pl.BlockSpec((1, D_ATTN, ROW_TILE), lambda i: (i, 0, 0))
    else:
        q_shape, q_spec = jax.ShapeDtypeStruct((rows, D_ATTN), F32), row(D_ATTN)
        v_shape, v_spec = jax.ShapeDtypeStruct((rows, D_ATTN), BF16), row(D_ATTN)
    return pl.pallas_call(
        functools.partial(_mix_in_kernel, transposed=transposed),
        grid=(nt,),
        in_specs=[row(D_MODEL), _const_spec((1, D_MODEL)), _const_spec(w_in.shape),
                  pl.BlockSpec((1, 1, V7X_LANES), lambda i: (i, 0, 0)),
                  pl.BlockSpec((1, 1, V7X_LANES), lambda i: (i, 0, 0)),
                  _const_spec((ROW_TILE, V7X_LANES)), _const_spec((ROW_TILE, V7X_LANES)),
                  _const_spec((1, V7X_LANES)), _const_spec((1, V7X_LANES))],
        out_specs=[row(D_POOL), row(D_ATTN), row(D_ATTN), row(2 * D_MODEL),
                   q_spec, row(D_ATTN), v_spec],
        out_shape=[jax.ShapeDtypeStruct((rows, D_POOL), F32),
                   jax.ShapeDtypeStruct((rows, D_ATTN), F32),
                   jax.ShapeDtypeStruct((rows, D_ATTN), F32),
                   jax.ShapeDtypeStruct((rows, 2 * D_MODEL), BF16),
                   q_shape,
                   jax.ShapeDtypeStruct((rows, D_ATTN), BF16),
                   v_shape],
        compiler_params=pltpu.CompilerParams(
            dimension_semantics=("arbitrary",), vmem_limit_bytes=_mib(48)),
        name="mix_in",
    )(h, mix_pre_g, w_in, *tables)


def _lambda(lam_ref):
    lq1, lk1, lq2, lk2 = (lam_ref[i:i + 1, :] for i in range(4))
    return (jnp.exp(jnp.sum(lq1 * lk1, axis=1, keepdims=True))
            - jnp.exp(jnp.sum(lq2 * lk2, axis=1, keepdims=True)) + LAM_INIT)


def _attn_prompt_kernel(lam_ref, g_ref, q_ref, k_ref, v_ref, o_ref, acc_ref, s_ref):
    tb = ROW_TILE
    hd2 = 2 * HEAD_DIM
    i = pl.program_id(1)
    acc_ref[...] = jnp.zeros_like(acc_ref)
    heads = [slice(h * hd2, (h + 1) * hd2) for h in range(ATTN_HEADS_PER_STEP)]
    strips = [slice(c, c + ATTN_STRIP) for c in range(0, 2 * tb, ATTN_STRIP)]

    def logits(j, rows, cols):
        kj = k_ref[pl.ds(pl.multiple_of(j * tb, tb), tb), rows]
        return _dot(kj, q_ref[0, rows, cols])

    def consume(j, m, l, masked, prefetch):
        m = [list(x) for x in m]
        l = [list(x) for x in l]
        for n, cols in enumerate(strips):
            for h, rows in enumerate(heads):
                if prefetch:
                    s_next = logits(j + 1, rows, cols)

                def s_strip():
                    s = s_ref[h, :, cols]
                    if masked:
                        k_chunk = lax.broadcasted_iota(jnp.int32, s.shape, 0) // CHUNK
                        q_pos = cols.start + lax.broadcasted_iota(jnp.int32, s.shape, 1)
                        s = jnp.where(k_chunk <= (q_pos % tb) // CHUNK, s, -jnp.inf)
                    return s
                m_new = jnp.maximum(m[h][n], jnp.max(s_strip(), axis=0, keepdims=True))
                alpha = jnp.exp2(m[h][n] - m_new)
                e = jnp.exp2(s_strip() - m_new)
                l[h][n] = alpha * l[h][n] + jnp.sum(e, axis=0, keepdims=True)
                m[h][n] = m_new
                acc_ref[rows, cols] = (acc_ref[rows, cols] * alpha
                                       + _dot(v_ref[j, rows, :], e.astype(BF16)))
                if prefetch:
                    s_ref[h, :, cols] = s_next
        return m, l

    m0 = [[jnp.full((1, ATTN_STRIP), -jnp.inf, F32) for _ in strips] for _ in heads]
    l0 = [[jnp.zeros((1, ATTN_STRIP), F32) for _ in strips] for _ in heads]
    for h, rows in enumerate(heads):
        for cols in strips:
            s_ref[h, :, cols] = logits(0, rows, cols)
    m, l = lax.fori_loop(0, i, lambda j, c: consume(j, c[0], c[1], False, True), (m0, l0))
    m, l = consume(i, m, l, True, False)

    lam = _lambda(lam_ref)
    for h, rows in enumerate(heads):
        o = acc_ref[rows, :] * (1.0 / jnp.concatenate(l[h], axis=1))
        o = o[:, 0:tb] - lam * o[:, tb:2 * tb]
        ms = jnp.mean(o * o, axis=0, keepdims=True)
        o_ref[rows, :] = o * lax.rsqrt(ms + EPS) * g_ref[...] * (1.0 - LAM_INIT)


def _attn_prompt(lam, subln_col, q_t, k_b, v_t):
    nt = q_t.shape[0]
    rows = nt * ROW_TILE
    hd2 = 2 * HEAD_DIM
    gw = ATTN_HEADS_PER_STEP * hd2
    resident = dict(pipeline_mode=pl.Buffered(1))
    return pl.pallas_call(
        _attn_prompt_kernel,
        grid=(H_DIFF // ATTN_HEADS_PER_STEP, nt),
        in_specs=[pl.BlockSpec((4, HEAD_DIM), lambda h, i: (0, 0)),
                  pl.BlockSpec((hd2, 1), lambda h, i: (0, 0)),
                  pl.BlockSpec((1, gw, 2 * ROW_TILE), lambda h, i: (i, h, 0)),
                  pl.BlockSpec((rows, gw), lambda h, i: (0, h), **resident),
                  pl.BlockSpec((nt, gw, ROW_TILE), lambda h, i: (0, h, 0), **resident)],
        out_specs=pl.BlockSpec((gw, ROW_TILE), lambda h, i: (h, i)),
        out_shape=jax.ShapeDtypeStruct((D_ATTN, rows), F32),
        scratch_shapes=[pltpu.VMEM((gw, 2 * ROW_TILE), F32),
                        pltpu.VMEM((ATTN_HEADS_PER_STEP, ROW_TILE, 2 * ROW_TILE), F32)],
        compiler_params=pltpu.CompilerParams(
            dimension_semantics=("arbitrary", "arbitrary"), vmem_limit_bytes=_mib(48)),
        name="attn_prompt",
    )(lam, subln_col, q_t, k_b, v_t)


def _attn_decode_kernel(lam_ref, g_ref, q_ref, ck_ref, cv_ref, kn_ref, vn_ref, o_ref):
    lam = _lambda(lam_ref)
    hd2 = 2 * HEAD_DIM
    nt = (((1,), (1,)), ((), ()))
    for h in range(H_DIFF):
        sl = slice(h * hd2, (h + 1) * hd2)
        q = q_ref[0, :, sl]
        t = q.shape[0]
        comp = lax.broadcasted_iota(jnp.int32, q.shape, 1)
        lhs = jnp.concatenate([jnp.where(comp < HEAD_DIM, q, 0.0),
                               jnp.where(comp >= HEAD_DIM, q, 0.0)], axis=0).astype(BF16)
        s_c = lax.dot_general(lhs, ck_ref[0, :, sl].astype(BF16), nt, preferred_element_type=F32)
        s_n = lax.dot_general(lhs, kn_ref[0, :, sl].astype(BF16), nt, preferred_element_type=F32)
        m = jnp.maximum(jnp.max(s_c, axis=1, keepdims=True), jnp.max(s_n, axis=1, keepdims=True))
        e_c = jnp.exp2(s_c - m)
        e_n = jnp.exp2(s_n - m)
        inv_l = 1.0 / (jnp.sum(e_c, axis=1, keepdims=True) + jnp.sum(e_n, axis=1, keepdims=True))
        p_c = e_c * inv_l
        p_n = e_n * inv_l
        a_c = (p_c[0:t] - lam * p_c[t:2 * t]).astype(BF16)
        a_n = (p_n[0:t] - lam * p_n[t:2 * t]).astype(BF16)
        o = _dot(a_c, cv_ref[0, :, sl].astype(BF16)) + _dot(a_n, vn_ref[0, :, sl].astype(BF16))
        o_ref[0, :, sl] = _rms(o, g_ref[...]) * (1.0 - LAM_INIT)


def _attn_decode(lam, subln_row, q, cache_k, cache_v, k_new, v_new):
    b, past, _ = cache_k.shape
    t = q.shape[1]
    seq = lambda n: pl.BlockSpec((1, n, D_ATTN), lambda i: (i, 0, 0))
    return pl.pallas_call(
        _attn_decode_kernel,
        grid=(b,),
        in_specs=[pl.BlockSpec((4, HEAD_DIM), lambda i: (0, 0)),
                  pl.BlockSpec((1, 2 * HEAD_DIM), lambda i: (0, 0)),
                  seq(t), seq(past), seq(past), seq(t), seq(t)],
        out_specs=seq(t),
        out_shape=jax.ShapeDtypeStruct((b, t, D_ATTN), F32),
        compiler_params=pltpu.CompilerParams(
            dimension_semantics=("arbitrary",), vmem_limit_bytes=_mib(56)),
        name="attn_decode",
    )(lam, subln_row, q, cache_k, cache_v, k_new, v_new)


def _merge_kernel(h_ref, attn_ref, gates_ref, halo_ref, body_ref, poolw_ref, pscale_ref,
                  wbp_ref, wba_ref, wout_ref, postg_ref, o_ref, *, hist0, attn_transposed):
    nb, tl, _ = body_ref.shape
    hist = hist0 + pl.program_id(0) * tl
    hi = hist + 1 + lax.broadcasted_iota(jnp.int32, (tl, 1), 0)
    halo_row = lax.broadcasted_iota(jnp.int32, (POOL_HALO, 1), 0)
    halo_ok = halo_row >= POOL_HALO - jnp.minimum(hist, POOL_STATE)
    pooled = []
    for b in range(nb):
        halo = jnp.where(halo_ok, halo_ref[b], 0.0)
        ext = jnp.concatenate([halo, body_ref[b]], axis=0)
        groups = []
        for g, win in enumerate(POOL_WINDOWS):
            x = ext[:, g * POOL_GROUP:(g + 1) * POOL_GROUP]
            s, shift = x, 1
            while shift < win:
                s = s + pltpu.roll(s, shift, 0)
                shift *= 2
            inv_cnt = 1.0 / jnp.minimum(hi, win).astype(F32)
            mixed = s[POOL_HALO:] * inv_cnt - x[POOL_HALO:]
            groups.append(_dot(mixed.astype(BF16), poolw_ref[g]))
        pooled.append(jnp.concatenate(groups, axis=1) * pscale_ref[...])
    pool_out = pooled[0] if nb == 1 else jnp.concatenate(pooled, axis=0)

    attn = attn_ref[...].T if attn_transposed else attn_ref[...]
    branch_pool = _dot(pool_out.astype(BF16), wbp_ref[...])
    branch_attn = _dot(attn.astype(BF16), wba_ref[...])
    merged = (gates_ref[:, 0:D_MODEL].astype(F32) * branch_pool
              + gates_ref[:, D_MODEL:2 * D_MODEL].astype(F32) * branch_attn)
    o_ref[...] = h_ref[...] + _rms(_dot(merged.astype(BF16), wout_ref[...]), postg_ref[...])


def _merge(h, attn, gates, halo, body, halo_spec, body_spec, hist0, attn_transposed,
           pool_w, pool_scale, w_bp, w_ba, w_out, post_g):
    rows = h.shape[0]
    row = lambda w: pl.BlockSpec((ROW_TILE, w), lambda i: (i, 0))
    attn_spec = (pl.BlockSpec((D_ATTN, ROW_TILE), lambda i: (0, i)) if attn_transposed
                 else row(D_ATTN))
    return pl.pallas_call(
        functools.partial(_merge_kernel, hist0=hist0, attn_transposed=attn_transposed),
        grid=(rows // ROW_TILE,),
        in_specs=[row(D_MODEL), attn_spec, row(2 * D_MODEL), halo_spec, body_spec,
                  _const_spec(pool_w.shape), _const_spec(pool_scale.shape),
                  _const_spec(w_bp.shape), _const_spec(w_ba.shape), _const_spec(w_out.shape),
                  _const_spec(post_g.shape)],
        out_specs=row(D_MODEL),
        out_shape=jax.ShapeDtypeStruct((rows, D_MODEL), F32),
        compiler_params=pltpu.CompilerParams(
            dimension_semantics=("arbitrary",), vmem_limit_bytes=_mib(48)),
        name="merge",
    )(h, attn, gates, halo, body, pool_w, pool_scale, w_bp, w_ba, w_out, post_g)


def kernel(x_prompt, x_sample, cache_k, cache_v, state_pool, ffn1_pre_g, ffn1_post_g, ffn1_w_gu, ffn1_w_down, mix_pre_g, mix_post_g, w_in, pool_w, pool_scale, lambda_q1, lambda_k1, lambda_q2, lambda_k2, subln_g, w_branch_pool, w_branch_attn, w_out, ffn2_pre_g, ffn2_post_g, ffn2_w_gu, ffn2_w_down):
    bp, seq, _ = x_prompt.shape
    db, dseq, _ = x_sample.shape
    past = cache_k.shape[2]
    assert bp == 1 and seq % ROW_TILE == 0 and db * dseq == ROW_TILE and dseq >= POOL_STATE
    assert (past + dseq - 1) // CHUNK <= past // CHUNK and past % CHUNK == 0

    l = 0
    bf = lambda w: w[l].astype(BF16)
    g1pre, g1post, gmpre, gmpost = ffn1_pre_g[l:l + 1], ffn1_post_g[l:l + 1], mix_pre_g[l:l + 1], mix_post_g[l:l + 1]
    g2pre, g2post = ffn2_pre_g[l:l + 1], ffn2_post_g[l:l + 1]
    w1gu, w1d, w2gu, w2d = bf(ffn1_w_gu), bf(ffn1_w_down), bf(ffn2_w_gu), bf(ffn2_w_down)
    win, wpool, wbp, wba, wo = bf(w_in), bf(pool_w), bf(w_branch_pool), bf(w_branch_attn), bf(w_out)
    pscale = pool_scale[l:l + 1]
    lam = jnp.concatenate([lambda_q1[l:l + 1], lambda_k1[l:l + 1],
                           lambda_q2[l:l + 1], lambda_k2[l:l + 1]], axis=0)
    subln_row = subln_g[l:l + 1]
    subln_col = subln_g[l][:, None]
    merge_w = (wpool, pscale, wbp, wba, wo, gmpost)

    xp = x_prompt.reshape(seq, D_MODEL)
    nt = seq // ROW_TILE
    hp = _ffn(xp, g1pre, g1post, w1gu, w1d)
    upool, kf, vf, gates, q_t, k_b, v_t = _mix_in(
        hp, gmpre, win, np.arange(nt) * ROW_TILE, np.arange(ROW_TILE), transposed=True)
    attn_t = _attn_prompt(lam, subln_col, q_t, k_b, v_t)
    per_tile = ROW_TILE // POOL_HALO
    halo_spec = pl.BlockSpec((1, POOL_HALO, D_POOL),
                             lambda i: (jnp.maximum(i * per_tile - 1, 0), 0, 0))
    body_spec = pl.BlockSpec((1, ROW_TILE, D_POOL), lambda i: (i, 0, 0))
    hp = _merge(hp, attn_t, gates, upool.reshape(seq // POOL_HALO, POOL_HALO, D_POOL),
                upool.reshape(nt, ROW_TILE, D_POOL), halo_spec, body_spec, 0, True, *merge_w)
    y_prompt = _ffn(hp, g2pre, g2post, w2gu, w2d).reshape(bp, seq, D_MODEL)
    new_k_p = kf.reshape(1, bp, seq, H_DIFF, 2, HEAD_DIM)
    new_v_p = vf.reshape(1, bp, seq, H_DIFF, 2 * HEAD_DIM)
    new_pool_p = upool[seq - POOL_STATE:].reshape(1, bp, POOL_STATE, D_POOL)

    xs = x_sample.reshape(db * dseq, D_MODEL)
    hs = _ffn(xs, g1pre, g1post, w1gu, w1d)
    upool_s, kf_s, vf_s, gates_s, q_s, _, _ = _mix_in(
        hs, gmpre, win, np.array([past]), np.arange(ROW_TILE) % dseq, transposed=False)
    attn_s = _attn_decode(lam, subln_row, q_s.reshape(db, dseq, D_ATTN),
                          cache_k[l].reshape(db, past, D_ATTN), cache_v[l].reshape(db, past, D_ATTN),
                          kf_s.reshape(db, dseq, D_ATTN), vf_s.reshape(db, dseq, D_ATTN))
    hist = state_pool[l]
    halo_s = jnp.concatenate(
        [jnp.zeros((db, POOL_HALO - POOL_STATE, D_POOL), F32), hist], axis=1)
    body_s = upool_s.reshape(db, dseq, D_POOL)
    hs = _merge(hs, attn_s.reshape(db * dseq, D_ATTN), gates_s, halo_s, body_s,
                pl.BlockSpec((db, POOL_HALO, D_POOL), lambda i: (0, 0, 0)),
                pl.BlockSpec((db, dseq, D_POOL), lambda i: (0, 0, 0)),
                POOL_STATE, False, *merge_w)
    y_sample = _ffn(hs, g2pre, g2post, w2gu, w2d).reshape(db, dseq, D_MODEL)
    new_k_s = kf_s.reshape(1, db, dseq, H_DIFF, 2, HEAD_DIM)
    new_v_s = vf_s.reshape(1, db, dseq, H_DIFF, 2 * HEAD_DIM)
    new_pool_s = body_s[:, dseq - POOL_STATE:].reshape(1, db, POOL_STATE, D_POOL)

    return (y_prompt, y_sample, new_k_p, new_v_p, new_pool_p, new_k_s, new_v_s, new_pool_s)
```

```python
import functools
import math

import numpy as np
import jax
import jax.numpy as jnp
from jax import lax
from jax.experimental import pallas as pl
from jax.experimental.pallas import tpu as pltpu

D_MODEL = 1024
D_FF = 2816
D_POOL = 512
POOL_WINDOWS = (2, 4, 8, 16)
POOL_GROUP = 128
POOL_STATE = 15
H_DIFF = 4
HEAD_DIM = 64
D_ATTN = 512
ROT_DIM = 16
ROPE_THETA = 500000.0
CHUNK = 64
EPS = 1e-6
LAM_INIT = 0.8 - 0.6 * math.exp(-0.3 * 0)

V7X_LANES = 128
POOL_HALO = 16
ROW_TILE = 512
FF_CHUNKS = ((0, 1024), (1024, 1024), (2048, 768))
ATTN_STRIP = 256
ATTN_HEADS_PER_STEP = 4
QK_SCALE_LOG2 = HEAD_DIM ** -0.5 * math.log2(math.e)

F32 = jnp.float32
BF16 = jnp.bfloat16


def _mib(n):
    return int(n * 1024 * 1024)


def _rms(x, g):
    return x * lax.rsqrt(jnp.mean(x * x, axis=-1, keepdims=True) + EPS) * g


def _dot(a, b):
    return jnp.dot(a, b, preferred_element_type=F32)


def _const_spec(shape):
    nd = len(shape)
    return pl.BlockSpec(shape, lambda *_: (0,) * nd, pipeline_mode=pl.Buffered(1))


def _ffn_kernel(x_ref, pre_ref, post_ref, wgu_ref, wd_ref, o_ref):
    x = x_ref[...]
    xn = _rms(x, pre_ref[...]).astype(BF16)
    acc = None
    for c0, cw in FF_CHUNKS:
        gate = _dot(xn, wgu_ref[:, c0:c0 + cw])
        up = _dot(xn, wgu_ref[:, D_FF + c0:D_FF + c0 + cw])
        act = (gate * jax.nn.sigmoid(gate) * up).astype(BF16)
        part = _dot(act, wd_ref[c0:c0 + cw, :])
        acc = part if acc is None else acc + part
    o_ref[...] = x + 0.5 * _rms(acc, post_ref[...])


def _ffn(x, pre_g, post_g, w_gu, w_down):
    rows = x.shape[0]
    row_spec = pl.BlockSpec((ROW_TILE, D_MODEL), lambda i: (i, 0))
    return pl.pallas_call(
        _ffn_kernel,
        grid=(rows // ROW_TILE,),
        in_specs=[row_spec, _const_spec((1, D_MODEL)), _const_spec((1, D_MODEL)),
                  _const_spec((D_MODEL, 2 * D_FF)), _const_spec((D_FF, D_MODEL))],
        out_specs=row_spec,
        out_shape=jax.ShapeDtypeStruct((rows, D_MODEL), F32),
        compiler_params=pltpu.CompilerParams(
            dimension_semantics=("arbitrary",), vmem_limit_bytes=_mib(48)),
        name="ffn",
    )(x, pre_g, post_g, w_gu, w_down)


def _rope_tables(tile_pos0, row_pos):
    d = np.arange(V7X_LANES) % HEAD_DIM
    inv = np.where(d < ROT_DIM, ROPE_THETA ** (-(d % (ROT_DIM // 2)) / (ROT_DIM // 2)), 0.0)
    ang_a = np.asarray(tile_pos0, np.float64)[:, None] * inv[None, :]
    ang_b = np.asarray(row_pos, np.float64)[:, None] * inv[None, :]
    half = ROT_DIM // 2
    sel_lo = np.where(d < half, -1.0, 0.0)[None, :]
    sel_hi = np.where((d >= half) & (d < ROT_DIM), 1.0, 0.0)[None, :]
    f = lambda a: jnp.asarray(a, F32)
    return (f(np.cos(ang_a))[:, None, :], f(np.sin(ang_a))[:, None, :],
            f(np.cos(ang_b)), f(np.sin(ang_b)), f(sel_lo), f(sel_hi))


def _mix_in_kernel(h_ref, g_ref, win_ref, cos_a_ref, sin_a_ref, cos_b_ref, sin_b_ref,
                   sel_lo_ref, sel_hi_ref,
                   upool_ref, kf_ref, vf_ref, gates_ref, q_ref, kb_ref, v_ref, *, transposed):
    u = _rms(h_ref[...], g_ref[...]).astype(BF16)
    upool_ref[...] = _dot(u, win_ref[:, 0:D_POOL])

    ca, sa = cos_a_ref[0], sin_a_ref[0]
    cb, sb = cos_b_ref[...], sin_b_ref[...]
    cos = ca * cb - sa * sb
    sin = sa * cb + ca * sb
    rep = D_ATTN // V7X_LANES
    cos_f = jnp.concatenate([cos] * rep, axis=1)
    sin_lo = jnp.concatenate([sin * sel_lo_ref[...]] * rep, axis=1)
    sin_hi = jnp.concatenate([sin * sel_hi_ref[...]] * rep, axis=1)
    half = ROT_DIM // 2

    def rope(x):
        return (x * cos_f + pltpu.roll(x, D_ATTN - half, 1) * sin_lo
                + pltpu.roll(x, half, 1) * sin_hi)

    q = rope(_dot(u, win_ref[:, D_POOL:D_POOL + D_ATTN])) * QK_SCALE_LOG2
    k = rope(_dot(u, win_ref[:, D_POOL + D_ATTN:D_POOL + 2 * D_ATTN]))
    v = _dot(u, win_ref[:, D_POOL + 2 * D_ATTN:D_POOL + 3 * D_ATTN])
    kf_ref[...] = k
    vf_ref[...] = v
    kb_ref[...] = k.astype(BF16)
    if transposed:
        comp = lax.broadcasted_iota(jnp.int32, q.shape, 1) % (2 * HEAD_DIM)
        q_top = jnp.where(comp < HEAD_DIM, q, 0.0)
        q_bot = jnp.where(comp >= HEAD_DIM, q, 0.0)
        q_ref[0, :, 0:ROW_TILE] = q_top.T.astype(BF16)
        q_ref[0, :, ROW_TILE:2 * ROW_TILE] = q_bot.T.astype(BF16)
        v_ref[0] = v.T.astype(BF16)
    else:
        q_ref[...] = q
        v_ref[...] = v.astype(BF16)

    g0 = D_POOL + 3 * D_ATTN
    for c in range(2):
        z = _dot(u, win_ref[:, g0 + c * D_MODEL:g0 + (c + 1) * D_MODEL])
        gates_ref[:, c * D_MODEL:(c + 1) * D_MODEL] = jax.nn.sigmoid(z).astype(BF16)


def _mix_in(h, mix_pre_g, w_in, tile_pos0, row_pos, transposed):
    rows = h.shape[0]
    nt = rows // ROW_TILE
    tables = _rope_tables(tile_pos0, row_pos)
    row = lambda w: pl.BlockSpec((ROW_TILE, w), lambda i: (i, 0))
    if transposed:
        q_shape = jax.ShapeDtypeStruct((nt, D_ATTN, 2 * ROW_TILE), BF16)
        q_spec = pl.BlockSpec((1, D_ATTN, 2 * ROW_TILE), lambda i: (i, 0, 0))
        v_shape = jax.ShapeDtypeStruct((nt, D_ATTN, ROW_TILE), BF16)
        v_spec = pl.BlockSpec((1, D_ATTN, ROW_TILE), lambda i: (i, 0, 0))
    else:
        q_shape, q_spec = jax.ShapeDtypeStruct((rows, D_ATTN), F32), row(D_ATTN)
        v_shape, v_spec = jax.ShapeDtypeStruct((rows, D_ATTN), BF16), row(D_ATTN)
    return pl.pallas_call(
        functools.partial(_mix_in_kernel, transposed=transposed),
        grid=(nt,),
        in_specs=[row(D_MODEL), _const_spec((1, D_MODEL)), _const_spec(w_in.shape),
                  pl.BlockSpec((1, 1, V7X_LANES), lambda i: (i, 0, 0)),
                  pl.BlockSpec((1, 1, V7X_LANES), lambda i: (i, 0, 0)),
                  _const_spec((ROW_TILE, V7X_LANES)), _const_spec((ROW_TILE, V7X_LANES)),
                  _const_spec((1, V7X_LANES)), _const_spec((1, V7X_LANES))],
        out_specs=[row(D_POOL), row(D_ATTN), row(D_ATTN), row(2 * D_MODEL),
                   q_spec, row(D_ATTN), v_spec],
        out_shape=[jax.ShapeDtypeStruct((rows, D_POOL), F32),
                   jax.ShapeDtypeStruct((rows, D_ATTN), F32),
                   jax.ShapeDtypeStruct((rows, D_ATTN), F32),
                   jax.ShapeDtypeStruct((rows, 2 * D_MODEL), BF16),
                   q_shape,
                   jax.ShapeDtypeStruct((rows, D_ATTN), BF16),
                   v_shape],
        compiler_params=pltpu.CompilerParams(
            dimension_semantics=("arbitrary",), vmem_limit_bytes=_mib(48)),
        name="mix_in",
    )(h, mix_pre_g, w_in, *tables)


def _lambda(lam_ref):
    lq1, lk1, lq2, lk2 = (lam_ref[i:i + 1, :] for i in range(4))
    return (jnp.exp(jnp.sum(lq1 * lk1, axis=1, keepdims=True))
            - jnp.exp(jnp.sum(lq2 * lk2, axis=1, keepdims=True)) + LAM_INIT)


def _attn_prompt_kernel(lam_ref, g_ref, q_ref, k_ref, v_ref, o_ref, acc_ref, s_ref):
    tb = ROW_TILE
    hd2 = 2 * HEAD_DIM
    i = pl.program_id(1)
    acc_ref[...] = jnp.zeros_like(acc_ref)
    heads = [slice(h * hd2, (h + 1) * hd2) for h in range(ATTN_HEADS_PER_STEP)]
    strips = [slice(c, c + ATTN_STRIP) for c in range(0, 2 * tb, ATTN_STRIP)]

    def logits(j, rows, cols):
        kj = k_ref[pl.ds(pl.multiple_of(j * tb, tb), tb), rows]
        return _dot(kj, q_ref[0, rows, cols])

    def consume(j, m, l, masked, prefetch):
        m = [list(x) for x in m]
        l = [list(x) for x in l]
        for n, cols in enumerate(strips):
            for h, rows in enumerate(heads):
                if prefetch:
                    s_next = logits(j + 1, rows, cols)

                def s_strip():
                    s = s_ref[h, :, cols]
                    if masked:
                        k_chunk = lax.broadcasted_iota(jnp.int32, s.shape, 0) // CHUNK
                        q_pos = cols.start + lax.broadcasted_iota(jnp.int32, s.shape, 1)
                        s = jnp.where(k_chunk <= (q_pos % tb) // CHUNK, s, -jnp.inf)
                    return s
                m_new = jnp.maximum(m[h][n], jnp.max(s_strip(), axis=0, keepdims=True))
                alpha = jnp.exp2(m[h][n] - m_new)
                e = jnp.exp2(s_strip() - m_new)
                l[h][n] = alpha * l[h][n] + jnp.sum(e, axis=0, keepdims=True)
                m[h][n] = m_new
                acc_ref[rows, cols] = (acc_ref[rows, cols] * alpha
                                       + _dot(v_ref[j, rows, :], e.astype(BF16)))
                if prefetch:
                    s_ref[h, :, cols] = s_next
        return m, l

    m0 = [[jnp.full((1, ATTN_STRIP), -jnp.inf, F32) for _ in strips] for _ in heads]
    l0 = [[jnp.zeros((1, ATTN_STRIP), F32) for _ in strips] for _ in heads]
    for h, rows in enumerate(heads):
        for cols in strips:
            s_ref[h, :, cols] = logits(0, rows, cols)
    m, l = lax.fori_loop(0, i, lambda j, c: consume(j, c[0], c[1], False, True), (m0, l0))
    m, l = consume(i, m, l, True, False)

    lam = _lambda(lam_ref)
    for h, rows in enumerate(heads):
        o = acc_ref[rows, :] * (1.0 / jnp.concatenate(l[h], axis=1))
        o = o[:, 0:tb] - lam * o[:, tb:2 * tb]
        ms = jnp.mean(o * o, axis=0, keepdims=True)
        o_ref[rows, :] = o * lax.rsqrt(ms + EPS) * g_ref[...] * (1.0 - LAM_INIT)


def _attn_prompt(lam, subln_col, q_t, k_b, v_t):
    nt = q_t.shape[0]
    rows = nt * ROW_TILE
    hd2 = 2 * HEAD_DIM
    gw = ATTN_HEADS_PER_STEP * hd2
    resident = dict(pipeline_mode=pl.Buffered(1))
    return pl.pallas_call(
        _attn_prompt_kernel,
        grid=(H_DIFF // ATTN_HEADS_PER_STEP, nt),
        in_specs=[pl.BlockSpec((4, HEAD_DIM), lambda h, i: (0, 0)),
                  pl.BlockSpec((hd2, 1), lambda h, i: (0, 0)),
                  pl.BlockSpec((1, gw, 2 * ROW_TILE), lambda h, i: (i, h, 0)),
                  pl.BlockSpec((rows, gw), lambda h, i: (0, h), **resident),
                  pl.BlockSpec((nt, gw, ROW_TILE), lambda h, i: (0, h, 0), **resident)],
        out_specs=pl.BlockSpec((gw, ROW_TILE), lambda h, i: (h, i)),
        out_shape=jax.ShapeDtypeStruct((D_ATTN, rows), F32),
        scratch_shapes=[pltpu.VMEM((gw, 2 * ROW_TILE), F32),
                        pltpu.VMEM((ATTN_HEADS_PER_STEP, ROW_TILE, 2 * ROW_TILE), F32)],
        compiler_params=pltpu.CompilerParams(
            dimension_semantics=("arbitrary", "arbitrary"), vmem_limit_bytes=_mib(56)),
        name="attn_prompt",
    )(lam, subln_col, q_t, k_b, v_t)


def _attn_decode_kernel(lam_ref, g_ref, q_ref, ck_ref, cv_ref, kn_ref, vn_ref, o_ref):
    lam = _lambda(lam_ref)
    hd2 = 2 * HEAD_DIM
    past = ck_ref.shape[2]
    nt = (((1,), (1,)), ((), ()))
    for h in range(H_DIFF):
        sl = slice(h * hd2, (h + 1) * hd2)
        q = q_ref[0, :, sl]
        t = q.shape[0]
        comp = lax.broadcasted_iota(jnp.int32, q.shape, 1)
        lhs = jnp.concatenate([jnp.where(comp < HEAD_DIM, q, 0.0),
                               jnp.where(comp >= HEAD_DIM, q, 0.0)], axis=0).astype(BF16)
        s_c = _dot(lhs, ck_ref[0, sl, :].astype(BF16))
        s_n = lax.dot_general(lhs, kn_ref[0, :, sl].astype(BF16), nt, preferred_element_type=F32)
        m = jnp.maximum(jnp.max(s_c, axis=1, keepdims=True), jnp.max(s_n, axis=1, keepdims=True))
        e_c = jnp.exp2(s_c - m)
        e_n = jnp.exp2(s_n - m)
        inv_l = 1.0 / (jnp.sum(e_c, axis=1, keepdims=True) + jnp.sum(e_n, axis=1, keepdims=True))
        p_c = e_c * inv_l
        p_n = e_n * inv_l
        a_c = (p_c[0:t] - lam * p_c[t:2 * t]).astype(BF16)
        a_n = (p_n[0:t] - lam * p_n[t:2 * t]).astype(BF16)
        v_c = cv_ref[0, pl.ds(h, past, stride=H_DIFF), :].astype(BF16)
        o = _dot(a_c, v_c) + _dot(a_n, vn_ref[0, :, sl].astype(BF16))
        o_ref[0, :, sl] = _rms(o, g_ref[...]) * (1.0 - LAM_INIT)


def _attn_decode(lam, subln_row, q, cache_kt, cache_v4, k_new, v_new):
    b, _, past = cache_kt.shape
    t = q.shape[1]
    seq = lambda n: pl.BlockSpec((1, n, D_ATTN), lambda i: (i, 0, 0))
    return pl.pallas_call(
        _attn_decode_kernel,
        grid=(b,),
        in_specs=[pl.BlockSpec((4, HEAD_DIM), lambda i: (0, 0)),
                  pl.BlockSpec((1, 2 * HEAD_DIM), lambda i: (0, 0)),
                  seq(t),
                  pl.BlockSpec((1, D_ATTN, past), lambda i: (i, 0, 0)),
                  pl.BlockSpec((1, past * H_DIFF, 2 * HEAD_DIM), lambda i: (i, 0, 0)),
                  seq(t), seq(t)],
        out_specs=seq(t),
        out_shape=jax.ShapeDtypeStruct((b, t, D_ATTN), F32),
        compiler_params=pltpu.CompilerParams(
            dimension_semantics=("arbitrary",), vmem_limit_bytes=_mib(56)),
        name="attn_decode",
    )(lam, subln_row, q, cache_kt, cache_v4, k_new, v_new)


def _merge_kernel(h_ref, attn_ref, gates_ref, halo_ref, body_ref, poolw_ref, pscale_ref,
                  wbp_ref, wba_ref, wout_ref, postg_ref, o_ref, *, hist0, attn_transposed):
    nb, tl, _ = body_ref.shape
    hist = hist0 + pl.program_id(0) * tl
    hi = hist + 1 + lax.broadcasted_iota(jnp.int32, (tl, 1), 0)
    halo_row = lax.broadcasted_iota(jnp.int32, (POOL_HALO, 1), 0)
    halo_ok = halo_row >= POOL_HALO - jnp.minimum(hist, POOL_STATE)
    pooled = []
    for b in range(nb):
        halo = jnp.where(halo_ok, halo_ref[b], 0.0)
        ext = jnp.concatenate([halo, body_ref[b]], axis=0)
        groups = []
        for g, win in enumerate(POOL_WINDOWS):
            x = ext[:, g * POOL_GROUP:(g + 1) * POOL_GROUP]
            s, shift = x, 1
            while shift < win:
                s = s + pltpu.roll(s, shift, 0)
                shift *= 2
            inv_cnt = 1.0 / jnp.minimum(hi, win).astype(F32)
            mixed = s[POOL_HALO:] * inv_cnt - x[POOL_HALO:]
            groups.append(_dot(mixed.astype(BF16), poolw_ref[g]))
        pooled.append(jnp.concatenate(groups, axis=1) * pscale_ref[...])
    pool_out = pooled[0] if nb == 1 else jnp.concatenate(pooled, axis=0)

    attn = attn_ref[...].T if attn_transposed else attn_ref[...]
    branch_pool = _dot(pool_out.astype(BF16), wbp_ref[...])
    branch_attn = _dot(attn.astype(BF16), wba_ref[...])
    merged = (gates_ref[:, 0:D_MODEL].astype(F32) * branch_pool
              + gates_ref[:, D_MODEL:2 * D_MODEL].astype(F32) * branch_attn)
    o_ref[...] = h_ref[...] + _rms(_dot(merged.astype(BF16), wout_ref[...]), postg_ref[...])


def _merge(h, attn, gates, halo, body, halo_spec, body_spec, hist0, attn_transposed,
           pool_w, pool_scale, w_bp, w_ba, w_out, post_g):
    rows = h.shape[0]
    row = lambda w: pl.BlockSpec((ROW_TILE, w), lambda i: (i, 0))
    attn_spec = (pl.BlockSpec((D_ATTN, ROW_TILE), lambda i: (0, i)) if attn_transposed
                 else row(D_ATTN))
    return pl.pallas_call(
        functools.partial(_merge_kernel, hist0=hist0, attn_transposed=attn_transposed),
        grid=(rows // ROW_TILE,),
        in_specs=[row(D_MODEL), attn_spec, row(2 * D_MODEL), halo_spec, body_spec,
                  _const_spec(pool_w.shape), _const_spec(pool_scale.shape),
                  _const_spec(w_bp.shape), _const_spec(w_ba.shape), _const_spec(w_out.shape),
                  _const_spec(post_g.shape)],
        out_specs=row(D_MODEL),
        out_shape=jax.ShapeDtypeStruct((rows, D_MODEL), F32),
        compiler_params=pltpu.CompilerParams(
            dimension_semantics=("arbitrary",), vmem_limit_bytes=_mib(48)),
        name="merge",
    )(h, attn, gates, halo, body, pool_w, pool_scale, w_bp, w_ba, w_out, post_g)


def kernel(x_prompt, x_sample, cache_k, cache_v, state_pool, ffn1_pre_g, ffn1_post_g, ffn1_w_gu, ffn1_w_down, mix_pre_g, mix_post_g, w_in, pool_w, pool_scale, lambda_q1, lambda_k1, lambda_q2, lambda_k2, subln_g, w_branch_pool, w_branch_attn, w_out, ffn2_pre_g, ffn2_post_g, ffn2_w_gu, ffn2_w_down):
    bp, seq, _ = x_prompt.shape
    db, dseq, _ = x_sample.shape
    past = cache_k.shape[2]
    assert bp == 1 and seq % ROW_TILE == 0 and db * dseq == ROW_TILE and dseq >= POOL_STATE
    assert (past + dseq - 1) // CHUNK <= past // CHUNK and past % CHUNK == 0

    l = 0
    bf = lambda w: w[l].astype(BF16)
    g1pre, g1post, gmpre, gmpost = ffn1_pre_g[l:l + 1], ffn1_post_g[l:l + 1], mix_pre_g[l:l + 1], mix_post_g[l:l + 1]
    g2pre, g2post = ffn2_pre_g[l:l + 1], ffn2_post_g[l:l + 1]
    w1gu, w1d, w2gu, w2d = bf(ffn1_w_gu), bf(ffn1_w_down), bf(ffn2_w_gu), bf(ffn2_w_down)
    win, wpool, wbp, wba, wo = bf(w_in), bf(pool_w), bf(w_branch_pool), bf(w_branch_attn), bf(w_out)
    pscale = pool_scale[l:l + 1]
    lam = jnp.concatenate([lambda_q1[l:l + 1], lambda_k1[l:l + 1],
                           lambda_q2[l:l + 1], lambda_k2[l:l + 1]], axis=0)
    subln_row = subln_g[l:l + 1]
    subln_col = subln_g[l][:, None]
    merge_w = (wpool, pscale, wbp, wba, wo, gmpost)

    xp = x_prompt.reshape(seq, D_MODEL)
    nt = seq // ROW_TILE
    hp = _ffn(xp, g1pre, g1post, w1gu, w1d)
    upool, kf, vf, gates, q_t, k_b, v_t = _mix_in(
        hp, gmpre, win, np.arange(nt) * ROW_TILE, np.arange(ROW_TILE), transposed=True)
    attn_t = _attn_prompt(lam, subln_col, q_t, k_b, v_t)
    per_tile = ROW_TILE // POOL_HALO
    halo_spec = pl.BlockSpec((1, POOL_HALO, D_POOL),
                             lambda i: (jnp.maximum(i * per_tile - 1, 0), 0, 0))
    body_spec = pl.BlockSpec((1, ROW_TILE, D_POOL), lambda i: (i, 0, 0))
    hp = _merge(hp, attn_t, gates, upool.reshape(seq // POOL_HALO, POOL_HALO, D_POOL),
                upool.reshape(nt, ROW_TILE, D_POOL), halo_spec, body_spec, 0, True, *merge_w)
    y_prompt = _ffn(hp, g2pre, g2post, w2gu, w2d).reshape(bp, seq, D_MODEL)
    new_k_p = kf.reshape(1, bp, seq, H_DIFF, 2, HEAD_DIM)
    new_v_p = vf.reshape(1, bp, seq, H_DIFF, 2 * HEAD_DIM)
    new_pool_p = upool[seq - POOL_STATE:].reshape(1, bp, POOL_STATE, D_POOL)

    xs = x_sample.reshape(db * dseq, D_MODEL)
    hs = _ffn(xs, g1pre, g1post, w1gu, w1d)
    upool_s, kf_s, vf_s, gates_s, q_s, _, _ = _mix_in(
        hs, gmpre, win, np.array([past]), np.arange(ROW_TILE) % dseq, transposed=False)
    cache_kt = jnp.transpose(cache_k[l], (0, 2, 3, 4, 1)).reshape(db, D_ATTN, past)
    cache_v4 = cache_v[l].reshape(db, past * H_DIFF, 2 * HEAD_DIM)
    attn_s = _attn_decode(lam, subln_row, q_s.reshape(db, dseq, D_ATTN), cache_kt, cache_v4,
                          kf_s.reshape(db, dseq, D_ATTN), vf_s.reshape(db, dseq, D_ATTN))
    hist = state_pool[l]
    halo_s = jnp.concatenate(
        [jnp.zeros((db, POOL_HALO - POOL_STATE, D_POOL), F32), hist], axis=1)
    body_s = upool_s.reshape(db, dseq, D_POOL)
    hs = _merge(hs, attn_s.reshape(db * dseq, D_ATTN), gates_s, halo_s, body_s,
                pl.BlockSpec((db, POOL_HALO, D_POOL), lambda i: (0, 0, 0)),
                pl.BlockSpec((db, dseq, D_POOL), lambda i: (0, 0, 0)),
                POOL_STATE, False, *merge_w)
    y_sample = _ffn(hs, g2pre, g2post, w2gu, w2d).reshape(db, dseq, D_MODEL)
    new_k_s = kf_s.reshape(1, db, dseq, H_DIFF, 2, HEAD_DIM)
    new_v_s = vf_s.reshape(1, db, dseq, H_DIFF, 2 * HEAD_DIM)
    new_pool_s = body_s[:, dseq - POOL_STATE:].reshape(1, db, POOL_STATE, D_POOL)

    return (y_prompt, y_sample, new_k_p, new_v_p, new_pool_p, new_k_s, new_v_s, new_pool_s)
```

```python
import functools
import math

import numpy as np
import jax
import jax.numpy as jnp
from jax import lax
from jax.experimental import pallas as pl
from jax.experimental.pallas import tpu as pltpu

D_MODEL = 1024
D_FF = 2816
D_POOL = 512
POOL_WINDOWS = (2, 4, 8, 16)
POOL_GROUP = 128
POOL_STATE = 15
H_DIFF = 4
HEAD_DIM = 64
D_ATTN = 512
ROT_DIM = 16
ROPE_THETA = 500000.0
CHUNK = 64
EPS = 1e-6
LAM_INIT = 0.8 - 0.6 * math.exp(-0.3 * 0)

V7X_LANES = 128
BF16_SUBLANES = 16
POOL_HALO = 16
ROW_TILE = 512
FF_CHUNKS = ((0, 1024), (1024, 1024), (2048, 768))
ATTN_STRIP = 256
ATTN_HEADS_PER_STEP = 4
QK_SCALE_LOG2 = HEAD_DIM ** -0.5 * math.log2(math.e)

F32 = jnp.float32
BF16 = jnp.bfloat16


def _mib(n):
    return int(n * 1024 * 1024)


def _rms(x, g):
    return x * lax.rsqrt(jnp.mean(x * x, axis=-1, keepdims=True) + EPS) * g


def _dot(a, b):
    return jnp.dot(a, b, preferred_element_type=F32)


def _const_spec(shape):
    nd = len(shape)
    return pl.BlockSpec(shape, lambda *_: (0,) * nd, pipeline_mode=pl.Buffered(1))


def _ffn_kernel(x_ref, pre_ref, post_ref, wgu_ref, wd_ref, o_ref):
    x = x_ref[...]
    xn = _rms(x, pre_ref[...]).astype(BF16)
    acc = None
    for c0, cw in FF_CHUNKS:
        gate = _dot(xn, wgu_ref[:, c0:c0 + cw])
        up = _dot(xn, wgu_ref[:, D_FF + c0:D_FF + c0 + cw])
        act = (gate * jax.nn.sigmoid(gate) * up).astype(BF16)
        part = _dot(act, wd_ref[c0:c0 + cw, :])
        acc = part if acc is None else acc + part
    o_ref[...] = x + 0.5 * _rms(acc, post_ref[...])


def _ffn(x, pre_g, post_g, w_gu, w_down):
    rows = x.shape[0]
    row_spec = pl.BlockSpec((ROW_TILE, D_MODEL), lambda i: (i, 0))
    return pl.pallas_call(
        _ffn_kernel,
        grid=(rows // ROW_TILE,),
        in_specs=[row_spec, _const_spec((1, D_MODEL)), _const_spec((1, D_MODEL)),
                  _const_spec((D_MODEL, 2 * D_FF)), _const_spec((D_FF, D_MODEL))],
        out_specs=row_spec,
        out_shape=jax.ShapeDtypeStruct((rows, D_MODEL), F32),
        compiler_params=pltpu.CompilerParams(
            dimension_semantics=("arbitrary",), vmem_limit_bytes=_mib(48)),
        name="ffn",
    )(x, pre_g, post_g, w_gu, w_down)


def _rope_tables(tile_pos0, row_pos):
    d = np.arange(V7X_LANES) % HEAD_DIM
    inv = np.where(d < ROT_DIM, ROPE_THETA ** (-(d % (ROT_DIM // 2)) / (ROT_DIM // 2)), 0.0)
    ang_a = np.asarray(tile_pos0, np.float64)[:, None] * inv[None, :]
    ang_b = np.asarray(row_pos, np.float64)[:, None] * inv[None, :]
    half = ROT_DIM // 2
    sel_lo = np.where(d < half, -1.0, 0.0)[None, :]
    sel_hi = np.where((d >= half) & (d < ROT_DIM), 1.0, 0.0)[None, :]
    f = lambda a: jnp.asarray(a, F32)
    return (f(np.cos(ang_a))[:, None, :], f(np.sin(ang_a))[:, None, :],
            f(np.cos(ang_b)), f(np.sin(ang_b)), f(sel_lo), f(sel_hi))


def _mix_in_kernel(h_ref, g_ref, win_ref, cos_a_ref, sin_a_ref, cos_b_ref, sin_b_ref,
                   sel_lo_ref, sel_hi_ref,
                   upool_ref, kf_ref, vf_ref, gates_ref, q_ref, kb_ref, v_ref, *, transposed):
    u = _rms(h_ref[...], g_ref[...]).astype(BF16)
    upool_ref[...] = _dot(u, win_ref[:, 0:D_POOL])

    ca, sa = cos_a_ref[0], sin_a_ref[0]
    cb, sb = cos_b_ref[...], sin_b_ref[...]
    cos = ca * cb - sa * sb
    sin = sa * cb + ca * sb
    rep = D_ATTN // V7X_LANES
    cos_f = jnp.concatenate([cos] * rep, axis=1)
    sin_lo = jnp.concatenate([sin * sel_lo_ref[...]] * rep, axis=1)
    sin_hi = jnp.concatenate([sin * sel_hi_ref[...]] * rep, axis=1)
    half = ROT_DIM // 2

    def rope(x):
        return (x * cos_f + pltpu.roll(x, D_ATTN - half, 1) * sin_lo
                + pltpu.roll(x, half, 1) * sin_hi)

    q = rope(_dot(u, win_ref[:, D_POOL:D_POOL + D_ATTN])) * QK_SCALE_LOG2
    k = rope(_dot(u, win_ref[:, D_POOL + D_ATTN:D_POOL + 2 * D_ATTN]))
    v = _dot(u, win_ref[:, D_POOL + 2 * D_ATTN:D_POOL + 3 * D_ATTN])
    kf_ref[...] = k
    vf_ref[...] = v
    kb_ref[...] = k.astype(BF16)
    if transposed:
        comp = lax.broadcasted_iota(jnp.int32, q.shape, 1) % (2 * HEAD_DIM)
        q_top = jnp.where(comp < HEAD_DIM, q, 0.0)
        q_bot = jnp.where(comp >= HEAD_DIM, q, 0.0)
        q_ref[0, :, 0:ROW_TILE] = q_top.T.astype(BF16)
        q_ref[0, :, ROW_TILE:2 * ROW_TILE] = q_bot.T.astype(BF16)
        v_ref[0] = v.T.astype(BF16)
    else:
        q_ref[...] = q
        v_ref[...] = v.astype(BF16)

    g0 = D_POOL + 3 * D_ATTN
    for c in range(2):
        z = _dot(u, win_ref[:, g0 + c * D_MODEL:g0 + (c + 1) * D_MODEL])
        gates_ref[:, c * D_MODEL:(c + 1) * D_MODEL] = jax.nn.sigmoid(z).astype(BF16)


def _mix_in(h, mix_pre_g, w_in, tile_pos0, row_pos, transposed):
    rows = h.shape[0]
    nt = rows // ROW_TILE
    tables = _rope_tables(tile_pos0, row_pos)
    row = lambda w: pl.BlockSpec((ROW_TILE, w), lambda i: (i, 0))
    if transposed:
        q_shape = jax.ShapeDtypeStruct((nt, D_ATTN, 2 * ROW_TILE), BF16)
        q_spec = pl.BlockSpec((1, D_ATTN, 2 * ROW_TILE), lambda i: (i, 0, 0))
        v_shape = jax.ShapeDtypeStruct((nt, D_ATTN, ROW_TILE), BF16)
        v_spec = pl.BlockSpec((1, D_ATTN, ROW_TILE), lambda i: (i, 0, 0))
    else:
        q_shape, q_spec = jax.ShapeDtypeStruct((rows, D_ATTN), F32), row(D_ATTN)
        v_shape, v_spec = jax.ShapeDtypeStruct((rows, D_ATTN), BF16), row(D_ATTN)
    return pl.pallas_call(
        functools.partial(_mix_in_kernel, transposed=transposed),
        grid=(nt,),
        in_specs=[row(D_MODEL), _const_spec((1, D_MODEL)), _const_spec(w_in.shape),
                  pl.BlockSpec((1, 1, V7X_LANES), lambda i: (i, 0, 0)),
                  pl.BlockSpec((1, 1, V7X_LANES), lambda i: (i, 0, 0)),
                  _const_spec((ROW_TILE, V7X_LANES)), _const_spec((ROW_TILE, V7X_LANES)),
                  _const_spec((1, V7X_LANES)), _const_spec((1, V7X_LANES))],
        out_specs=[row(D_POOL), row(D_ATTN), row(D_ATTN), row(2 * D_MODEL),
                   q_spec, row(D_ATTN), v_spec],
        out_shape=[jax.ShapeDtypeStruct((rows, D_POOL), F32),
                   jax.ShapeDtypeStruct((rows, D_ATTN), F32),
                   jax.ShapeDtypeStruct((rows, D_ATTN), F32),
                   jax.ShapeDtypeStruct((rows, 2 * D_MODEL), BF16),
                   q_shape,
                   jax.ShapeDtypeStruct((rows, D_ATTN), BF16),
                   v_shape],
        compiler_params=pltpu.CompilerParams(
            dimension_semantics=("arbitrary",), vmem_limit_bytes=_mib(48)),
        name="mix_in",
    )(h, mix_pre_g, w_in, *tables)


def _lambda(lam_ref):
    lq1, lk1, lq2, lk2 = (lam_ref[i:i + 1, :] for i in range(4))
    return (jnp.exp(jnp.sum(lq1 * lk1, axis=1, keepdims=True))
            - jnp.exp(jnp.sum(lq2 * lk2, axis=1, keepdims=True)) + LAM_INIT)


def _attn_prompt_kernel(lam_ref, g_ref, q_ref, k_ref, v_ref, o_ref, acc_ref, s_ref):
    tb = ROW_TILE
    hd2 = 2 * HEAD_DIM
    i = pl.program_id(1)
    acc_ref[...] = jnp.zeros_like(acc_ref)
    heads = [slice(h * hd2, (h + 1) * hd2) for h in range(ATTN_HEADS_PER_STEP)]
    strips = [slice(c, c + ATTN_STRIP) for c in range(0, 2 * tb, ATTN_STRIP)]
    ones = jnp.ones((BF16_SUBLANES, tb), BF16)

    def logits(j, rows, cols):
        kj = k_ref[pl.ds(pl.multiple_of(j * tb, tb), tb), rows]
        return _dot(kj, q_ref[0, rows, cols])

    def consume(j, m, l, masked, prefetch):
        m = [list(x) for x in m]
        l = [list(x) for x in l]
        for n, cols in enumerate(strips):
            for h, rows in enumerate(heads):
                if prefetch:
                    s_next = logits(j + 1, rows, cols)

                def s_strip():
                    s = s_ref[h, n]
                    if masked:
                        k_chunk = lax.broadcasted_iota(jnp.int32, s.shape, 0) // CHUNK
                        q_pos = cols.start + lax.broadcasted_iota(jnp.int32, s.shape, 1)
                        s = jnp.where(k_chunk <= (q_pos % tb) // CHUNK, s, -jnp.inf)
                    return s
                m_new = jnp.maximum(m[h][n], jnp.max(s_strip(), axis=0, keepdims=True))
                alpha = jnp.exp2(m[h][n] - m_new)
                e = jnp.exp2((s_strip() - m_new).astype(BF16))
                pv = _dot(jnp.concatenate([v_ref[j, rows, :], ones], axis=0), e)
                l[h][n] = alpha * l[h][n] + pv[hd2:hd2 + 1, :]
                m[h][n] = m_new
                acc_ref[h, n] = acc_ref[h, n] * alpha + pv[0:hd2, :]
                if prefetch:
                    s_ref[h, n] = s_next
        return m, l

    m0 = [[jnp.full((1, ATTN_STRIP), -jnp.inf, F32) for _ in strips] for _ in heads]
    l0 = [[jnp.zeros((1, ATTN_STRIP), F32) for _ in strips] for _ in heads]
    for h, rows in enumerate(heads):
        for n, cols in enumerate(strips):
            s_ref[h, n] = logits(0, rows, cols)
    m, l = lax.fori_loop(0, i, lambda j, c: consume(j, c[0], c[1], False, True), (m0, l0))
    m, l = consume(i, m, l, True, False)

    lam = _lambda(lam_ref)
    for h, rows in enumerate(heads):
        o = jnp.concatenate([acc_ref[h, n] * (1.0 / l[h][n]) for n in range(len(strips))], axis=1)
        o = o[:, 0:tb] - lam * o[:, tb:2 * tb]
        ms = jnp.mean(o * o, axis=0, keepdims=True)
        o_ref[rows, :] = o * lax.rsqrt(ms + EPS) * g_ref[...] * (1.0 - LAM_INIT)


def _attn_prompt(lam, subln_col, q_t, k_b, v_t):
    nt = q_t.shape[0]
    rows = nt * ROW_TILE
    hd2 = 2 * HEAD_DIM
    gw = ATTN_HEADS_PER_STEP * hd2
    n_strips = 2 * ROW_TILE // ATTN_STRIP
    resident = dict(pipeline_mode=pl.Buffered(1))
    return pl.pallas_call(
        _attn_prompt_kernel,
        grid=(H_DIFF // ATTN_HEADS_PER_STEP, nt),
        in_specs=[pl.BlockSpec((4, HEAD_DIM), lambda h, i: (0, 0)),
                  pl.BlockSpec((hd2, 1), lambda h, i: (0, 0)),
                  pl.BlockSpec((1, gw, 2 * ROW_TILE), lambda h, i: (i, h, 0)),
                  pl.BlockSpec((rows, gw), lambda h, i: (0, h), **resident),
                  pl.BlockSpec((nt, gw, ROW_TILE), lambda h, i: (0, h, 0), **resident)],
        out_specs=pl.BlockSpec((gw, ROW_TILE), lambda h, i: (h, i)),
        out_shape=jax.ShapeDtypeStruct((D_ATTN, rows), F32),
        scratch_shapes=[pltpu.VMEM((ATTN_HEADS_PER_STEP, n_strips, hd2, ATTN_STRIP), F32),
                        pltpu.VMEM((ATTN_HEADS_PER_STEP, n_strips, ROW_TILE, ATTN_STRIP), F32)],
        compiler_params=pltpu.CompilerParams(
            dimension_semantics=("arbitrary", "arbitrary"), vmem_limit_bytes=_mib(56)),
        name="attn_prompt",
    )(lam, subln_col, q_t, k_b, v_t)


def _attn_decode_kernel(lam_ref, g_ref, q_ref, ck_ref, cv_ref, kn_ref, vn_ref, o_ref):
    lam = _lambda(lam_ref)
    hd2 = 2 * HEAD_DIM
    past = ck_ref.shape[2]
    nt = (((1,), (1,)), ((), ()))
    for h in range(H_DIFF):
        sl = slice(h * hd2, (h + 1) * hd2)
        q = q_ref[0, :, sl]
        t = q.shape[0]
        comp = lax.broadcasted_iota(jnp.int32, q.shape, 1)
        lhs = jnp.concatenate([jnp.where(comp < HEAD_DIM, q, 0.0),
                               jnp.where(comp >= HEAD_DIM, q, 0.0)], axis=0).astype(BF16)
        s_c = _dot(lhs, ck_ref[0, sl, :].astype(BF16))
        s_n = lax.dot_general(lhs, kn_ref[0, :, sl].astype(BF16), nt, preferred_element_type=F32)
        m = jnp.maximum(jnp.max(s_c, axis=1, keepdims=True), jnp.max(s_n, axis=1, keepdims=True))
        e_c = jnp.exp2(s_c - m)
        e_n = jnp.exp2(s_n - m)
        inv_l = 1.0 / (jnp.sum(e_c, axis=1, keepdims=True) + jnp.sum(e_n, axis=1, keepdims=True))
        p_c = e_c * inv_l
        p_n = e_n * inv_l
        a_c = (p_c[0:t] - lam * p_c[t:2 * t]).astype(BF16)
        a_n = (p_n[0:t] - lam * p_n[t:2 * t]).astype(BF16)
        v_c = cv_ref[0, pl.ds(h, past, stride=H_DIFF), :].astype(BF16)
        o = _dot(a_c, v_c) + _dot(a_n, vn_ref[0, :, sl].astype(BF16))
        o_ref[0, :, sl] = _rms(o, g_ref[...]) * (1.0 - LAM_INIT)


def _attn_decode(lam, subln_row, q, cache_kt, cache_v4, k_new, v_new):
    b, _, past = cache_kt.shape
    t = q.shape[1]
    seq = lambda n: pl.BlockSpec((1, n, D_ATTN), lambda i: (i, 0, 0))
    return pl.pallas_call(
        _attn_decode_kernel,
        grid=(b,),
        in_specs=[pl.BlockSpec((4, HEAD_DIM), lambda i: (0, 0)),
                  pl.BlockSpec((1, 2 * HEAD_DIM), lambda i: (0, 0)),
                  seq(t),
                  pl.BlockSpec((1, D_ATTN, past), lambda i: (i, 0, 0)),
                  pl.BlockSpec((1, past * H_DIFF, 2 * HEAD_DIM), lambda i: (i, 0, 0)),
                  seq(t), seq(t)],
        out_specs=seq(t),
        out_shape=jax.ShapeDtypeStruct((b, t, D_ATTN), F32),
        compiler_params=pltpu.CompilerParams(
            dimension_semantics=("arbitrary",), vmem_limit_bytes=_mib(56)),
        name="attn_decode",
    )(lam, subln_row, q, cache_kt, cache_v4, k_new, v_new)


def _merge_kernel(h_ref, attn_ref, gates_ref, halo_ref, body_ref, poolw_ref, pscale_ref,
                  wbp_ref, wba_ref, wout_ref, postg_ref, o_ref, *, hist0, attn_transposed):
    nb, tl, _ = body_ref.shape
    hist = hist0 + pl.program_id(0) * tl
    hi = hist + 1 + lax.broadcasted_iota(jnp.int32, (tl, 1), 0)
    halo_row = lax.broadcasted_iota(jnp.int32, (POOL_HALO, 1), 0)
    halo_ok = halo_row >= POOL_HALO - jnp.minimum(hist, POOL_STATE)
    pooled = []
    for b in range(nb):
        halo = jnp.where(halo_ok, halo_ref[b], 0.0)
        ext = jnp.concatenate([halo, body_ref[b]], axis=0)
        groups = []
        for g, win in enumerate(POOL_WINDOWS):
            x = ext[:, g * POOL_GROUP:(g + 1) * POOL_GROUP]
            s, shift = x, 1
            while shift < win:
                s = s + pltpu.roll(s, shift, 0)
                shift *= 2
            inv_cnt = 1.0 / jnp.minimum(hi, win).astype(F32)
            mixed = s[POOL_HALO:] * inv_cnt - x[POOL_HALO:]
            groups.append(_dot(mixed.astype(BF16), poolw_ref[g]))
        pooled.append(jnp.concatenate(groups, axis=1) * pscale_ref[...])
    pool_out = pooled[0] if nb == 1 else jnp.concatenate(pooled, axis=0)

    attn = attn_ref[...].T if attn_transposed else attn_ref[...]
    branch_pool = _dot(pool_out.astype(BF16), wbp_ref[...])
    branch_attn = _dot(attn.astype(BF16), wba_ref[...])
    merged = (gates_ref[:, 0:D_MODEL].astype(F32) * branch_pool
              + gates_ref[:, D_MODEL:2 * D_MODEL].astype(F32) * branch_attn)
    o_ref[...] = h_ref[...] + _rms(_dot(merged.astype(BF16), wout_ref[...]), postg_ref[...])


def _merge(h, attn, gates, halo, body, halo_spec, body_spec, hist0, attn_transposed,
           pool_w, pool_scale, w_bp, w_ba, w_out, post_g):
    rows = h.shape[0]
    row = lambda w: pl.BlockSpec((ROW_TILE, w), lambda i: (i, 0))
    attn_spec = (pl.BlockSpec((D_ATTN, ROW_TILE), lambda i: (0, i)) if attn_transposed
                 else row(D_ATTN))
    return pl.pallas_call(
        functools.partial(_merge_kernel, hist0=hist0, attn_transposed=attn_transposed),
        grid=(rows // ROW_TILE,),
        in_specs=[row(D_MODEL), attn_spec, row(2 * D_MODEL), halo_spec, body_spec,
                  _const_spec(pool_w.shape), _const_spec(pool_scale.shape),
                  _const_spec(w_bp.shape), _const_spec(w_ba.shape), _const_spec(w_out.shape),
                  _const_spec(post_g.shape)],
        out_specs=row(D_MODEL),
        out_shape=jax.ShapeDtypeStruct((rows, D_MODEL), F32),
        compiler_params=pltpu.CompilerParams(
            dimension_semantics=("arbitrary",), vmem_limit_bytes=_mib(48)),
        name="merge",
    )(h, attn, gates, halo, body, pool_w, pool_scale, w_bp, w_ba, w_out, post_g)


def kernel(x_prompt, x_sample, cache_k, cache_v, state_pool, ffn1_pre_g, ffn1_post_g, ffn1_w_gu, ffn1_w_down, mix_pre_g, mix_post_g, w_in, pool_w, pool_scale, lambda_q1, lambda_k1, lambda_q2, lambda_k2, subln_g, w_branch_pool, w_branch_attn, w_out, ffn2_pre_g, ffn2_post_g, ffn2_w_gu, ffn2_w_down):
    bp, seq, _ = x_prompt.shape
    db, dseq, _ = x_sample.shape
    past = cache_k.shape[2]
    assert bp == 1 and seq % ROW_TILE == 0 and db * dseq == ROW_TILE and dseq >= POOL_STATE
    assert (past + dseq - 1) // CHUNK <= past // CHUNK and past % CHUNK == 0

    l = 0
    bf = lambda w: w[l].astype(BF16)
    g1pre, g1post, gmpre, gmpost = ffn1_pre_g[l:l + 1], ffn1_post_g[l:l + 1], mix_pre_g[l:l + 1], mix_post_g[l:l + 1]
    g2pre, g2post = ffn2_pre_g[l:l + 1], ffn2_post_g[l:l + 1]
    w1gu, w1d, w2gu, w2d = bf(ffn1_w_gu), bf(ffn1_w_down), bf(ffn2_w_gu), bf(ffn2_w_down)
    win, wpool, wbp, wba, wo = bf(w_in), bf(pool_w), bf(w_branch_pool), bf(w_branch_attn), bf(w_out)
    pscale = pool_scale[l:l + 1]
    lam = jnp.concatenate([lambda_q1[l:l + 1], lambda_k1[l:l + 1],
                           lambda_q2[l:l + 1], lambda_k2[l:l + 1]], axis=0)
    subln_row = subln_g[l:l + 1]
    subln_col = subln_g[l][:, None]
    merge_w = (wpool, pscale, wbp, wba, wo, gmpost)

    xp = x_prompt.reshape(seq, D_MODEL)
    nt = seq // ROW_TILE
    hp = _ffn(xp, g1pre, g1post, w1gu, w1d)
    upool, kf, vf, gates, q_t, k_b, v_t = _mix_in(
        hp, gmpre, win, np.arange(nt) * ROW_TILE, np.arange(ROW_TILE), transposed=True)
    attn_t = _attn_prompt(lam, subln_col, q_t, k_b, v_t)
    per_tile = ROW_TILE // POOL_HALO
    halo_spec = pl.BlockSpec((1, POOL_HALO, D_POOL),
                             lambda i: (jnp.maximum(i * per_tile - 1, 0), 0, 0))
    body_spec = pl.BlockSpec((1, ROW_TILE, D_POOL), lambda i: (i, 0, 0))
    hp = _merge(hp, attn_t, gates, upool.reshape(seq // POOL_HALO, POOL_HALO, D_POOL),
                upool.reshape(nt, ROW_TILE, D_POOL), halo_spec, body_spec, 0, True, *merge_w)
    y_prompt = _ffn(hp, g2pre, g2post, w2gu, w2d).reshape(bp, seq, D_MODEL)
    new_k_p = kf.reshape(1, bp, seq, H_DIFF, 2, HEAD_DIM)
    new_v_p = vf.reshape(1, bp, seq, H_DIFF, 2 * HEAD_DIM)
    new_pool_p = upool[seq - POOL_STATE:].reshape(1, bp, POOL_STATE, D_POOL)

    xs = x_sample.reshape(db * dseq, D_MODEL)
    hs = _ffn(xs, g1pre, g1post, w1gu, w1d)
    upool_s, kf_s, vf_s, gates_s, q_s, _, _ = _mix_in(
        hs, gmpre, win, np.array([past]), np.arange(ROW_TILE) % dseq, transposed=False)
    cache_kt = jnp.transpose(cache_k[l], (0, 2, 3, 4, 1)).reshape(db, D_ATTN, past)
    cache_v4 = cache_v[l].reshape(db, past * H_DIFF, 2 * HEAD_DIM)
    attn_s = _attn_decode(lam, subln_row, q_s.reshape(db, dseq, D_ATTN), cache_kt, cache_v4,
                          kf_s.reshape(db, dseq, D_ATTN), vf_s.reshape(db, dseq, D_ATTN))
    hist = state_pool[l]
    halo_s = jnp.concatenate(
        [jnp.zeros((db, POOL_HALO - POOL_STATE, D_POOL), F32), hist], axis=1)
    body_s = upool_s.reshape(db, dseq, D_POOL)
    hs = _merge(hs, attn_s.reshape(db * dseq, D_ATTN), gates_s, halo_s, body_s,
                pl.BlockSpec((db, POOL_HALO, D_POOL), lambda i: (0, 0, 0)),
                pl.BlockSpec((db, dseq, D_POOL), lambda i: (0, 0, 0)),
                POOL_STATE, False, *merge_w)
    y_sample = _ffn(hs, g2pre, g2post, w2gu, w2d).reshape(db, dseq, D_MODEL)
    new_k_s = kf_s.reshape(1, db, dseq, H_DIFF, 2, HEAD_DIM)
    new_v_s = vf_s.reshape(1, db, dseq, H_DIFF, 2 * HEAD_DIM)
    new_pool_s = body_s[:, dseq - POOL_STATE:].reshape(1, db, POOL_STATE, D_POOL)

    return (y_prompt, y_sample, new_k_p, new_v_p, new_pool_p, new_k_s, new_v_s, new_pool_s)
```

```python
import functools
import math

import numpy as np
import jax
import jax.numpy as jnp
from jax import lax
from jax.experimental import pallas as pl
from jax.experimental.pallas import tpu as pltpu

D_MODEL = 1024
D_FF = 2816
D_POOL = 512
POOL_WINDOWS = (2, 4, 8, 16)
POOL_GROUP = 128
POOL_STATE = 15
H_DIFF = 4
HEAD_DIM = 64
D_ATTN = 512
ROT_DIM = 16
ROPE_THETA = 500000.0
CHUNK = 64
EPS = 1e-6
LAM_INIT = 0.8 - 0.6 * math.exp(-0.3 * 0)

V7X_LANES = 128
BF16_SUBLANES = 16
POOL_HALO = 16
ROW_TILE = 512
FF_CHUNKS = ((0, 1024), (1024, 1024), (2048, 768))
ATTN_STRIP = 256
ATTN_HEADS_PER_STEP = 4
QK_SCALE_LOG2 = HEAD_DIM ** -0.5 * math.log2(math.e)

F32 = jnp.float32
BF16 = jnp.bfloat16


def _mib(n):
    return int(n * 1024 * 1024)


def _rms(x, g):
    return x * lax.rsqrt(jnp.mean(x * x, axis=-1, keepdims=True) + EPS) * g


def _dot(a, b):
    return jnp.dot(a, b, preferred_element_type=F32)


def _const_spec(shape):
    nd = len(shape)
    return pl.BlockSpec(shape, lambda *_: (0,) * nd, pipeline_mode=pl.Buffered(1))


def _ffn_kernel(x_ref, pre_ref, post_ref, wgu_ref, wd_ref, o_ref):
    x = x_ref[...]
    xn = _rms(x, pre_ref[...]).astype(BF16)
    acc = None
    for c0, cw in FF_CHUNKS:
        gate = _dot(xn, wgu_ref[:, c0:c0 + cw])
        up = _dot(xn, wgu_ref[:, D_FF + c0:D_FF + c0 + cw])
        act = (gate * jax.nn.sigmoid(gate) * up).astype(BF16)
        part = _dot(act, wd_ref[c0:c0 + cw, :])
        acc = part if acc is None else acc + part
    o_ref[...] = x + 0.5 * _rms(acc, post_ref[...])


def _ffn(x, pre_g, post_g, w_gu, w_down):
    rows = x.shape[0]
    row_spec = pl.BlockSpec((ROW_TILE, D_MODEL), lambda i: (i, 0))
    return pl.pallas_call(
        _ffn_kernel,
        grid=(rows // ROW_TILE,),
        in_specs=[row_spec, _const_spec((1, D_MODEL)), _const_spec((1, D_MODEL)),
                  _const_spec((D_MODEL, 2 * D_FF)), _const_spec((D_FF, D_MODEL))],
        out_specs=row_spec,
        out_shape=jax.ShapeDtypeStruct((rows, D_MODEL), F32),
        compiler_params=pltpu.CompilerParams(
            dimension_semantics=("arbitrary",), vmem_limit_bytes=_mib(48)),
        name="ffn",
    )(x, pre_g, post_g, w_gu, w_down)


def _rope_tables(tile_pos0, row_pos):
    d = np.arange(V7X_LANES) % HEAD_DIM
    inv = np.where(d < ROT_DIM, ROPE_THETA ** (-(d % (ROT_DIM // 2)) / (ROT_DIM // 2)), 0.0)
    ang_a = np.asarray(tile_pos0, np.float64)[:, None] * inv[None, :]
    ang_b = np.asarray(row_pos, np.float64)[:, None] * inv[None, :]
    half = ROT_DIM // 2
    sel_lo = np.where(d < half, -1.0, 0.0)[None, :]
    sel_hi = np.where((d >= half) & (d < ROT_DIM), 1.0, 0.0)[None, :]
    f = lambda a: jnp.asarray(a, F32)
    return (f(np.cos(ang_a))[:, None, :], f(np.sin(ang_a))[:, None, :],
            f(np.cos(ang_b)), f(np.sin(ang_b)), f(sel_lo), f(sel_hi))


def _mix_in_kernel(h_ref, g_ref, win_ref, cos_a_ref, sin_a_ref, cos_b_ref, sin_b_ref,
                   sel_lo_ref, sel_hi_ref,
                   upool_ref, kf_ref, vf_ref, gates_ref, q_ref, kb_ref, v_ref, *, transposed):
    u = _rms(h_ref[...], g_ref[...]).astype(BF16)
    upool_ref[...] = _dot(u, win_ref[:, 0:D_POOL])

    ca, sa = cos_a_ref[0], sin_a_ref[0]
    cb, sb = cos_b_ref[...], sin_b_ref[...]
    cos = ca * cb - sa * sb
    sin = sa * cb + ca * sb
    rep = D_ATTN // V7X_LANES
    cos_f = jnp.concatenate([cos] * rep, axis=1)
    sin_lo = jnp.concatenate([sin * sel_lo_ref[...]] * rep, axis=1)
    sin_hi = jnp.concatenate([sin * sel_hi_ref[...]] * rep, axis=1)
    half = ROT_DIM // 2

    def rope(x):
        return (x * cos_f + pltpu.roll(x, D_ATTN - half, 1) * sin_lo
                + pltpu.roll(x, half, 1) * sin_hi)

    q = rope(_dot(u, win_ref[:, D_POOL:D_POOL + D_ATTN])) * QK_SCALE_LOG2
    k = rope(_dot(u, win_ref[:, D_POOL + D_ATTN:D_POOL + 2 * D_ATTN]))
    v = _dot(u, win_ref[:, D_POOL + 2 * D_ATTN:D_POOL + 3 * D_ATTN])
    kf_ref[...] = k
    vf_ref[...] = v.reshape(v.shape[0], H_DIFF, 2 * HEAD_DIM)
    kb_ref[...] = k.astype(BF16)
    if transposed:
        comp = lax.broadcasted_iota(jnp.int32, q.shape, 1) % (2 * HEAD_DIM)
        q_top = jnp.where(comp < HEAD_DIM, q, 0.0)
        q_bot = jnp.where(comp >= HEAD_DIM, q, 0.0)
        q_ref[0, :, 0:ROW_TILE] = q_top.T.astype(BF16)
        q_ref[0, :, ROW_TILE:2 * ROW_TILE] = q_bot.T.astype(BF16)
        v_ref[0] = v.T.astype(BF16)
    else:
        q_ref[...] = q
        v_ref[...] = v.astype(BF16)

    g0 = D_POOL + 3 * D_ATTN
    for c in range(2):
        z = _dot(u, win_ref[:, g0 + c * D_MODEL:g0 + (c + 1) * D_MODEL])
        gates_ref[:, c * D_MODEL:(c + 1) * D_MODEL] = jax.nn.sigmoid(z).astype(BF16)


def _mix_in(h, mix_pre_g, w_in, tile_pos0, row_pos, transposed):
    rows = h.shape[0]
    nt = rows // ROW_TILE
    tables = _rope_tables(tile_pos0, row_pos)
    row = lambda w: pl.BlockSpec((ROW_TILE, w), lambda i: (i, 0))
    if transposed:
        q_shape = jax.ShapeDtypeStruct((nt, D_ATTN, 2 * ROW_TILE), BF16)
        q_spec = pl.BlockSpec((1, D_ATTN, 2 * ROW_TILE), lambda i: (i, 0, 0))
        v_shape = jax.ShapeDtypeStruct((nt, D_ATTN, ROW_TILE), BF16)
        v_spec = pl.BlockSpec((1, D_ATTN, ROW_TILE), lambda i: (i, 0, 0))
    else:
        q_shape, q_spec = jax.ShapeDtypeStruct((rows, D_ATTN), F32), row(D_ATTN)
        v_shape, v_spec = jax.ShapeDtypeStruct((rows, D_ATTN), BF16), row(D_ATTN)
    return pl.pallas_call(
        functools.partial(_mix_in_kernel, transposed=transposed),
        grid=(nt,),
        in_specs=[row(D_MODEL), _const_spec((1, D_MODEL)), _const_spec(w_in.shape),
                  pl.BlockSpec((1, 1, V7X_LANES), lambda i: (i, 0, 0)),
                  pl.BlockSpec((1, 1, V7X_LANES), lambda i: (i, 0, 0)),
                  _const_spec((ROW_TILE, V7X_LANES)), _const_spec((ROW_TILE, V7X_LANES)),
                  _const_spec((1, V7X_LANES)), _const_spec((1, V7X_LANES))],
        out_specs=[row(D_POOL), row(D_ATTN),
                   pl.BlockSpec((ROW_TILE, H_DIFF, 2 * HEAD_DIM), lambda i: (i, 0, 0)),
                   row(2 * D_MODEL),
                   q_spec, row(D_ATTN), v_spec],
        out_shape=[jax.ShapeDtypeStruct((rows, D_POOL), F32),
                   jax.ShapeDtypeStruct((rows, D_ATTN), F32),
                   jax.ShapeDtypeStruct((rows, H_DIFF, 2 * HEAD_DIM), F32),
                   jax.ShapeDtypeStruct((rows, 2 * D_MODEL), BF16),
                   q_shape,
                   jax.ShapeDtypeStruct((rows, D_ATTN), BF16),
                   v_shape],
        compiler_params=pltpu.CompilerParams(
            dimension_semantics=("arbitrary",), vmem_limit_bytes=_mib(48)),
        name="mix_in",
    )(h, mix_pre_g, w_in, *tables)


def _lambda(lam_ref):
    lq1, lk1, lq2, lk2 = (lam_ref[i:i + 1, :] for i in range(4))
    return (jnp.exp(jnp.sum(lq1 * lk1, axis=1, keepdims=True))
            - jnp.exp(jnp.sum(lq2 * lk2, axis=1, keepdims=True)) + LAM_INIT)


def _attn_prompt_kernel(lam_ref, g_ref, q_ref, qn_ref, k_ref, v_ref, o_ref, acc_ref, s_ref):
    tb = ROW_TILE
    hd2 = 2 * HEAD_DIM
    i = pl.program_id(1)
    acc_ref[...] = jnp.zeros_like(acc_ref)
    heads = [slice(h * hd2, (h + 1) * hd2) for h in range(ATTN_HEADS_PER_STEP)]
    strips = [slice(c, c + ATTN_STRIP) for c in range(0, 2 * tb, ATTN_STRIP)]
    ones = jnp.ones((BF16_SUBLANES, tb), BF16)

    def logits(j, rows, cols, q_blk):
        kj = k_ref[pl.ds(pl.multiple_of(j * tb, tb), tb), rows]
        return _dot(kj, q_blk[0, rows, cols])

    def consume(j, m, l, diagonal, next_logits):
        m = [list(x) for x in m]
        l = [list(x) for x in l]
        for n, cols in enumerate(strips):
            q0 = cols.start % tb
            keys = q0 + ATTN_STRIP if diagonal else tb
            for h, rows in enumerate(heads):
                s_next = next_logits(rows, cols)

                def s_strip():
                    s = s_ref[h, n, 0:keys, :]
                    if diagonal:
                        k_chunk = lax.broadcasted_iota(jnp.int32, s.shape, 0) // CHUNK
                        q_chunk = (q0 + lax.broadcasted_iota(jnp.int32, s.shape, 1)) // CHUNK
                        s = jnp.where(k_chunk <= q_chunk, s, -jnp.inf)
                    return s
                m_new = jnp.maximum(m[h][n], jnp.max(s_strip(), axis=0, keepdims=True))
                alpha = jnp.exp2(m[h][n] - m_new)
                e = jnp.exp2((s_strip() - m_new).astype(BF16))
                pv = _dot(jnp.concatenate([v_ref[j, rows, 0:keys], ones[:, 0:keys]], axis=0), e)
                l[h][n] = alpha * l[h][n] + pv[hd2:hd2 + 1, :]
                m[h][n] = m_new
                acc_ref[h, n] = acc_ref[h, n] * alpha + pv[0:hd2, :]
                s_ref[h, n] = s_next
        return m, l

    @pl.when(i == 0)
    def _():
        for h, rows in enumerate(heads):
            for n, cols in enumerate(strips):
                s_ref[h, n] = logits(0, rows, cols, q_ref)

    m0 = [[jnp.full((1, ATTN_STRIP), -jnp.inf, F32) for _ in strips] for _ in heads]
    l0 = [[jnp.zeros((1, ATTN_STRIP), F32) for _ in strips] for _ in heads]
    m, l = lax.fori_loop(
        0, i,
        lambda j, c: consume(j, c[0], c[1], False, lambda r, c_: logits(j + 1, r, c_, q_ref)),
        (m0, l0))
    m, l = consume(i, m, l, True, lambda r, c_: logits(0, r, c_, qn_ref))

    lam = _lambda(lam_ref)
    for h, rows in enumerate(heads):
        o = jnp.concatenate([acc_ref[h, n] * (1.0 / l[h][n]) for n in range(len(strips))], axis=1)
        o = o[:, 0:tb] - lam * o[:, tb:2 * tb]
        ms = jnp.mean(o * o, axis=0, keepdims=True)
        o_ref[rows, :] = o * lax.rsqrt(ms + EPS) * g_ref[...] * (1.0 - LAM_INIT)


def _attn_prompt(lam, subln_col, q_t, k_b, v_t):
    nt = q_t.shape[0]
    rows = nt * ROW_TILE
    hd2 = 2 * HEAD_DIM
    gw = ATTN_HEADS_PER_STEP * hd2
    n_strips = 2 * ROW_TILE // ATTN_STRIP
    resident = dict(pipeline_mode=pl.Buffered(1))
    return pl.pallas_call(
        _attn_prompt_kernel,
        grid=(H_DIFF // ATTN_HEADS_PER_STEP, nt),
        in_specs=[pl.BlockSpec((4, HEAD_DIM), lambda h, i: (0, 0)),
                  pl.BlockSpec((hd2, 1), lambda h, i: (0, 0)),
                  pl.BlockSpec((1, gw, 2 * ROW_TILE), lambda h, i: (i, h, 0)),
                  pl.BlockSpec((1, gw, 2 * ROW_TILE), lambda h, i: (jnp.minimum(i + 1, nt - 1), h, 0)),
                  pl.BlockSpec((rows, gw), lambda h, i: (0, h), **resident),
                  pl.BlockSpec((nt, gw, ROW_TILE), lambda h, i: (0, h, 0), **resident)],
        out_specs=pl.BlockSpec((gw, ROW_TILE), lambda h, i: (h, i)),
        out_shape=jax.ShapeDtypeStruct((D_ATTN, rows), F32),
        scratch_shapes=[pltpu.VMEM((ATTN_HEADS_PER_STEP, n_strips, hd2, ATTN_STRIP), F32),
                        pltpu.VMEM((ATTN_HEADS_PER_STEP, n_strips, ROW_TILE, ATTN_STRIP), F32)],
        compiler_params=pltpu.CompilerParams(
            dimension_semantics=("arbitrary", "arbitrary"), vmem_limit_bytes=_mib(56)),
        name="attn_prompt",
    )(lam, subln_col, q_t, q_t, k_b, v_t)


def _attn_decode_kernel(lam_ref, g_ref, q_ref, ck_ref, cv_ref, kn_ref, vn_ref, o_ref):
    lam = _lambda(lam_ref)
    hd2 = 2 * HEAD_DIM
    past = ck_ref.shape[2]
    nt = (((1,), (1,)), ((), ()))
    for h in range(H_DIFF):
        sl = slice(h * hd2, (h + 1) * hd2)
        q = q_ref[0, :, sl]
        t = q.shape[0]
        comp = lax.broadcasted_iota(jnp.int32, q.shape, 1)
        lhs = jnp.concatenate([jnp.where(comp < HEAD_DIM, q, 0.0),
                               jnp.where(comp >= HEAD_DIM, q, 0.0)], axis=0).astype(BF16)
        s_c = _dot(lhs, ck_ref[0, sl, :].astype(BF16))
        s_n = lax.dot_general(lhs, kn_ref[0, :, sl].astype(BF16), nt, preferred_element_type=F32)
        m = jnp.maximum(jnp.max(s_c, axis=1, keepdims=True), jnp.max(s_n, axis=1, keepdims=True))
        e_c = jnp.exp2(s_c - m)
        e_n = jnp.exp2(s_n - m)
        inv_l = 1.0 / (jnp.sum(e_c, axis=1, keepdims=True) + jnp.sum(e_n, axis=1, keepdims=True))
        p_c = e_c * inv_l
        p_n = e_n * inv_l
        a_c = (p_c[0:t] - lam * p_c[t:2 * t]).astype(BF16)
        a_n = (p_n[0:t] - lam * p_n[t:2 * t]).astype(BF16)
        v_c = cv_ref[0, pl.ds(h, past, stride=H_DIFF), :].astype(BF16)
        o = _dot(a_c, v_c) + _dot(a_n, vn_ref[0, :, sl].astype(BF16))
        o_ref[0, :, sl] = _rms(o, g_ref[...]) * (1.0 - LAM_INIT)


def _attn_decode(lam, subln_row, q, cache_kt, cache_v4, k_new, v_new):
    b, _, past = cache_kt.shape
    t = q.shape[1]
    seq = lambda n: pl.BlockSpec((1, n, D_ATTN), lambda i: (i, 0, 0))
    return pl.pallas_call(
        _attn_decode_kernel,
        grid=(b,),
        in_specs=[pl.BlockSpec((4, HEAD_DIM), lambda i: (0, 0)),
                  pl.BlockSpec((1, 2 * HEAD_DIM), lambda i: (0, 0)),
                  seq(t),
                  pl.BlockSpec((1, D_ATTN, past), lambda i: (i, 0, 0)),
                  pl.BlockSpec((1, past * H_DIFF, 2 * HEAD_DIM), lambda i: (i, 0, 0)),
                  seq(t), seq(t)],
        out_specs=seq(t),
        out_shape=jax.ShapeDtypeStruct((b, t, D_ATTN), F32),
        compiler_params=pltpu.CompilerParams(
            dimension_semantics=("arbitrary",), vmem_limit_bytes=_mib(56)),
        name="attn_decode",
    )(lam, subln_row, q, cache_kt, cache_v4, k_new, v_new)


def _merge_kernel(h_ref, attn_ref, gates_ref, halo_ref, body_ref, poolw_ref, pscale_ref,
                  wbp_ref, wba_ref, wout_ref, postg_ref, o_ref, *, hist0, attn_transposed):
    nb, tl, _ = body_ref.shape
    hist = hist0 + pl.program_id(0) * tl
    hi = hist + 1 + lax.broadcasted_iota(jnp.int32, (tl, 1), 0)
    halo_row = lax.broadcasted_iota(jnp.int32, (POOL_HALO, 1), 0)
    halo_ok = halo_row >= POOL_HALO - jnp.minimum(hist, POOL_STATE)
    pooled = []
    for b in range(nb):
        halo = jnp.where(halo_ok, halo_ref[b], 0.0)
        ext = jnp.concatenate([halo, body_ref[b]], axis=0)
        groups = []
        for g, win in enumerate(POOL_WINDOWS):
            x = ext[:, g * POOL_GROUP:(g + 1) * POOL_GROUP]
            s, shift = x, 1
            while shift < win:
                s = s + pltpu.roll(s, shift, 0)
                shift *= 2
            inv_cnt = 1.0 / jnp.minimum(hi, win).astype(F32)
            mixed = s[POOL_HALO:] * inv_cnt - x[POOL_HALO:]
            groups.append(_dot(mixed.astype(BF16), poolw_ref[g]))
        pooled.append(jnp.concatenate(groups, axis=1) * pscale_ref[...])
    pool_out = pooled[0] if nb == 1 else jnp.concatenate(pooled, axis=0)

    attn = attn_ref[...].T if attn_transposed else attn_ref[...]
    branch_pool = _dot(pool_out.astype(BF16), wbp_ref[...])
    branch_attn = _dot(attn.astype(BF16), wba_ref[...])
    merged = (gates_ref[:, 0:D_MODEL].astype(F32) * branch_pool
              + gates_ref[:, D_MODEL:2 * D_MODEL].astype(F32) * branch_attn)
    o_ref[...] = h_ref[...] + _rms(_dot(merged.astype(BF16), wout_ref[...]), postg_ref[...])


def _merge(h, attn, gates, halo, body, halo_spec, body_spec, hist0, attn_transposed,
           pool_w, pool_scale, w_bp, w_ba, w_out, post_g):
    rows = h.shape[0]
    row = lambda w: pl.BlockSpec((ROW_TILE, w), lambda i: (i, 0))
    attn_spec = (pl.BlockSpec((D_ATTN, ROW_TILE), lambda i: (0, i)) if attn_transposed
                 else row(D_ATTN))
    return pl.pallas_call(
        functools.partial(_merge_kernel, hist0=hist0, attn_transposed=attn_transposed),
        grid=(rows // ROW_TILE,),
        in_specs=[row(D_MODEL), attn_spec, row(2 * D_MODEL), halo_spec, body_spec,
                  _const_spec(pool_w.shape), _const_spec(pool_scale.shape),
                  _const_spec(w_bp.shape), _const_spec(w_ba.shape), _const_spec(w_out.shape),
                  _const_spec(post_g.shape)],
        out_specs=row(D_MODEL),
        out_shape=jax.ShapeDtypeStruct((rows, D_MODEL), F32),
        compiler_params=pltpu.CompilerParams(
            dimension_semantics=("arbitrary",), vmem_limit_bytes=_mib(48)),
        name="merge",
    )(h, attn, gates, halo, body, pool_w, pool_scale, w_bp, w_ba, w_out, post_g)


def kernel(x_prompt, x_sample, cache_k, cache_v, state_pool, ffn1_pre_g, ffn1_post_g, ffn1_w_gu, ffn1_w_down, mix_pre_g, mix_post_g, w_in, pool_w, pool_scale, lambda_q1, lambda_k1, lambda_q2, lambda_k2, subln_g, w_branch_pool, w_branch_attn, w_out, ffn2_pre_g, ffn2_post_g, ffn2_w_gu, ffn2_w_down):
    bp, seq, _ = x_prompt.shape
    db, dseq, _ = x_sample.shape
    past = cache_k.shape[2]
    assert bp == 1 and seq % ROW_TILE == 0 and db * dseq == ROW_TILE and dseq >= POOL_STATE
    assert (past + dseq - 1) // CHUNK <= past // CHUNK and past % CHUNK == 0

    l = 0
    bf = lambda w: w[l].astype(BF16)
    g1pre, g1post, gmpre, gmpost = ffn1_pre_g[l:l + 1], ffn1_post_g[l:l + 1], mix_pre_g[l:l + 1], mix_post_g[l:l + 1]
    g2pre, g2post = ffn2_pre_g[l:l + 1], ffn2_post_g[l:l + 1]
    w1gu, w1d, w2gu, w2d = bf(ffn1_w_gu), bf(ffn1_w_down), bf(ffn2_w_gu), bf(ffn2_w_down)
    win, wpool, wbp, wba, wo = bf(w_in), bf(pool_w), bf(w_branch_pool), bf(w_branch_attn), bf(w_out)
    pscale = pool_scale[l:l + 1]
    lam = jnp.concatenate([lambda_q1[l:l + 1], lambda_k1[l:l + 1],
                           lambda_q2[l:l + 1], lambda_k2[l:l + 1]], axis=0)
    subln_row = subln_g[l:l + 1]
    subln_col = subln_g[l][:, None]
    merge_w = (wpool, pscale, wbp, wba, wo, gmpost)

    xp = x_prompt.reshape(seq, D_MODEL)
    nt = seq // ROW_TILE
    hp = _ffn(xp, g1pre, g1post, w1gu, w1d)
    upool, kf, vf, gates, q_t, k_b, v_t = _mix_in(
        hp, gmpre, win, np.arange(nt) * ROW_TILE, np.arange(ROW_TILE), transposed=True)
    attn_t = _attn_prompt(lam, subln_col, q_t, k_b, v_t)
    per_tile = ROW_TILE // POOL_HALO
    halo_spec = pl.BlockSpec((1, POOL_HALO, D_POOL),
                             lambda i: (jnp.maximum(i * per_tile - 1, 0), 0, 0))
    body_spec = pl.BlockSpec((1, ROW_TILE, D_POOL), lambda i: (i, 0, 0))
    hp = _merge(hp, attn_t, gates, upool.reshape(seq // POOL_HALO, POOL_HALO, D_POOL),
                upool.reshape(nt, ROW_TILE, D_POOL), halo_spec, body_spec, 0, True, *merge_w)
    y_prompt = _ffn(hp, g2pre, g2post, w2gu, w2d).reshape(bp, seq, D_MODEL)
    new_k_p = kf.reshape(1, bp, seq, H_DIFF, 2, HEAD_DIM)
    new_v_p = vf.reshape(1, bp, seq, H_DIFF, 2 * HEAD_DIM)
    new_pool_p = upool[seq - POOL_STATE:].reshape(1, bp, POOL_STATE, D_POOL)

    xs = x_sample.reshape(db * dseq, D_MODEL)
    hs = _ffn(xs, g1pre, g1post, w1gu, w1d)
    upool_s, kf_s, vf_s, gates_s, q_s, _, _ = _mix_in(
        hs, gmpre, win, np.array([past]), np.arange(ROW_TILE) % dseq, transposed=False)
    cache_kt = jnp.transpose(cache_k[l], (0, 2, 3, 4, 1)).reshape(db, D_ATTN, past)
    cache_v4 = cache_v[l].reshape(db, past * H_DIFF, 2 * HEAD_DIM)
    attn_s = _attn_decode(lam, subln_row, q_s.reshape(db, dseq, D_ATTN), cache_kt, cache_v4,
                          kf_s.reshape(db, dseq, D_ATTN), vf_s.reshape(db, dseq, D_ATTN))
    hist = state_pool[l]
    halo_s = jnp.concatenate(
        [jnp.zeros((db, POOL_HALO - POOL_STATE, D_POOL), F32), hist], axis=1)
    body_s = upool_s.reshape(db, dseq, D_POOL)
    hs = _merge(hs, attn_s.reshape(db * dseq, D_ATTN), gates_s, halo_s, body_s,
                pl.BlockSpec((db, POOL_HALO, D_POOL), lambda i: (0, 0, 0)),
                pl.BlockSpec((db, dseq, D_POOL), lambda i: (0, 0, 0)),
                POOL_STATE, False, *merge_w)
    y_sample = _ffn(hs, g2pre, g2post, w2gu, w2d).reshape(db, dseq, D_MODEL)
    new_k_s = kf_s.reshape(1, db, dseq, H_DIFF, 2, HEAD_DIM)
    new_v_s = vf_s.reshape(1, db, dseq, H_DIFF, 2 * HEAD_DIM)
    new_pool_s = body_s[:, dseq - POOL_STATE:].reshape(1, db, POOL_STATE, D_POOL)

    return (y_prompt, y_sample, new_k_p, new_v_p, new_pool_p, new_k_s, new_v_s, new_pool_s)
```

```python
import functools
import math

import numpy as np
import jax
import jax.numpy as jnp
from jax import lax
from jax.experimental import pallas as pl
from jax.experimental.pallas import tpu as pltpu

D_MODEL = 1024
D_FF = 2816
D_POOL = 512
POOL_WINDOWS = (2, 4, 8, 16)
POOL_GROUP = 128
POOL_STATE = 15
H_DIFF = 4
HEAD_DIM = 64
D_ATTN = 512
ROT_DIM = 16
ROPE_THETA = 500000.0
CHUNK = 64
EPS = 1e-6
LAM_INIT = 0.8 - 0.6 * math.exp(-0.3 * 0)

V7X_LANES = 128
BF16_SUBLANES = 16
POOL_HALO = 16
ROW_TILE = 512
FF_CHUNKS = ((0, 1024), (1024, 1024), (2048, 768))
ATTN_STRIP = 256
ATTN_HEADS_PER_STEP = 4
QK_SCALE_LOG2 = HEAD_DIM ** -0.5 * math.log2(math.e)

F32 = jnp.float32
BF16 = jnp.bfloat16


def _mib(n):
    return int(n * 1024 * 1024)


def _rms(x, g):
    return x * lax.rsqrt(jnp.mean(x * x, axis=-1, keepdims=True) + EPS) * g


def _dot(a, b):
    return jnp.dot(a, b, preferred_element_type=F32)


def _const_spec(shape):
    nd = len(shape)
    return pl.BlockSpec(shape, lambda *_: (0,) * nd, pipeline_mode=pl.Buffered(1))


def _ffn_kernel(x_ref, pre_ref, post_ref, wgu_ref, wd_ref, o_ref):
    x = x_ref[...]
    xn = _rms(x, pre_ref[...]).astype(BF16)
    acc = None
    for c0, cw in FF_CHUNKS:
        gate = _dot(xn, wgu_ref[:, c0:c0 + cw])
        up = _dot(xn, wgu_ref[:, D_FF + c0:D_FF + c0 + cw])
        act = (gate * jax.nn.sigmoid(gate) * up).astype(BF16)
        part = _dot(act, wd_ref[c0:c0 + cw, :])
        acc = part if acc is None else acc + part
    o_ref[...] = x + 0.5 * _rms(acc, post_ref[...])


def _ffn(x, pre_g, post_g, w_gu, w_down):
    rows = x.shape[0]
    row_spec = pl.BlockSpec((ROW_TILE, D_MODEL), lambda i: (i, 0))
    return pl.pallas_call(
        _ffn_kernel,
        grid=(rows // ROW_TILE,),
        in_specs=[row_spec, _const_spec((1, D_MODEL)), _const_spec((1, D_MODEL)),
                  _const_spec((D_MODEL, 2 * D_FF)), _const_spec((D_FF, D_MODEL))],
        out_specs=row_spec,
        out_shape=jax.ShapeDtypeStruct((rows, D_MODEL), F32),
        compiler_params=pltpu.CompilerParams(
            dimension_semantics=("arbitrary",), vmem_limit_bytes=_mib(48)),
        name="ffn",
    )(x, pre_g, post_g, w_gu, w_down)


def _rope_tables(tile_pos0, row_pos):
    d = np.arange(V7X_LANES) % HEAD_DIM
    inv = np.where(d < ROT_DIM, ROPE_THETA ** (-(d % (ROT_DIM // 2)) / (ROT_DIM // 2)), 0.0)
    ang_a = np.asarray(tile_pos0, np.float64)[:, None] * inv[None, :]
    ang_b = np.asarray(row_pos, np.float64)[:, None] * inv[None, :]
    half = ROT_DIM // 2
    sel_lo = np.where(d < half, -1.0, 0.0)[None, :]
    sel_hi = np.where((d >= half) & (d < ROT_DIM), 1.0, 0.0)[None, :]
    f = lambda a: jnp.asarray(a, F32)
    return (f(np.cos(ang_a))[:, None, :], f(np.sin(ang_a))[:, None, :],
            f(np.cos(ang_b)), f(np.sin(ang_b)), f(sel_lo), f(sel_hi))


def _mix_in_kernel(h_ref, g_ref, win_ref, cos_a_ref, sin_a_ref, cos_b_ref, sin_b_ref,
                   sel_lo_ref, sel_hi_ref,
                   upool_ref, kf_ref, vf_ref, gates_ref, q_ref, kb_ref, v_ref, *, transposed):
    u = _rms(h_ref[...], g_ref[...]).astype(BF16)
    upool_ref[...] = _dot(u, win_ref[:, 0:D_POOL])

    ca, sa = cos_a_ref[0], sin_a_ref[0]
    cb, sb = cos_b_ref[...], sin_b_ref[...]
    cos = ca * cb - sa * sb
    sin = sa * cb + ca * sb
    rep = D_ATTN // V7X_LANES
    cos_f = jnp.concatenate([cos] * rep, axis=1)
    sin_lo = jnp.concatenate([sin * sel_lo_ref[...]] * rep, axis=1)
    sin_hi = jnp.concatenate([sin * sel_hi_ref[...]] * rep, axis=1)
    half = ROT_DIM // 2

    def rope(x):
        return (x * cos_f + pltpu.roll(x, D_ATTN - half, 1) * sin_lo
                + pltpu.roll(x, half, 1) * sin_hi)

    q = rope(_dot(u, win_ref[:, D_POOL:D_POOL + D_ATTN])) * QK_SCALE_LOG2
    k = rope(_dot(u, win_ref[:, D_POOL + D_ATTN:D_POOL + 2 * D_ATTN]))
    v = _dot(u, win_ref[:, D_POOL + 2 * D_ATTN:D_POOL + 3 * D_ATTN])
    kf_ref[...] = k
    vf_ref[...] = v.reshape(v.shape[0], H_DIFF, 2 * HEAD_DIM)
    kb_ref[...] = k.astype(BF16)
    if transposed:
        comp = lax.broadcasted_iota(jnp.int32, q.shape, 1) % (2 * HEAD_DIM)
        q_top = jnp.where(comp < HEAD_DIM, q, 0.0)
        q_bot = jnp.where(comp >= HEAD_DIM, q, 0.0)
        q_ref[0, :, 0:ROW_TILE] = q_top.T.astype(BF16)
        q_ref[0, :, ROW_TILE:2 * ROW_TILE] = q_bot.T.astype(BF16)
        v_ref[0] = v.T.astype(BF16)
    else:
        q_ref[...] = q
        v_ref[...] = v.astype(BF16)

    g0 = D_POOL + 3 * D_ATTN
    for c in range(2):
        z = _dot(u, win_ref[:, g0 + c * D_MODEL:g0 + (c + 1) * D_MODEL])
        gates_ref[:, c * D_MODEL:(c + 1) * D_MODEL] = jax.nn.sigmoid(z).astype(BF16)


def _mix_in(h, mix_pre_g, w_in, tile_pos0, row_pos, transposed):
    rows = h.shape[0]
    nt = rows // ROW_TILE
    tables = _rope_tables(tile_pos0, row_pos)
    row = lambda w: pl.BlockSpec((ROW_TILE, w), lambda i: (i, 0))
    if transposed:
        q_shape = jax.ShapeDtypeStruct((nt, D_ATTN, 2 * ROW_TILE), BF16)
        q_spec = pl.BlockSpec((1, D_ATTN, 2 * ROW_TILE), lambda i: (i, 0, 0))
        v_shape = jax.ShapeDtypeStruct((nt, D_ATTN, ROW_TILE), BF16)
        v_spec = pl.BlockSpec((1, D_ATTN, ROW_TILE), lambda i: (i, 0, 0))
    else:
        q_shape, q_spec = jax.ShapeDtypeStruct((rows, D_ATTN), F32), row(D_ATTN)
        v_shape, v_spec = jax.ShapeDtypeStruct((rows, D_ATTN), BF16), row(D_ATTN)
    return pl.pallas_call(
        functools.partial(_mix_in_kernel, transposed=transposed),
        grid=(nt,),
        in_specs=[row(D_MODEL), _const_spec((1, D_MODEL)), _const_spec(w_in.shape),
                  pl.BlockSpec((1, 1, V7X_LANES), lambda i: (i, 0, 0)),
                  pl.BlockSpec((1, 1, V7X_LANES), lambda i: (i, 0, 0)),
                  _const_spec((ROW_TILE, V7X_LANES)), _const_spec((ROW_TILE, V7X_LANES)),
                  _const_spec((1, V7X_LANES)), _const_spec((1, V7X_LANES))],
        out_specs=[row(D_POOL), row(D_ATTN),
                   pl.BlockSpec((ROW_TILE, H_DIFF, 2 * HEAD_DIM), lambda i: (i, 0, 0)),
                   row(2 * D_MODEL),
                   q_spec, row(D_ATTN), v_spec],
        out_shape=[jax.ShapeDtypeStruct((rows, D_POOL), F32),
                   jax.ShapeDtypeStruct((rows, D_ATTN), F32),
                   jax.ShapeDtypeStruct((rows, H_DIFF, 2 * HEAD_DIM), F32),
                   jax.ShapeDtypeStruct((rows, 2 * D_MODEL), BF16),
                   q_shape,
                   jax.ShapeDtypeStruct((rows, D_ATTN), BF16),
                   v_shape],
        compiler_params=pltpu.CompilerParams(
            dimension_semantics=("arbitrary",), vmem_limit_bytes=_mib(48)),
        name="mix_in",
    )(h, mix_pre_g, w_in, *tables)


def _lambda(lam_ref):
    lq1, lk1, lq2, lk2 = (lam_ref[i:i + 1, :] for i in range(4))
    return (jnp.exp(jnp.sum(lq1 * lk1, axis=1, keepdims=True))
            - jnp.exp(jnp.sum(lq2 * lk2, axis=1, keepdims=True)) + LAM_INIT)


def _attn_prompt_kernel(lam_ref, g_ref, q_ref, qn_ref, k_ref, v_ref, o_ref, acc_ref, s_ref):
    tb = ROW_TILE
    hd2 = 2 * HEAD_DIM
    i = pl.program_id(1)
    acc_ref[...] = jnp.zeros_like(acc_ref)
    heads = [slice(h * hd2, (h + 1) * hd2) for h in range(ATTN_HEADS_PER_STEP)]
    strips = [slice(c, c + ATTN_STRIP) for c in range(0, 2 * tb, ATTN_STRIP)]
    ones = jnp.ones((BF16_SUBLANES, tb), BF16)

    def logits(j, rows, cols, q_blk):
        kj = k_ref[pl.ds(pl.multiple_of(j * tb, tb), tb), rows]
        return _dot(kj, q_blk[0, rows, cols])

    def consume(j, m, l, diagonal, next_logits):
        m = [list(x) for x in m]
        l = [list(x) for x in l]
        for n, cols in enumerate(strips):
            q0 = cols.start % tb
            keys = q0 + ATTN_STRIP if diagonal else tb
            for h, rows in enumerate(heads):
                s_next = next_logits(rows, cols)

                def s_strip():
                    s = s_ref[h, n, 0:keys, :]
                    if diagonal:
                        k_chunk = lax.broadcasted_iota(jnp.int32, s.shape, 0) // CHUNK
                        q_chunk = (q0 + lax.broadcasted_iota(jnp.int32, s.shape, 1)) // CHUNK
                        s = jnp.where(k_chunk <= q_chunk, s, -jnp.inf)
                    return s
                m_new = jnp.maximum(m[h][n], jnp.max(s_strip(), axis=0, keepdims=True))
                alpha = jnp.exp2(m[h][n] - m_new)
                e = jnp.exp2((s_strip() - m_new).astype(BF16))
                pv = _dot(jnp.concatenate([v_ref[j, rows, 0:keys], ones[:, 0:keys]], axis=0), e)
                l[h][n] = alpha * l[h][n] + pv[hd2:hd2 + 1, :]
                m[h][n] = m_new
                acc_ref[h, n] = acc_ref[h, n] * alpha + pv[0:hd2, :]
                s_ref[h, n] = s_next
        return m, l

    @pl.when(i == 0)
    def _():
        for h, rows in enumerate(heads):
            for n, cols in enumerate(strips):
                s_ref[h, n] = logits(0, rows, cols, q_ref)

    m0 = [[jnp.full((1, ATTN_STRIP), -jnp.inf, F32) for _ in strips] for _ in heads]
    l0 = [[jnp.zeros((1, ATTN_STRIP), F32) for _ in strips] for _ in heads]
    def full_block(j, ml):
        return consume(j, ml[0], ml[1], False, lambda r, c: logits(j + 1, r, c, q_ref))

    m, l = lax.fori_loop(
        0, i // 2, lambda t, ml: full_block(2 * t + 1, full_block(2 * t, ml)), (m0, l0))
    m, l = lax.cond(i % 2 == 1, lambda ml: full_block(i - 1, ml), lambda ml: ml, (m, l))
    m, l = consume(i, m, l, True, lambda r, c: logits(0, r, c, qn_ref))

    lam = _lambda(lam_ref)
    for h, rows in enumerate(heads):
        o = jnp.concatenate([acc_ref[h, n] * (1.0 / l[h][n]) for n in range(len(strips))], axis=1)
        o = o[:, 0:tb] - lam * o[:, tb:2 * tb]
        ms = jnp.mean(o * o, axis=0, keepdims=True)
        o_ref[rows, :] = o * lax.rsqrt(ms + EPS) * g_ref[...] * (1.0 - LAM_INIT)


def _attn_prompt(lam, subln_col, q_t, k_b, v_t):
    nt = q_t.shape[0]
    rows = nt * ROW_TILE
    hd2 = 2 * HEAD_DIM
    gw = ATTN_HEADS_PER_STEP * hd2
    n_strips = 2 * ROW_TILE // ATTN_STRIP
    resident = dict(pipeline_mode=pl.Buffered(1))
    return pl.pallas_call(
        _attn_prompt_kernel,
        grid=(H_DIFF // ATTN_HEADS_PER_STEP, nt),
        in_specs=[pl.BlockSpec((4, HEAD_DIM), lambda h, i: (0, 0)),
                  pl.BlockSpec((hd2, 1), lambda h, i: (0, 0)),
                  pl.BlockSpec((1, gw, 2 * ROW_TILE), lambda h, i: (i, h, 0)),
                  pl.BlockSpec((1, gw, 2 * ROW_TILE), lambda h, i: (jnp.minimum(i + 1, nt - 1), h, 0)),
                  pl.BlockSpec((rows, gw), lambda h, i: (0, h), **resident),
                  pl.BlockSpec((nt, gw, ROW_TILE), lambda h, i: (0, h, 0), **resident)],
        out_specs=pl.BlockSpec((gw, ROW_TILE), lambda h, i: (h, i)),
        out_shape=jax.ShapeDtypeStruct((D_ATTN, rows), F32),
        scratch_shapes=[pltpu.VMEM((ATTN_HEADS_PER_STEP, n_strips, hd2, ATTN_STRIP), F32),
                        pltpu.VMEM((ATTN_HEADS_PER_STEP, n_strips, ROW_TILE, ATTN_STRIP), F32)],
        compiler_params=pltpu.CompilerParams(
            dimension_semantics=("arbitrary", "arbitrary"), vmem_limit_bytes=_mib(56)),
        name="attn_prompt",
    )(lam, subln_col, q_t, q_t, k_b, v_t)


def _attn_decode_kernel(lam_ref, g_ref, q_ref, ck_ref, cv_ref, kn_ref, vn_ref, o_ref):
    lam = _lambda(lam_ref)
    hd2 = 2 * HEAD_DIM
    past = ck_ref.shape[2]
    nt = (((1,), (1,)), ((), ()))
    for h in range(H_DIFF):
        sl = slice(h * hd2, (h + 1) * hd2)
        q = q_ref[0, :, sl]
        t = q.shape[0]
        comp = lax.broadcasted_iota(jnp.int32, q.shape, 1)
        lhs = jnp.concatenate([jnp.where(comp < HEAD_DIM, q, 0.0),
                               jnp.where(comp >= HEAD_DIM, q, 0.0)], axis=0).astype(BF16)
        s_c = _dot(lhs, ck_ref[0, sl, :].astype(BF16))
        s_n = lax.dot_general(lhs, kn_ref[0, :, sl].astype(BF16), nt, preferred_element_type=F32)
        m = jnp.maximum(jnp.max(s_c, axis=1, keepdims=True), jnp.max(s_n, axis=1, keepdims=True))
        e_c = jnp.exp2(s_c - m)
        e_n = jnp.exp2(s_n - m)
        inv_l = 1.0 / (jnp.sum(e_c, axis=1, keepdims=True) + jnp.sum(e_n, axis=1, keepdims=True))
        p_c = e_c * inv_l
        p_n = e_n * inv_l
        a_c = (p_c[0:t] - lam * p_c[t:2 * t]).astype(BF16)
        a_n = (p_n[0:t] - lam * p_n[t:2 * t]).astype(BF16)
        v_c = cv_ref[0, pl.ds(h, past, stride=H_DIFF), :].astype(BF16)
        o = _dot(a_c, v_c) + _dot(a_n, vn_ref[0, :, sl].astype(BF16))
        o_ref[0, :, sl] = _rms(o, g_ref[...]) * (1.0 - LAM_INIT)


def _attn_decode(lam, subln_row, q, cache_kt, cache_v4, k_new, v_new):
    b, _, past = cache_kt.shape
    t = q.shape[1]
    seq = lambda n: pl.BlockSpec((1, n, D_ATTN), lambda i: (i, 0, 0))
    return pl.pallas_call(
        _attn_decode_kernel,
        grid=(b,),
        in_specs=[pl.BlockSpec((4, HEAD_DIM), lambda i: (0, 0)),
                  pl.BlockSpec((1, 2 * HEAD_DIM), lambda i: (0, 0)),
                  seq(t),
                  pl.BlockSpec((1, D_ATTN, past), lambda i: (i, 0, 0)),
                  pl.BlockSpec((1, past * H_DIFF, 2 * HEAD_DIM), lambda i: (i, 0, 0)),
                  seq(t), seq(t)],
        out_specs=seq(t),
        out_shape=jax.ShapeDtypeStruct((b, t, D_ATTN), F32),
        compiler_params=pltpu.CompilerParams(
            dimension_semantics=("arbitrary",), vmem_limit_bytes=_mib(56)),
        name="attn_decode",
    )(lam, subln_row, q, cache_kt, cache_v4, k_new, v_new)


def _merge_kernel(h_ref, attn_ref, gates_ref, halo_ref, body_ref, poolw_ref, pscale_ref,
                  wbp_ref, wba_ref, wout_ref, postg_ref, o_ref, *, hist0, attn_transposed):
    nb, tl, _ = body_ref.shape
    hist = hist0 + pl.program_id(0) * tl
    hi = hist + 1 + lax.broadcasted_iota(jnp.int32, (tl, 1), 0)
    halo_row = lax.broadcasted_iota(jnp.int32, (POOL_HALO, 1), 0)
    halo_ok = halo_row >= POOL_HALO - jnp.minimum(hist, POOL_STATE)
    pooled = []
    for b in range(nb):
        halo = jnp.where(halo_ok, halo_ref[b], 0.0)
        ext = jnp.concatenate([halo, body_ref[b]], axis=0)
        groups = []
        for g, win in enumerate(POOL_WINDOWS):
            x = ext[:, g * POOL_GROUP:(g + 1) * POOL_GROUP]
            s, shift = x, 1
            while shift < win:
                s = s + pltpu.roll(s, shift, 0)
                shift *= 2
            inv_cnt = 1.0 / jnp.minimum(hi, win).astype(F32)
            mixed = s[POOL_HALO:] * inv_cnt - x[POOL_HALO:]
            groups.append(_dot(mixed.astype(BF16), poolw_ref[g]))
        pooled.append(jnp.concatenate(groups, axis=1) * pscale_ref[...])
    pool_out = pooled[0] if nb == 1 else jnp.concatenate(pooled, axis=0)

    attn = attn_ref[...].T if attn_transposed else attn_ref[...]
    branch_pool = _dot(pool_out.astype(BF16), wbp_ref[...])
    branch_attn = _dot(attn.astype(BF16), wba_ref[...])
    merged = (gates_ref[:, 0:D_MODEL].astype(F32) * branch_pool
              + gates_ref[:, D_MODEL:2 * D_MODEL].astype(F32) * branch_attn)
    o_ref[...] = h_ref[...] + _rms(_dot(merged.astype(BF16), wout_ref[...]), postg_ref[...])


def _merge(h, attn, gates, halo, body, halo_spec, body_spec, hist0, attn_transposed,
           pool_w, pool_scale, w_bp, w_ba, w_out, post_g):
    rows = h.shape[0]
    row = lambda w: pl.BlockSpec((ROW_TILE, w), lambda i: (i, 0))
    attn_spec = (pl.BlockSpec((D_ATTN, ROW_TILE), lambda i: (0, i)) if attn_transposed
                 else row(D_ATTN))
    return pl.pallas_call(
        functools.partial(_merge_kernel, hist0=hist0, attn_transposed=attn_transposed),
        grid=(rows // ROW_TILE,),
        in_specs=[row(D_MODEL), attn_spec, row(2 * D_MODEL), halo_spec, body_spec,
                  _const_spec(pool_w.shape), _const_spec(pool_scale.shape),
                  _const_spec(w_bp.shape), _const_spec(w_ba.shape), _const_spec(w_out.shape),
                  _const_spec(post_g.shape)],
        out_specs=row(D_MODEL),
        out_shape=jax.ShapeDtypeStruct((rows, D_MODEL), F32),
        compiler_params=pltpu.CompilerParams(
            dimension_semantics=("arbitrary",), vmem_limit_bytes=_mib(48)),
        name="merge",
    )(h, attn, gates, halo, body, pool_w, pool_scale, w_bp, w_ba, w_out, post_g)


def kernel(x_prompt, x_sample, cache_k, cache_v, state_pool, ffn1_pre_g, ffn1_post_g, ffn1_w_gu, ffn1_w_down, mix_pre_g, mix_post_g, w_in, pool_w, pool_scale, lambda_q1, lambda_k1, lambda_q2, lambda_k2, subln_g, w_branch_pool, w_branch_attn, w_out, ffn2_pre_g, ffn2_post_g, ffn2_w_gu, ffn2_w_down):
    bp, seq, _ = x_prompt.shape
    db, dseq, _ = x_sample.shape
    past = cache_k.shape[2]
    assert bp == 1 and seq % ROW_TILE == 0 and db * dseq == ROW_TILE and dseq >= POOL_STATE
    assert (past + dseq - 1) // CHUNK <= past // CHUNK and past % CHUNK == 0

    l = 0
    bf = lambda w: w[l].astype(BF16)
    g1pre, g1post, gmpre, gmpost = ffn1_pre_g[l:l + 1], ffn1_post_g[l:l + 1], mix_pre_g[l:l + 1], mix_post_g[l:l + 1]
    g2pre, g2post = ffn2_pre_g[l:l + 1], ffn2_post_g[l:l + 1]
    w1gu, w1d, w2gu, w2d = bf(ffn1_w_gu), bf(ffn1_w_down), bf(ffn2_w_gu), bf(ffn2_w_down)
    win, wpool, wbp, wba, wo = bf(w_in), bf(pool_w), bf(w_branch_pool), bf(w_branch_attn), bf(w_out)
    pscale = pool_scale[l:l + 1]
    lam = jnp.concatenate([lambda_q1[l:l + 1], lambda_k1[l:l + 1],
                           lambda_q2[l:l + 1], lambda_k2[l:l + 1]], axis=0)
    subln_row = subln_g[l:l + 1]
    subln_col = subln_g[l][:, None]
    merge_w = (wpool, pscale, wbp, wba, wo, gmpost)

    xp = x_prompt.reshape(seq, D_MODEL)
    nt = seq // ROW_TILE
    hp = _ffn(xp, g1pre, g1post, w1gu, w1d)
    upool, kf, vf, gates, q_t, k_b, v_t = _mix_in(
        hp, gmpre, win, np.arange(nt) * ROW_TILE, np.arange(ROW_TILE), transposed=True)
    attn_t = _attn_prompt(lam, subln_col, q_t, k_b, v_t)
    per_tile = ROW_TILE // POOL_HALO
    halo_spec = pl.BlockSpec((1, POOL_HALO, D_POOL),
                             lambda i: (jnp.maximum(i * per_tile - 1, 0), 0, 0))
    body_spec = pl.BlockSpec((1, ROW_TILE, D_POOL), lambda i: (i, 0, 0))
    hp = _merge(hp, attn_t, gates, upool.reshape(seq // POOL_HALO, POOL_HALO, D_POOL),
                upool.reshape(nt, ROW_TILE, D_POOL), halo_spec, body_spec, 0, True, *merge_w)
    y_prompt = _ffn(hp, g2pre, g2post, w2gu, w2d).reshape(bp, seq, D_MODEL)
    new_k_p = kf.reshape(1, bp, seq, H_DIFF, 2, HEAD_DIM)
    new_v_p = vf.reshape(1, bp, seq, H_DIFF, 2 * HEAD_DIM)
    new_pool_p = upool[seq - POOL_STATE:].reshape(1, bp, POOL_STATE, D_POOL)

    xs = x_sample.reshape(db * dseq, D_MODEL)
    hs = _ffn(xs, g1pre, g1post, w1gu, w1d)
    upool_s, kf_s, vf_s, gates_s, q_s, _, _ = _mix_in(
        hs, gmpre, win, np.array([past]), np.arange(ROW_TILE) % dseq, transposed=False)
    cache_kt = jnp.transpose(cache_k[l], (0, 2, 3, 4, 1)).reshape(db, D_ATTN, past)
    cache_v4 = cache_v[l].reshape(db, past * H_DIFF, 2 * HEAD_DIM)
    attn_s = _attn_decode(lam, subln_row, q_s.reshape(db, dseq, D_ATTN), cache_kt, cache_v4,
                          kf_s.reshape(db, dseq, D_ATTN), vf_s.reshape(db, dseq, D_ATTN))
    hist = state_pool[l]
    halo_s = jnp.concatenate(
        [jnp.zeros((db, POOL_HALO - POOL_STATE, D_POOL), F32), hist], axis=1)
    body_s = upool_s.reshape(db, dseq, D_POOL)
    hs = _merge(hs, attn_s.reshape(db * dseq, D_ATTN), gates_s, halo_s, body_s,
                pl.BlockSpec((db, POOL_HALO, D_POOL), lambda i: (0, 0, 0)),
                pl.BlockSpec((db, dseq, D_POOL), lambda i: (0, 0, 0)),
                POOL_STATE, False, *merge_w)
    y_sample = _ffn(hs, g2pre, g2post, w2gu, w2d).reshape(db, dseq, D_MODEL)
    new_k_s = kf_s.reshape(1, db, dseq, H_DIFF, 2, HEAD_DIM)
    new_v_s = vf_s.reshape(1, db, dseq, H_DIFF, 2 * HEAD_DIM)
    new_pool_s = body_s[:, dseq - POOL_STATE:].reshape(1, db, POOL_STATE, D_POOL)

    return (y_prompt, y_sample, new_k_p, new_v_p, new_pool_p, new_k_s, new_v_s, new_pool_s)
```

```python
import functools
import math

import numpy as np
import jax
import jax.numpy as jnp
from jax import lax
from jax.experimental import pallas as pl
from jax.experimental.pallas import tpu as pltpu

D_MODEL = 1024
D_FF = 2816
D_POOL = 512
POOL_WINDOWS = (2, 4, 8, 16)
POOL_GROUP = 128
POOL_STATE = 15
H_DIFF = 4
HEAD_DIM = 64
D_ATTN = 512
ROT_DIM = 16
ROPE_THETA = 500000.0
CHUNK = 64
EPS = 1e-6
LAM_INIT = 0.8 - 0.6 * math.exp(-0.3 * 0)

V7X_LANES = 128
BF16_SUBLANES = 16
POOL_HALO = 16
ROW_TILE = 512
FF_CHUNKS = ((0, 1024), (1024, 1024), (2048, 768))
ATTN_STRIP = 256
ATTN_HEADS_PER_STEP = 4
QK_SCALE_LOG2 = HEAD_DIM ** -0.5 * math.log2(math.e)

F32 = jnp.float32
BF16 = jnp.bfloat16


def _mib(n):
    return int(n * 1024 * 1024)


def _rms(x, g):
    return x * lax.rsqrt(jnp.mean(x * x, axis=-1, keepdims=True) + EPS) * g


def _dot(a, b):
    return jnp.dot(a, b, preferred_element_type=F32)


def _const_spec(shape):
    nd = len(shape)
    return pl.BlockSpec(shape, lambda *_: (0,) * nd, pipeline_mode=pl.Buffered(1))


def _ffn_kernel(x_ref, pre_ref, post_ref, wgu_ref, wd_ref, o_ref):
    x = x_ref[...]
    xn = _rms(x, pre_ref[...]).astype(BF16)
    acc = None
    for c0, cw in FF_CHUNKS:
        gate = _dot(xn, wgu_ref[:, c0:c0 + cw])
        up = _dot(xn, wgu_ref[:, D_FF + c0:D_FF + c0 + cw])
        act = (gate * jax.nn.sigmoid(gate) * up).astype(BF16)
        part = _dot(act, wd_ref[c0:c0 + cw, :])
        acc = part if acc is None else acc + part
    o_ref[...] = x + 0.5 * _rms(acc, post_ref[...])


def _ffn(x, pre_g, post_g, w_gu, w_down):
    rows = x.shape[0]
    row_spec = pl.BlockSpec((ROW_TILE, D_MODEL), lambda i: (i, 0))
    return pl.pallas_call(
        _ffn_kernel,
        grid=(rows // ROW_TILE,),
        in_specs=[row_spec, _const_spec((1, D_MODEL)), _const_spec((1, D_MODEL)),
                  _const_spec((D_MODEL, 2 * D_FF)), _const_spec((D_FF, D_MODEL))],
        out_specs=row_spec,
        out_shape=jax.ShapeDtypeStruct((rows, D_MODEL), F32),
        compiler_params=pltpu.CompilerParams(
            dimension_semantics=("arbitrary",), vmem_limit_bytes=_mib(48)),
        name="ffn",
    )(x, pre_g, post_g, w_gu, w_down)


def _rope_tables(tile_pos0, row_pos):
    d = np.arange(V7X_LANES) % HEAD_DIM
    inv = np.where(d < ROT_DIM, ROPE_THETA ** (-(d % (ROT_DIM // 2)) / (ROT_DIM // 2)), 0.0)
    ang_a = np.asarray(tile_pos0, np.float64)[:, None] * inv[None, :]
    ang_b = np.asarray(row_pos, np.float64)[:, None] * inv[None, :]
    half = ROT_DIM // 2
    sel_lo = np.where(d < half, -1.0, 0.0)[None, :]
    sel_hi = np.where((d >= half) & (d < ROT_DIM), 1.0, 0.0)[None, :]
    f = lambda a: jnp.asarray(a, F32)
    return (f(np.cos(ang_a))[:, None, :], f(np.sin(ang_a))[:, None, :],
            f(np.cos(ang_b)), f(np.sin(ang_b)), f(sel_lo), f(sel_hi))


def _mix_in_kernel(h_ref, g_ref, win_ref, cos_a_ref, sin_a_ref, cos_b_ref, sin_b_ref,
                   sel_lo_ref, sel_hi_ref,
                   upool_ref, kf_ref, vf_ref, gates_ref, q_ref, kb_ref, v_ref, *, transposed):
    u = _rms(h_ref[...], g_ref[...]).astype(BF16)
    upool_ref[...] = _dot(u, win_ref[:, 0:D_POOL])

    ca, sa = cos_a_ref[0], sin_a_ref[0]
    cb, sb = cos_b_ref[...], sin_b_ref[...]
    cos = ca * cb - sa * sb
    sin = sa * cb + ca * sb
    rep = D_ATTN // V7X_LANES
    cos_f = jnp.concatenate([cos] * rep, axis=1)
    sin_lo = jnp.concatenate([sin * sel_lo_ref[...]] * rep, axis=1)
    sin_hi = jnp.concatenate([sin * sel_hi_ref[...]] * rep, axis=1)
    half = ROT_DIM // 2

    def rope(x):
        return (x * cos_f + pltpu.roll(x, D_ATTN - half, 1) * sin_lo
                + pltpu.roll(x, half, 1) * sin_hi)

    q = rope(_dot(u, win_ref[:, D_POOL:D_POOL + D_ATTN])) * QK_SCALE_LOG2
    k = rope(_dot(u, win_ref[:, D_POOL + D_ATTN:D_POOL + 2 * D_ATTN]))
    v = _dot(u, win_ref[:, D_POOL + 2 * D_ATTN:D_POOL + 3 * D_ATTN])
    kf_ref[...] = k
    vf_ref[...] = v.reshape(v.shape[0], H_DIFF, 2 * HEAD_DIM)
    for hh in range(H_DIFF):
        kb_ref[hh] = k[:, hh * 2 * HEAD_DIM:(hh + 1) * 2 * HEAD_DIM].astype(BF16)
    if transposed:
        comp = lax.broadcasted_iota(jnp.int32, q.shape, 1) % (2 * HEAD_DIM)
        q_top = jnp.where(comp < HEAD_DIM, q, 0.0)
        q_bot = jnp.where(comp >= HEAD_DIM, q, 0.0)
        q_ref[0, :, 0:ROW_TILE] = q_top.T.astype(BF16)
        q_ref[0, :, ROW_TILE:2 * ROW_TILE] = q_bot.T.astype(BF16)
        v_ref[0] = v.T.astype(BF16)
    else:
        q_ref[...] = q
        v_ref[...] = v.astype(BF16)

    g0 = D_POOL + 3 * D_ATTN
    for c in range(2):
        z = _dot(u, win_ref[:, g0 + c * D_MODEL:g0 + (c + 1) * D_MODEL])
        gates_ref[:, c * D_MODEL:(c + 1) * D_MODEL] = jax.nn.sigmoid(z).astype(BF16)


def _mix_in(h, mix_pre_g, w_in, tile_pos0, row_pos, transposed):
    rows = h.shape[0]
    nt = rows // ROW_TILE
    tables = _rope_tables(tile_pos0, row_pos)
    row = lambda w: pl.BlockSpec((ROW_TILE, w), lambda i: (i, 0))
    if transposed:
        q_shape = jax.ShapeDtypeStruct((nt, D_ATTN, 2 * ROW_TILE), BF16)
        q_spec = pl.BlockSpec((1, D_ATTN, 2 * ROW_TILE), lambda i: (i, 0, 0))
        v_shape = jax.ShapeDtypeStruct((nt, D_ATTN, ROW_TILE), BF16)
        v_spec = pl.BlockSpec((1, D_ATTN, ROW_TILE), lambda i: (i, 0, 0))
    else:
        q_shape, q_spec = jax.ShapeDtypeStruct((rows, D_ATTN), F32), row(D_ATTN)
        v_shape, v_spec = jax.ShapeDtypeStruct((rows, D_ATTN), BF16), row(D_ATTN)
    return pl.pallas_call(
        functools.partial(_mix_in_kernel, transposed=transposed),
        grid=(nt,),
        in_specs=[row(D_MODEL), _const_spec((1, D_MODEL)), _const_spec(w_in.shape),
                  pl.BlockSpec((1, 1, V7X_LANES), lambda i: (i, 0, 0)),
                  pl.BlockSpec((1, 1, V7X_LANES), lambda i: (i, 0, 0)),
                  _const_spec((ROW_TILE, V7X_LANES)), _const_spec((ROW_TILE, V7X_LANES)),
                  _const_spec((1, V7X_LANES)), _const_spec((1, V7X_LANES))],
        out_specs=[row(D_POOL), row(D_ATTN),
                   pl.BlockSpec((ROW_TILE, H_DIFF, 2 * HEAD_DIM), lambda i: (i, 0, 0)),
                   row(2 * D_MODEL),
                   q_spec,
                   pl.BlockSpec((H_DIFF, ROW_TILE, 2 * HEAD_DIM), lambda i: (0, i, 0)),
                   v_spec],
        out_shape=[jax.ShapeDtypeStruct((rows, D_POOL), F32),
                   jax.ShapeDtypeStruct((rows, D_ATTN), F32),
                   jax.ShapeDtypeStruct((rows, H_DIFF, 2 * HEAD_DIM), F32),
                   jax.ShapeDtypeStruct((rows, 2 * D_MODEL), BF16),
                   q_shape,
                   jax.ShapeDtypeStruct((H_DIFF, rows, 2 * HEAD_DIM), BF16),
                   v_shape],
        compiler_params=pltpu.CompilerParams(
            dimension_semantics=("arbitrary",), vmem_limit_bytes=_mib(48)),
        name="mix_in",
    )(h, mix_pre_g, w_in, *tables)


def _lambda(lam_ref):
    lq1, lk1, lq2, lk2 = (lam_ref[i:i + 1, :] for i in range(4))
    return (jnp.exp(jnp.sum(lq1 * lk1, axis=1, keepdims=True))
            - jnp.exp(jnp.sum(lq2 * lk2, axis=1, keepdims=True)) + LAM_INIT)


def _attn_prompt_kernel(lam_ref, g_ref, q_ref, qn_ref, k_ref, v_ref, o_ref, acc_ref, s_ref):
    tb = ROW_TILE
    hd2 = 2 * HEAD_DIM
    i = pl.program_id(1)
    acc_ref[...] = jnp.zeros_like(acc_ref)
    heads = [slice(h * hd2, (h + 1) * hd2) for h in range(ATTN_HEADS_PER_STEP)]
    strips = [slice(c, c + ATTN_STRIP) for c in range(0, 2 * tb, ATTN_STRIP)]
    ones = jnp.ones((BF16_SUBLANES, tb), BF16)

    def logits(j, rows, cols, q_blk):
        kj = k_ref[rows.start // hd2, pl.ds(pl.multiple_of(j * tb, tb), tb), :]
        return _dot(kj, q_blk[0, rows, cols])

    def consume(j, m, l, diagonal, next_logits):
        m = [list(x) for x in m]
        l = [list(x) for x in l]
        for n, cols in enumerate(strips):
            q0 = cols.start % tb
            keys = q0 + ATTN_STRIP if diagonal else tb
            for h, rows in enumerate(heads):
                s_next = next_logits(rows, cols)

                def s_strip():
                    s = s_ref[h, n, 0:keys, :]
                    if diagonal:
                        k_chunk = lax.broadcasted_iota(jnp.int32, s.shape, 0) // CHUNK
                        q_chunk = (q0 + lax.broadcasted_iota(jnp.int32, s.shape, 1)) // CHUNK
                        s = jnp.where(k_chunk <= q_chunk, s, -jnp.inf)
                    return s
                m_new = jnp.maximum(m[h][n], jnp.max(s_strip(), axis=0, keepdims=True))
                alpha = jnp.exp2(m[h][n] - m_new)
                e = jnp.exp2((s_strip() - m_new).astype(BF16))
                pv = _dot(jnp.concatenate([v_ref[j, rows, 0:keys], ones[:, 0:keys]], axis=0), e)
                l[h][n] = alpha * l[h][n] + pv[hd2:hd2 + 1, :]
                m[h][n] = m_new
                acc_ref[h, n] = acc_ref[h, n] * alpha + pv[0:hd2, :]
                s_ref[h, n] = s_next
        return m, l

    @pl.when(i == 0)
    def _():
        for h, rows in enumerate(heads):
            for n, cols in enumerate(strips):
                s_ref[h, n] = logits(0, rows, cols, q_ref)

    m0 = [[jnp.full((1, ATTN_STRIP), -jnp.inf, F32) for _ in strips] for _ in heads]
    l0 = [[jnp.zeros((1, ATTN_STRIP), F32) for _ in strips] for _ in heads]
    def full_block(j, ml):
        return consume(j, ml[0], ml[1], False, lambda r, c: logits(j + 1, r, c, q_ref))

    m, l = lax.fori_loop(
        0, i // 2, lambda t, ml: full_block(2 * t + 1, full_block(2 * t, ml)), (m0, l0))
    m, l = lax.cond(i % 2 == 1, lambda ml: full_block(i - 1, ml), lambda ml: ml, (m, l))
    m, l = consume(i, m, l, True, lambda r, c: logits(0, r, c, qn_ref))

    lam = _lambda(lam_ref)
    for h, rows in enumerate(heads):
        o = jnp.concatenate([acc_ref[h, n] * (1.0 / l[h][n]) for n in range(len(strips))], axis=1)
        o = o[:, 0:tb] - lam * o[:, tb:2 * tb]
        ms = jnp.mean(o * o, axis=0, keepdims=True)
        o_ref[rows, :] = o * lax.rsqrt(ms + EPS) * g_ref[...] * (1.0 - LAM_INIT)


def _attn_prompt(lam, subln_col, q_t, k_b, v_t):
    nt = q_t.shape[0]
    rows = nt * ROW_TILE
    hd2 = 2 * HEAD_DIM
    gw = ATTN_HEADS_PER_STEP * hd2
    n_strips = 2 * ROW_TILE // ATTN_STRIP
    resident = dict(pipeline_mode=pl.Buffered(1))
    return pl.pallas_call(
        _attn_prompt_kernel,
        grid=(H_DIFF // ATTN_HEADS_PER_STEP, nt),
        in_specs=[pl.BlockSpec((4, HEAD_DIM), lambda h, i: (0, 0)),
                  pl.BlockSpec((hd2, 1), lambda h, i: (0, 0)),
                  pl.BlockSpec((1, gw, 2 * ROW_TILE), lambda h, i: (i, h, 0)),
                  pl.BlockSpec((1, gw, 2 * ROW_TILE), lambda h, i: (jnp.minimum(i + 1, nt - 1), h, 0)),
                  pl.BlockSpec((ATTN_HEADS_PER_STEP, rows, hd2), lambda h, i: (h, 0, 0), **resident),
                  pl.BlockSpec((nt, gw, ROW_TILE), lambda h, i: (0, h, 0), **resident)],
        out_specs=pl.BlockSpec((gw, ROW_TILE), lambda h, i: (h, i)),
        out_shape=jax.ShapeDtypeStruct((D_ATTN, rows), F32),
        scratch_shapes=[pltpu.VMEM((ATTN_HEADS_PER_STEP, n_strips, hd2, ATTN_STRIP), F32),
                        pltpu.VMEM((ATTN_HEADS_PER_STEP, n_strips, ROW_TILE, ATTN_STRIP), F32)],
        compiler_params=pltpu.CompilerParams(
            dimension_semantics=("arbitrary", "arbitrary"), vmem_limit_bytes=_mib(56)),
        name="attn_prompt",
    )(lam, subln_col, q_t, q_t, k_b, v_t)


def _attn_decode_kernel(lam_ref, g_ref, q_ref, ck_ref, cv_ref, kn_ref, vn_ref, o_ref):
    lam = _lambda(lam_ref)
    hd2 = 2 * HEAD_DIM
    past = ck_ref.shape[2]
    nt = (((1,), (1,)), ((), ()))
    for h in range(H_DIFF):
        sl = slice(h * hd2, (h + 1) * hd2)
        q = q_ref[0, :, sl]
        t = q.shape[0]
        comp = lax.broadcasted_iota(jnp.int32, q.shape, 1)
        lhs = jnp.concatenate([jnp.where(comp < HEAD_DIM, q, 0.0),
                               jnp.where(comp >= HEAD_DIM, q, 0.0)], axis=0).astype(BF16)
        s_c = _dot(lhs, ck_ref[0, sl, :].astype(BF16))
        s_n = lax.dot_general(lhs, kn_ref[0, :, sl].astype(BF16), nt, preferred_element_type=F32)
        m = jnp.maximum(jnp.max(s_c, axis=1, keepdims=True), jnp.max(s_n, axis=1, keepdims=True))
        e_c = jnp.exp2(s_c - m)
        e_n = jnp.exp2(s_n - m)
        inv_l = 1.0 / (jnp.sum(e_c, axis=1, keepdims=True) + jnp.sum(e_n, axis=1, keepdims=True))
        p_c = e_c * inv_l
        p_n = e_n * inv_l
        a_c = (p_c[0:t] - lam * p_c[t:2 * t]).astype(BF16)
        a_n = (p_n[0:t] - lam * p_n[t:2 * t]).astype(BF16)
        v_c = cv_ref[0, pl.ds(h, past, stride=H_DIFF), :].astype(BF16)
        o = _dot(a_c, v_c) + _dot(a_n, vn_ref[0, :, sl].astype(BF16))
        o_ref[0, :, sl] = _rms(o, g_ref[...]) * (1.0 - LAM_INIT)


def _attn_decode(lam, subln_row, q, cache_kt, cache_v4, k_new, v_new):
    b, _, past = cache_kt.shape
    t = q.shape[1]
    seq = lambda n: pl.BlockSpec((1, n, D_ATTN), lambda i: (i, 0, 0))
    return pl.pallas_call(
        _attn_decode_kernel,
        grid=(b,),
        in_specs=[pl.BlockSpec((4, HEAD_DIM), lambda i: (0, 0)),
                  pl.BlockSpec((1, 2 * HEAD_DIM), lambda i: (0, 0)),
                  seq(t),
                  pl.BlockSpec((1, D_ATTN, past), lambda i: (i, 0, 0)),
                  pl.BlockSpec((1, past * H_DIFF, 2 * HEAD_DIM), lambda i: (i, 0, 0)),
                  seq(t), seq(t)],
        out_specs=seq(t),
        out_shape=jax.ShapeDtypeStruct((b, t, D_ATTN), F32),
        compiler_params=pltpu.CompilerParams(
            dimension_semantics=("arbitrary",), vmem_limit_bytes=_mib(56)),
        name="attn_decode",
    )(lam, subln_row, q, cache_kt, cache_v4, k_new, v_new)


def _merge_kernel(h_ref, attn_ref, gates_ref, halo_ref, body_ref, poolw_ref, pscale_ref,
                  wbp_ref, wba_ref, wout_ref, postg_ref, o_ref, *, hist0, attn_transposed):
    nb, tl, _ = body_ref.shape
    hist = hist0 + pl.program_id(0) * tl
    hi = hist + 1 + lax.broadcasted_iota(jnp.int32, (tl, 1), 0)
    halo_row = lax.broadcasted_iota(jnp.int32, (POOL_HALO, 1), 0)
    halo_ok = halo_row >= POOL_HALO - jnp.minimum(hist, POOL_STATE)
    pooled = []
    for b in range(nb):
        halo = jnp.where(halo_ok, halo_ref[b], 0.0)
        ext = jnp.concatenate([halo, body_ref[b]], axis=0)
        groups = []
        for g, win in enumerate(POOL_WINDOWS):
            x = ext[:, g * POOL_GROUP:(g + 1) * POOL_GROUP]
            s, shift = x, 1
            while shift < win:
                s = s + pltpu.roll(s, shift, 0)
                shift *= 2
            inv_cnt = 1.0 / jnp.minimum(hi, win).astype(F32)
            mixed = s[POOL_HALO:] * inv_cnt - x[POOL_HALO:]
            groups.append(_dot(mixed.astype(BF16), poolw_ref[g]))
        pooled.append(jnp.concatenate(groups, axis=1) * pscale_ref[...])
    pool_out = pooled[0] if nb == 1 else jnp.concatenate(pooled, axis=0)

    attn = attn_ref[...].T if attn_transposed else attn_ref[...]
    branch_pool = _dot(pool_out.astype(BF16), wbp_ref[...])
    branch_attn = _dot(attn.astype(BF16), wba_ref[...])
    merged = (gates_ref[:, 0:D_MODEL].astype(F32) * branch_pool
              + gates_ref[:, D_MODEL:2 * D_MODEL].astype(F32) * branch_attn)
    o_ref[...] = h_ref[...] + _rms(_dot(merged.astype(BF16), wout_ref[...]), postg_ref[...])


def _merge(h, attn, gates, halo, body, halo_spec, body_spec, hist0, attn_transposed,
           pool_w, pool_scale, w_bp, w_ba, w_out, post_g):
    rows = h.shape[0]
    row = lambda w: pl.BlockSpec((ROW_TILE, w), lambda i: (i, 0))
    attn_spec = (pl.BlockSpec((D_ATTN, ROW_TILE), lambda i: (0, i)) if attn_transposed
                 else row(D_ATTN))
    return pl.pallas_call(
        functools.partial(_merge_kernel, hist0=hist0, attn_transposed=attn_transposed),
        grid=(rows // ROW_TILE,),
        in_specs=[row(D_MODEL), attn_spec, row(2 * D_MODEL), halo_spec, body_spec,
                  _const_spec(pool_w.shape), _const_spec(pool_scale.shape),
                  _const_spec(w_bp.shape), _const_spec(w_ba.shape), _const_spec(w_out.shape),
                  _const_spec(post_g.shape)],
        out_specs=row(D_MODEL),
        out_shape=jax.ShapeDtypeStruct((rows, D_MODEL), F32),
        compiler_params=pltpu.CompilerParams(
            dimension_semantics=("arbitrary",), vmem_limit_bytes=_mib(48)),
        name="merge",
    )(h, attn, gates, halo, body, pool_w, pool_scale, w_bp, w_ba, w_out, post_g)


def kernel(x_prompt, x_sample, cache_k, cache_v, state_pool, ffn1_pre_g, ffn1_post_g, ffn1_w_gu, ffn1_w_down, mix_pre_g, mix_post_g, w_in, pool_w, pool_scale, lambda_q1, lambda_k1, lambda_q2, lambda_k2, subln_g, w_branch_pool, w_branch_attn, w_out, ffn2_pre_g, ffn2_post_g, ffn2_w_gu, ffn2_w_down):
    bp, seq, _ = x_prompt.shape
    db, dseq, _ = x_sample.shape
    past = cache_k.shape[2]
    assert bp == 1 and seq % ROW_TILE == 0 and db * dseq == ROW_TILE and dseq >= POOL_STATE
    assert (past + dseq - 1) // CHUNK <= past // CHUNK and past % CHUNK == 0

    l = 0
    bf = lambda w: w[l].astype(BF16)
    g1pre, g1post, gmpre, gmpost = ffn1_pre_g[l:l + 1], ffn1_post_g[l:l + 1], mix_pre_g[l:l + 1], mix_post_g[l:l + 1]
    g2pre, g2post = ffn2_pre_g[l:l + 1], ffn2_post_g[l:l + 1]
    w1gu, w1d, w2gu, w2d = bf(ffn1_w_gu), bf(ffn1_w_down), bf(ffn2_w_gu), bf(ffn2_w_down)
    win, wpool, wbp, wba, wo = bf(w_in), bf(pool_w), bf(w_branch_pool), bf(w_branch_attn), bf(w_out)
    pscale = pool_scale[l:l + 1]
    lam = jnp.concatenate([lambda_q1[l:l + 1], lambda_k1[l:l + 1],
                           lambda_q2[l:l + 1], lambda_k2[l:l + 1]], axis=0)
    subln_row = subln_g[l:l + 1]
    subln_col = subln_g[l][:, None]
    merge_w = (wpool, pscale, wbp, wba, wo, gmpost)

    xp = x_prompt.reshape(seq, D_MODEL)
    nt = seq // ROW_TILE
    hp = _ffn(xp, g1pre, g1post, w1gu, w1d)
    upool, kf, vf, gates, q_t, k_b, v_t = _mix_in(
        hp, gmpre, win, np.arange(nt) * ROW_TILE, np.arange(ROW_TILE), transposed=True)
    attn_t = _attn_prompt(lam, subln_col, q_t, k_b, v_t)
    per_tile = ROW_TILE // POOL_HALO
    halo_spec = pl.BlockSpec((1, POOL_HALO, D_POOL),
                             lambda i: (jnp.maximum(i * per_tile - 1, 0), 0, 0))
    body_spec = pl.BlockSpec((1, ROW_TILE, D_POOL), lambda i: (i, 0, 0))
    hp = _merge(hp, attn_t, gates, upool.reshape(seq // POOL_HALO, POOL_HALO, D_POOL),
                upool.reshape(nt, ROW_TILE, D_POOL), halo_spec, body_spec, 0, True, *merge_w)
    y_prompt = _ffn(hp, g2pre, g2post, w2gu, w2d).reshape(bp, seq, D_MODEL)
    new_k_p = kf.reshape(1, bp, seq, H_DIFF, 2, HEAD_DIM)
    new_v_p = vf.reshape(1, bp, seq, H_DIFF, 2 * HEAD_DIM)
    new_pool_p = upool[seq - POOL_STATE:].reshape(1, bp, POOL_STATE, D_POOL)

    xs = x_sample.reshape(db * dseq, D_MODEL)
    hs = _ffn(xs, g1pre, g1post, w1gu, w1d)
    upool_s, kf_s, vf_s, gates_s, q_s, _, _ = _mix_in(
        hs, gmpre, win, np.array([past]), np.arange(ROW_TILE) % dseq, transposed=False)
    cache_kt = jnp.transpose(cache_k[l], (0, 2, 3, 4, 1)).reshape(db, D_ATTN, past)
    cache_v4 = cache_v[l].reshape(db, past * H_DIFF, 2 * HEAD_DIM)
    attn_s = _attn_decode(lam, subln_row, q_s.reshape(db, dseq, D_ATTN), cache_kt, cache_v4,
                          kf_s.reshape(db, dseq, D_ATTN), vf_s.reshape(db, dseq, D_ATTN))
    hist = state_pool[l]
    halo_s = jnp.concatenate(
        [jnp.zeros((db, POOL_HALO - POOL_STATE, D_POOL), F32), hist], axis=1)
    body_s = upool_s.reshape(db, dseq, D_POOL)
    hs = _merge(hs, attn_s.reshape(db * dseq, D_ATTN), gates_s, halo_s, body_s,
                pl.BlockSpec((db, POOL_HALO, D_POOL), lambda i: (0, 0, 0)),
                pl.BlockSpec((db, dseq, D_POOL), lambda i: (0, 0, 0)),
                POOL_STATE, False, *merge_w)
    y_sample = _ffn(hs, g2pre, g2post, w2gu, w2d).reshape(db, dseq, D_MODEL)
    new_k_s = kf_s.reshape(1, db, dseq, H_DIFF, 2, HEAD_DIM)
    new_v_s = vf_s.reshape(1, db, dseq, H_DIFF, 2 * HEAD_DIM)
    new_pool_s = body_s[:, dseq - POOL_STATE:].reshape(1, db, POOL_STATE, D_POOL)

    return (y_prompt, y_sample, new_k_p, new_v_p, new_pool_p, new_k_s, new_v_s, new_pool_s)
```

```python
import functools
import math

import numpy as np
import jax
import jax.numpy as jnp
from jax import lax
from jax.experimental import pallas as pl
from jax.experimental.pallas import tpu as pltpu

D_MODEL = 1024
D_FF = 2816
D_POOL = 512
POOL_WINDOWS = (2, 4, 8, 16)
POOL_GROUP = 128
POOL_STATE = 15
H_DIFF = 4
HEAD_DIM = 64
D_ATTN = 512
ROT_DIM = 16
ROPE_THETA = 500000.0
CHUNK = 64
EPS = 1e-6
LAM_INIT = 0.8 - 0.6 * math.exp(-0.3 * 0)

V7X_LANES = 128
BF16_SUBLANES = 16
POOL_HALO = 16
ROW_TILE = 512
FF_CHUNKS = ((0, 1024), (1024, 1024), (2048, 768))
ATTN_STRIP = 256
ATTN_HEADS_PER_STEP = 4
QK_SCALE_LOG2 = HEAD_DIM ** -0.5 * math.log2(math.e)

F32 = jnp.float32
BF16 = jnp.bfloat16


def _mib(n):
    return int(n * 1024 * 1024)


def _rms(x, g):
    return x * lax.rsqrt(jnp.mean(x * x, axis=-1, keepdims=True) + EPS) * g


def _dot(a, b):
    return jnp.dot(a, b, preferred_element_type=F32)


def _const_spec(shape):
    nd = len(shape)
    return pl.BlockSpec(shape, lambda *_: (0,) * nd, pipeline_mode=pl.Buffered(1))


def _decode_attn_stages(first, last, lam_ref, g_ref, q_ref, ck_ref, cv_ref, kn_ref, vn_ref,
                        o_ref, m_ref, l_ref, acc_ref):
    hd2 = 2 * HEAD_DIM
    keys = ck_ref.shape[2]
    nt_dims = (((1,), (1,)), ((), ()))
    heads = [slice(h * hd2, (h + 1) * hd2) for h in range(H_DIFF)]
    state = {}

    def scores():
        for h, sl in enumerate(heads):
            q = q_ref[0, :, sl]
            comp = lax.broadcasted_iota(jnp.int32, q.shape, 1)
            lhs = jnp.concatenate([jnp.where(comp < HEAD_DIM, q, 0.0),
                                   jnp.where(comp >= HEAD_DIM, q, 0.0)], axis=0).astype(BF16)
            s_c = _dot(lhs, ck_ref[0, sl, :].astype(BF16))
            s_n = lax.dot_general(lhs, kn_ref[0, :, sl].astype(BF16), nt_dims,
                                  preferred_element_type=F32)
            s_n = jnp.where(last, s_n, -jnp.inf)
            m_old = jnp.where(first, -jnp.inf, m_ref[h])
            m_new = jnp.maximum(m_old, jnp.maximum(jnp.max(s_c, axis=1, keepdims=True),
                                                   jnp.max(s_n, axis=1, keepdims=True)))
            alpha = jnp.exp2(m_old - m_new)
            e_c = jnp.exp2(s_c - m_new)
            e_n = jnp.exp2(s_n - m_new)
            l_new = (alpha * jnp.where(first, 0.0, l_ref[h])
                     + jnp.sum(e_c, axis=1, keepdims=True) + jnp.sum(e_n, axis=1, keepdims=True))
            m_ref[h] = m_new
            l_ref[h] = l_new
            state[h] = (alpha, e_c.astype(BF16), e_n.astype(BF16), l_new)

    def outputs():
        lam = _lambda(lam_ref)
        for h, sl in enumerate(heads):
            alpha, e_c, e_n, l_new = state[h]
            v_c = cv_ref[0, pl.ds(h, keys, stride=H_DIFF), :].astype(BF16)
            acc = (alpha * jnp.where(first, 0.0, acc_ref[h])
                   + _dot(e_c, v_c) + _dot(e_n, vn_ref[0, :, sl].astype(BF16)))
            acc_ref[h] = acc
            o = acc * (1.0 / l_new)
            t = o.shape[0] // 2
            o = o[0:t] - lam * o[t:2 * t]
            o_ref[0, :, sl] = _rms(o, g_ref[...]) * (1.0 - LAM_INIT)

    return scores, outputs


def _ffn_kernel(*refs, decode_parts):
    if decode_parts:
        (x_ref, pre_ref, post_ref, wgu_ref, wd_ref, *attn_in, o_ref, attn_o_ref,
         m_ref, l_ref, acc_ref) = refs
        step = pl.program_id(0)

        @pl.when(step == 0)
        def _():
            m_ref[...] = jnp.zeros_like(m_ref)
            l_ref[...] = jnp.zeros_like(l_ref)
            acc_ref[...] = jnp.zeros_like(acc_ref)

        part = step % decode_parts
        side = _decode_attn_stages(part == 0, part == decode_parts - 1, *attn_in,
                                   attn_o_ref, m_ref, l_ref, acc_ref)
    else:
        x_ref, pre_ref, post_ref, wgu_ref, wd_ref, o_ref = refs
        side = ()
    x = x_ref[...]
    xn = _rms(x, pre_ref[...]).astype(BF16)
    acc = None
    for idx, (c0, cw) in enumerate(FF_CHUNKS):
        gate = _dot(xn, wgu_ref[:, c0:c0 + cw])
        up = _dot(xn, wgu_ref[:, D_FF + c0:D_FF + c0 + cw])
        act = (gate * jax.nn.sigmoid(gate) * up).astype(BF16)
        part_out = _dot(act, wd_ref[c0:c0 + cw, :])
        acc = part_out if acc is None else acc + part_out
        if idx < len(side):
            side[idx]()
    o_ref[...] = x + 0.5 * _rms(acc, post_ref[...])


def _ffn(x, pre_g, post_g, w_gu, w_down, decode=None):
    rows = x.shape[0]
    steps = rows // ROW_TILE
    row_spec = pl.BlockSpec((ROW_TILE, D_MODEL), lambda i: (i, 0))
    in_specs = [row_spec, _const_spec((1, D_MODEL)), _const_spec((1, D_MODEL)),
                _const_spec((D_MODEL, 2 * D_FF)), _const_spec((D_FF, D_MODEL))]
    out_specs, out_shape = row_spec, jax.ShapeDtypeStruct((rows, D_MODEL), F32)
    args, scratch, parts = [x, pre_g, post_g, w_gu, w_down], [], 0
    if decode is not None:
        seq0, n_seq, lam, subln_row, q, cache_kt, cache_v4, k_new, v_new = decode
        t, past = q.shape[1], cache_kt.shape[2]
        parts = steps // n_seq
        keys = past // parts
        assert steps == n_seq * parts and past == keys * parts and keys % V7X_LANES == 0
        seq = pl.BlockSpec((1, t, D_ATTN), lambda i: (seq0 + i // parts, 0, 0))
        in_specs += [pl.BlockSpec((4, HEAD_DIM), lambda i: (0, 0)),
                     pl.BlockSpec((1, 2 * HEAD_DIM), lambda i: (0, 0)),
                     seq,
                     pl.BlockSpec((1, D_ATTN, keys), lambda i: (seq0 + i // parts, 0, i % parts)),
                     pl.BlockSpec((1, keys * H_DIFF, 2 * HEAD_DIM),
                                  lambda i: (seq0 + i // parts, i % parts, 0)),
                     seq, seq]
        out_specs = [row_spec, pl.BlockSpec((1, t, D_ATTN), lambda i: (i // parts, 0, 0))]
        out_shape = [out_shape, jax.ShapeDtypeStruct((n_seq, t, D_ATTN), F32)]
        args += [lam, subln_row, q, cache_kt, cache_v4, k_new, v_new]
        scratch = [pltpu.VMEM((H_DIFF, 2 * t, 1), F32), pltpu.VMEM((H_DIFF, 2 * t, 1), F32),
                   pltpu.VMEM((H_DIFF, 2 * t, 2 * HEAD_DIM), F32)]
    return pl.pallas_call(
        functools.partial(_ffn_kernel, decode_parts=parts),
        grid=(steps,),
        in_specs=in_specs,
        out_specs=out_specs,
        out_shape=out_shape,
        scratch_shapes=scratch,
        compiler_params=pltpu.CompilerParams(
            dimension_semantics=("arbitrary",), vmem_limit_bytes=_mib(56 if parts else 48)),
        name="ffn_attn_decode" if parts else "ffn",
    )(*args)


def _rope_tables(tile_pos0, row_pos):
    d = np.arange(V7X_LANES) % HEAD_DIM
    inv = np.where(d < ROT_DIM, ROPE_THETA ** (-(d % (ROT_DIM // 2)) / (ROT_DIM // 2)), 0.0)
    ang_a = np.asarray(tile_pos0, np.float64)[:, None] * inv[None, :]
    ang_b = np.asarray(row_pos, np.float64)[:, None] * inv[None, :]
    half = ROT_DIM // 2
    sel_lo = np.where(d < half, -1.0, 0.0)[None, :]
    sel_hi = np.where((d >= half) & (d < ROT_DIM), 1.0, 0.0)[None, :]
    f = lambda a: jnp.asarray(a, F32)
    return (f(np.cos(ang_a))[:, None, :], f(np.sin(ang_a))[:, None, :],
            f(np.cos(ang_b)), f(np.sin(ang_b)), f(sel_lo), f(sel_hi))


def _mix_in_kernel(h_ref, g_ref, win_ref, cos_a_ref, sin_a_ref, cos_b_ref, sin_b_ref,
                   sel_lo_ref, sel_hi_ref,
                   upool_ref, kf_ref, vf_ref, gates_ref, q_ref, kb_ref, v_ref, *, transposed):
    u = _rms(h_ref[...], g_ref[...]).astype(BF16)
    upool_ref[...] = _dot(u, win_ref[:, 0:D_POOL])

    ca, sa = cos_a_ref[0], sin_a_ref[0]
    cb, sb = cos_b_ref[...], sin_b_ref[...]
    cos = ca * cb - sa * sb
    sin = sa * cb + ca * sb
    rep = D_ATTN // V7X_LANES
    cos_f = jnp.concatenate([cos] * rep, axis=1)
    sin_lo = jnp.concatenate([sin * sel_lo_ref[...]] * rep, axis=1)
    sin_hi = jnp.concatenate([sin * sel_hi_ref[...]] * rep, axis=1)
    half = ROT_DIM // 2

    def rope(x):
        return (x * cos_f + pltpu.roll(x, D_ATTN - half, 1) * sin_lo
                + pltpu.roll(x, half, 1) * sin_hi)

    q = rope(_dot(u, win_ref[:, D_POOL:D_POOL + D_ATTN])) * QK_SCALE_LOG2
    k = rope(_dot(u, win_ref[:, D_POOL + D_ATTN:D_POOL + 2 * D_ATTN]))
    v = _dot(u, win_ref[:, D_POOL + 2 * D_ATTN:D_POOL + 3 * D_ATTN])
    kf_ref[...] = k
    vf_ref[...] = v.reshape(v.shape[0], H_DIFF, 2 * HEAD_DIM)
    for hh in range(H_DIFF):
        kb_ref[hh] = k[:, hh * 2 * HEAD_DIM:(hh + 1) * 2 * HEAD_DIM].astype(BF16)
    if transposed:
        comp = lax.broadcasted_iota(jnp.int32, q.shape, 1) % (2 * HEAD_DIM)
        q_top = jnp.where(comp < HEAD_DIM, q, 0.0)
        q_bot = jnp.where(comp >= HEAD_DIM, q, 0.0)
        q_ref[0, :, 0:ROW_TILE] = q_top.T.astype(BF16)
        q_ref[0, :, ROW_TILE:2 * ROW_TILE] = q_bot.T.astype(BF16)
        v_ref[0] = v.T.astype(BF16)
    else:
        q_ref[...] = q
        v_ref[...] = v.astype(BF16)

    g0 = D_POOL + 3 * D_ATTN
    for c in range(2):
        z = _dot(u, win_ref[:, g0 + c * D_MODEL:g0 + (c + 1) * D_MODEL])
        gates_ref[:, c * D_MODEL:(c + 1) * D_MODEL] = jax.nn.sigmoid(z).astype(BF16)


def _mix_in(h, mix_pre_g, w_in, tile_pos0, row_pos, transposed):
    rows = h.shape[0]
    nt = rows // ROW_TILE
    tables = _rope_tables(tile_pos0, row_pos)
    row = lambda w: pl.BlockSpec((ROW_TILE, w), lambda i: (i, 0))
    if transposed:
        q_shape = jax.ShapeDtypeStruct((nt, D_ATTN, 2 * ROW_TILE), BF16)
        q_spec = pl.BlockSpec((1, D_ATTN, 2 * ROW_TILE), lambda i: (i, 0, 0))
        v_shape = jax.ShapeDtypeStruct((nt, D_ATTN, ROW_TILE), BF16)
        v_spec = pl.BlockSpec((1, D_ATTN, ROW_TILE), lambda i: (i, 0, 0))
    else:
        q_shape, q_spec = jax.ShapeDtypeStruct((rows, D_ATTN), F32), row(D_ATTN)
        v_shape, v_spec = jax.ShapeDtypeStruct((rows, D_ATTN), BF16), row(D_ATTN)
    return pl.pallas_call(
        functools.partial(_mix_in_kernel, transposed=transposed),
        grid=(nt,),
        in_specs=[row(D_MODEL), _const_spec((1, D_MODEL)), _const_spec(w_in.shape),
                  pl.BlockSpec((1, 1, V7X_LANES), lambda i: (i, 0, 0)),
                  pl.BlockSpec((1, 1, V7X_LANES), lambda i: (i, 0, 0)),
                  _const_spec((ROW_TILE, V7X_LANES)), _const_spec((ROW_TILE, V7X_LANES)),
                  _const_spec((1, V7X_LANES)), _const_spec((1, V7X_LANES))],
        out_specs=[row(D_POOL), row(D_ATTN),
                   pl.BlockSpec((ROW_TILE, H_DIFF, 2 * HEAD_DIM), lambda i: (i, 0, 0)),
                   row(2 * D_MODEL),
                   q_spec,
                   pl.BlockSpec((H_DIFF, ROW_TILE, 2 * HEAD_DIM), lambda i: (0, i, 0)),
                   v_spec],
        out_shape=[jax.ShapeDtypeStruct((rows, D_POOL), F32),
                   jax.ShapeDtypeStruct((rows, D_ATTN), F32),
                   jax.ShapeDtypeStruct((rows, H_DIFF, 2 * HEAD_DIM), F32),
                   jax.ShapeDtypeStruct((rows, 2 * D_MODEL), BF16),
                   q_shape,
                   jax.ShapeDtypeStruct((H_DIFF, rows, 2 * HEAD_DIM), BF16),
                   v_shape],
        compiler_params=pltpu.CompilerParams(
            dimension_semantics=("arbitrary",), vmem_limit_bytes=_mib(48)),
        name="mix_in",
    )(h, mix_pre_g, w_in, *tables)


def _lambda(lam_ref):
    lq1, lk1, lq2, lk2 = (lam_ref[i:i + 1, :] for i in range(4))
    return (jnp.exp(jnp.sum(lq1 * lk1, axis=1, keepdims=True))
            - jnp.exp(jnp.sum(lq2 * lk2, axis=1, keepdims=True)) + LAM_INIT)


def _attn_prompt_kernel(lam_ref, g_ref, q_ref, qn_ref, k_ref, v_ref, o_ref, acc_ref, s_ref):
    tb = ROW_TILE
    hd2 = 2 * HEAD_DIM
    i = pl.program_id(1)
    acc_ref[...] = jnp.zeros_like(acc_ref)
    heads = [slice(h * hd2, (h + 1) * hd2) for h in range(ATTN_HEADS_PER_STEP)]
    strips = [slice(c, c + ATTN_STRIP) for c in range(0, 2 * tb, ATTN_STRIP)]
    ones = jnp.ones((BF16_SUBLANES, tb), BF16)

    def logits(j, rows, cols, q_blk):
        kj = k_ref[rows.start // hd2, pl.ds(pl.multiple_of(j * tb, tb), tb), :]
        return _dot(kj, q_blk[0, rows, cols])

    def consume(j, m, l, diagonal, next_logits):
        m = [list(x) for x in m]
        l = [list(x) for x in l]
        for n, cols in enumerate(strips):
            q0 = cols.start % tb
            keys = q0 + ATTN_STRIP if diagonal else tb
            for h, rows in enumerate(heads):
                s_next = next_logits(rows, cols)

                def s_strip():
                    s = s_ref[h, n, 0:keys, :]
                    if diagonal:
                        k_chunk = lax.broadcasted_iota(jnp.int32, s.shape, 0) // CHUNK
                        q_chunk = (q0 + lax.broadcasted_iota(jnp.int32, s.shape, 1)) // CHUNK
                        s = jnp.where(k_chunk <= q_chunk, s, -jnp.inf)
                    return s
                m_new = jnp.maximum(m[h][n], jnp.max(s_strip(), axis=0, keepdims=True))
                alpha = jnp.exp2(m[h][n] - m_new)
                e = jnp.exp2((s_strip() - m_new).astype(BF16))
                pv = _dot(jnp.concatenate([v_ref[j, rows, 0:keys], ones[:, 0:keys]], axis=0), e)
                l[h][n] = alpha * l[h][n] + pv[hd2:hd2 + 1, :]
                m[h][n] = m_new
                acc_ref[h, n] = acc_ref[h, n] * alpha + pv[0:hd2, :]
                s_ref[h, n] = s_next
        return m, l

    @pl.when(i == 0)
    def _():
        for h, rows in enumerate(heads):
            for n, cols in enumerate(strips):
                s_ref[h, n] = logits(0, rows, cols, q_ref)

    m0 = [[jnp.full((1, ATTN_STRIP), -jnp.inf, F32) for _ in strips] for _ in heads]
    l0 = [[jnp.zeros((1, ATTN_STRIP), F32) for _ in strips] for _ in heads]
    def full_block(j, ml):
        return consume(j, ml[0], ml[1], False, lambda r, c: logits(j + 1, r, c, q_ref))

    m, l = lax.fori_loop(
        0, i // 2, lambda t, ml: full_block(2 * t + 1, full_block(2 * t, ml)), (m0, l0))
    m, l = lax.cond(i % 2 == 1, lambda ml: full_block(i - 1, ml), lambda ml: ml, (m, l))
    m, l = consume(i, m, l, True, lambda r, c: logits(0, r, c, qn_ref))

    lam = _lambda(lam_ref)
    for h, rows in enumerate(heads):
        o = jnp.concatenate([acc_ref[h, n] * (1.0 / l[h][n]) for n in range(len(strips))], axis=1)
        o = o[:, 0:tb] - lam * o[:, tb:2 * tb]
        ms = jnp.mean(o * o, axis=0, keepdims=True)
        o_ref[rows, :] = o * lax.rsqrt(ms + EPS) * g_ref[...] * (1.0 - LAM_INIT)


def _attn_prompt(lam, subln_col, q_t, k_b, v_t):
    nt = q_t.shape[0]
    rows = nt * ROW_TILE
    hd2 = 2 * HEAD_DIM
    gw = ATTN_HEADS_PER_STEP * hd2
    n_strips = 2 * ROW_TILE // ATTN_STRIP
    resident = dict(pipeline_mode=pl.Buffered(1))
    return pl.pallas_call(
        _attn_prompt_kernel,
        grid=(H_DIFF // ATTN_HEADS_PER_STEP, nt),
        in_specs=[pl.BlockSpec((4, HEAD_DIM), lambda h, i: (0, 0)),
                  pl.BlockSpec((hd2, 1), lambda h, i: (0, 0)),
                  pl.BlockSpec((1, gw, 2 * ROW_TILE), lambda h, i: (i, h, 0)),
                  pl.BlockSpec((1, gw, 2 * ROW_TILE), lambda h, i: (jnp.minimum(i + 1, nt - 1), h, 0)),
                  pl.BlockSpec((ATTN_HEADS_PER_STEP, rows, hd2), lambda h, i: (h, 0, 0), **resident),
                  pl.BlockSpec((nt, gw, ROW_TILE), lambda h, i: (0, h, 0), **resident)],
        out_specs=pl.BlockSpec((gw, ROW_TILE), lambda h, i: (h, i)),
        out_shape=jax.ShapeDtypeStruct((D_ATTN, rows), F32),
        scratch_shapes=[pltpu.VMEM((ATTN_HEADS_PER_STEP, n_strips, hd2, ATTN_STRIP), F32),
                        pltpu.VMEM((ATTN_HEADS_PER_STEP, n_strips, ROW_TILE, ATTN_STRIP), F32)],
        compiler_params=pltpu.CompilerParams(
            dimension_semantics=("arbitrary", "arbitrary"), vmem_limit_bytes=_mib(56)),
        name="attn_prompt",
    )(lam, subln_col, q_t, q_t, k_b, v_t)


def _merge_kernel(h_ref, attn_ref, gates_ref, halo_ref, body_ref, poolw_ref, pscale_ref,
                  wbp_ref, wba_ref, wout_ref, postg_ref, o_ref, *, hist0, attn_transposed):
    nb, tl, _ = body_ref.shape
    hist = hist0 + pl.program_id(0) * tl
    hi = hist + 1 + lax.broadcasted_iota(jnp.int32, (tl, 1), 0)
    halo_row = lax.broadcasted_iota(jnp.int32, (POOL_HALO, 1), 0)
    halo_ok = halo_row >= POOL_HALO - jnp.minimum(hist, POOL_STATE)
    pooled = []
    for b in range(nb):
        halo = jnp.where(halo_ok, halo_ref[b], 0.0)
        ext = jnp.concatenate([halo, body_ref[b]], axis=0)
        groups = []
        for g, win in enumerate(POOL_WINDOWS):
            x = ext[:, g * POOL_GROUP:(g + 1) * POOL_GROUP]
            s, shift = x, 1
            while shift < win:
                s = s + pltpu.roll(s, shift, 0)
                shift *= 2
            inv_cnt = 1.0 / jnp.minimum(hi, win).astype(F32)
            mixed = s[POOL_HALO:] * inv_cnt - x[POOL_HALO:]
            groups.append(_dot(mixed.astype(BF16), poolw_ref[g]))
        pooled.append(jnp.concatenate(groups, axis=1) * pscale_ref[...])
    pool_out = pooled[0] if nb == 1 else jnp.concatenate(pooled, axis=0)

    attn = attn_ref[...].T if attn_transposed else attn_ref[...]
    branch_pool = _dot(pool_out.astype(BF16), wbp_ref[...])
    branch_attn = _dot(attn.astype(BF16), wba_ref[...])
    merged = (gates_ref[:, 0:D_MODEL].astype(F32) * branch_pool
              + gates_ref[:, D_MODEL:2 * D_MODEL].astype(F32) * branch_attn)
    o_ref[...] = h_ref[...] + _rms(_dot(merged.astype(BF16), wout_ref[...]), postg_ref[...])


def _merge(h, attn, gates, halo, body, halo_spec, body_spec, hist0, attn_transposed,
           pool_w, pool_scale, w_bp, w_ba, w_out, post_g):
    rows = h.shape[0]
    row = lambda w: pl.BlockSpec((ROW_TILE, w), lambda i: (i, 0))
    attn_spec = (pl.BlockSpec((D_ATTN, ROW_TILE), lambda i: (0, i)) if attn_transposed
                 else row(D_ATTN))
    return pl.pallas_call(
        functools.partial(_merge_kernel, hist0=hist0, attn_transposed=attn_transposed),
        grid=(rows // ROW_TILE,),
        in_specs=[row(D_MODEL), attn_spec, row(2 * D_MODEL), halo_spec, body_spec,
                  _const_spec(pool_w.shape), _const_spec(pool_scale.shape),
                  _const_spec(w_bp.shape), _const_spec(w_ba.shape), _const_spec(w_out.shape),
                  _const_spec(post_g.shape)],
        out_specs=row(D_MODEL),
        out_shape=jax.ShapeDtypeStruct((rows, D_MODEL), F32),
        compiler_params=pltpu.CompilerParams(
            dimension_semantics=("arbitrary",), vmem_limit_bytes=_mib(48)),
        name="merge",
    )(h, attn, gates, halo, body, pool_w, pool_scale, w_bp, w_ba, w_out, post_g)


def kernel(x_prompt, x_sample, cache_k, cache_v, state_pool, ffn1_pre_g, ffn1_post_g, ffn1_w_gu, ffn1_w_down, mix_pre_g, mix_post_g, w_in, pool_w, pool_scale, lambda_q1, lambda_k1, lambda_q2, lambda_k2, subln_g, w_branch_pool, w_branch_attn, w_out, ffn2_pre_g, ffn2_post_g, ffn2_w_gu, ffn2_w_down):
    bp, seq, _ = x_prompt.shape
    db, dseq, _ = x_sample.shape
    past = cache_k.shape[2]
    assert bp == 1 and seq % ROW_TILE == 0 and db * dseq == ROW_TILE and dseq >= POOL_STATE
    assert (past + dseq - 1) // CHUNK <= past // CHUNK and past % CHUNK == 0

    l = 0
    bf = lambda w: w[l].astype(BF16)
    g1pre, g1post, gmpre, gmpost = ffn1_pre_g[l:l + 1], ffn1_post_g[l:l + 1], mix_pre_g[l:l + 1], mix_post_g[l:l + 1]
    g2pre, g2post = ffn2_pre_g[l:l + 1], ffn2_post_g[l:l + 1]
    w1gu, w1d, w2gu, w2d = bf(ffn1_w_gu), bf(ffn1_w_down), bf(ffn2_w_gu), bf(ffn2_w_down)
    win, wpool, wbp, wba, wo = bf(w_in), bf(pool_w), bf(w_branch_pool), bf(w_branch_attn), bf(w_out)
    pscale = pool_scale[l:l + 1]
    lam = jnp.concatenate([lambda_q1[l:l + 1], lambda_k1[l:l + 1],
                           lambda_q2[l:l + 1], lambda_k2[l:l + 1]], axis=0)
    subln_row = subln_g[l:l + 1]
    subln_col = subln_g[l][:, None]
    merge_w = (wpool, pscale, wbp, wba, wo, gmpost)

    xs = x_sample.reshape(db * dseq, D_MODEL)
    hs = _ffn(xs, g1pre, g1post, w1gu, w1d)
    upool_s, kf_s, vf_s, gates_s, q_s, _, _ = _mix_in(
        hs, gmpre, win, np.array([past]), np.arange(ROW_TILE) % dseq, transposed=False)
    cache_kt = jnp.transpose(cache_k[l], (0, 2, 3, 4, 1)).reshape(db, D_ATTN, past)
    cache_v4 = cache_v[l].reshape(db, past * H_DIFF, 2 * HEAD_DIM)
    decode_attn = lambda seq0: (seq0, db // 2, lam, subln_row, q_s.reshape(db, dseq, D_ATTN),
                                cache_kt, cache_v4, kf_s.reshape(db, dseq, D_ATTN),
                                vf_s.reshape(db, dseq, D_ATTN))

    xp = x_prompt.reshape(seq, D_MODEL)
    nt = seq // ROW_TILE
    hp, attn_s0 = _ffn(xp, g1pre, g1post, w1gu, w1d, decode_attn(0))
    upool, kf, vf, gates, q_t, k_b, v_t = _mix_in(
        hp, gmpre, win, np.arange(nt) * ROW_TILE, np.arange(ROW_TILE), transposed=True)
    attn_t = _attn_prompt(lam, subln_col, q_t, k_b, v_t)
    per_tile = ROW_TILE // POOL_HALO
    halo_spec = pl.BlockSpec((1, POOL_HALO, D_POOL),
                             lambda i: (jnp.maximum(i * per_tile - 1, 0), 0, 0))
    body_spec = pl.BlockSpec((1, ROW_TILE, D_POOL), lambda i: (i, 0, 0))
    hp = _merge(hp, attn_t, gates, upool.reshape(seq // POOL_HALO, POOL_HALO, D_POOL),
                upool.reshape(nt, ROW_TILE, D_POOL), halo_spec, body_spec, 0, True, *merge_w)
    y_prompt, attn_s1 = _ffn(hp, g2pre, g2post, w2gu, w2d, decode_attn(db // 2))
    y_prompt = y_prompt.reshape(bp, seq, D_MODEL)
    new_k_p = kf.reshape(1, bp, seq, H_DIFF, 2, HEAD_DIM)
    new_v_p = vf.reshape(1, bp, seq, H_DIFF, 2 * HEAD_DIM)
    new_pool_p = upool[seq - POOL_STATE:].reshape(1, bp, POOL_STATE, D_POOL)

    attn_s = jnp.concatenate([attn_s0, attn_s1], axis=0)
    hist = state_pool[l]
    halo_s = jnp.concatenate(
        [jnp.zeros((db, POOL_HALO - POOL_STATE, D_POOL), F32), hist], axis=1)
    body_s = upool_s.reshape(db, dseq, D_POOL)
    hs = _merge(hs, attn_s.reshape(db * dseq, D_ATTN), gates_s, halo_s, body_s,
                pl.BlockSpec((db, POOL_HALO, D_POOL), lambda i: (0, 0, 0)),
                pl.BlockSpec((db, dseq, D_POOL), lambda i: (0, 0, 0)),
                POOL_STATE, False, *merge_w)
    y_sample = _ffn(hs, g2pre, g2post, w2gu, w2d).reshape(db, dseq, D_MODEL)
    new_k_s = kf_s.reshape(1, db, dseq, H_DIFF, 2, HEAD_DIM)
    new_v_s = vf_s.reshape(1, db, dseq, H_DIFF, 2 * HEAD_DIM)
    new_pool_s = body_s[:, dseq - POOL_STATE:].reshape(1, db, POOL_STATE, D_POOL)

    return (y_prompt, y_sample, new_k_p, new_v_p, new_pool_p, new_k_s, new_v_s, new_pool_s)
```

```python
import functools
import math

import numpy as np
import jax
import jax.numpy as jnp
from jax import lax
from jax.experimental import pallas as pl
from jax.experimental.pallas import tpu as pltpu

D_MODEL = 1024
D_FF = 2816
D_POOL = 512
POOL_WINDOWS = (2, 4, 8, 16)
POOL_GROUP = 128
POOL_STATE = 15
H_DIFF = 4
HEAD_DIM = 64
D_ATTN = 512
ROT_DIM = 16
ROPE_THETA = 500000.0
CHUNK = 64
EPS = 1e-6
LAM_INIT = 0.8 - 0.6 * math.exp(-0.3 * 0)

V7X_LANES = 128
BF16_SUBLANES = 16
POOL_HALO = 16
ROW_TILE = 512
FF_CHUNKS = ((0, 1024), (1024, 1024), (2048, 768))
ATTN_STRIP = 256
ATTN_HEADS_PER_STEP = 4
QK_SCALE_LOG2 = HEAD_DIM ** -0.5 * math.log2(math.e)

F32 = jnp.float32
BF16 = jnp.bfloat16


def _mib(n):
    return int(n * 1024 * 1024)


def _rms(x, g):
    return x * lax.rsqrt(jnp.mean(x * x, axis=-1, keepdims=True) + EPS) * g


def _dot(a, b):
    return jnp.dot(a, b, preferred_element_type=F32)


def _const_spec(shape):
    nd = len(shape)
    return pl.BlockSpec(shape, lambda *_: (0,) * nd, pipeline_mode=pl.Buffered(1))


def _decode_attn_stages(first, last, lam_ref, g_ref, q_ref, ck_ref, cv_ref, kn_ref, vn_ref,
                        o_ref, m_ref, l_ref, acc_ref):
    hd2 = 2 * HEAD_DIM
    keys = ck_ref.shape[2]
    nt_dims = (((1,), (1,)), ((), ()))
    heads = [slice(h * hd2, (h + 1) * hd2) for h in range(H_DIFF)]
    state = {}

    def scores():
        for h, sl in enumerate(heads):
            q = q_ref[0, :, sl]
            comp = lax.broadcasted_iota(jnp.int32, q.shape, 1)
            lhs = jnp.concatenate([jnp.where(comp < HEAD_DIM, q, 0.0),
                                   jnp.where(comp >= HEAD_DIM, q, 0.0)], axis=0).astype(BF16)
            s_c = _dot(lhs, ck_ref[0, sl, :].astype(BF16))
            s_n = lax.dot_general(lhs, kn_ref[0, :, sl].astype(BF16), nt_dims,
                                  preferred_element_type=F32)
            s_n = jnp.where(last, s_n, -jnp.inf)
            m_old = jnp.where(first, -jnp.inf, m_ref[h])
            m_new = jnp.maximum(m_old, jnp.maximum(jnp.max(s_c, axis=1, keepdims=True),
                                                   jnp.max(s_n, axis=1, keepdims=True)))
            alpha = jnp.exp2(m_old - m_new)
            e_c = jnp.exp2(s_c - m_new)
            e_n = jnp.exp2(s_n - m_new)
            l_new = (alpha * jnp.where(first, 0.0, l_ref[h])
                     + jnp.sum(e_c, axis=1, keepdims=True) + jnp.sum(e_n, axis=1, keepdims=True))
            m_ref[h] = m_new
            l_ref[h] = l_new
            state[h] = (alpha, e_c.astype(BF16), e_n.astype(BF16), l_new)

    def outputs():
        lam = _lambda(lam_ref)
        for h, sl in enumerate(heads):
            alpha, e_c, e_n, l_new = state[h]
            v_c = cv_ref[0, pl.ds(h, keys, stride=H_DIFF), :].astype(BF16)
            acc = (alpha * jnp.where(first, 0.0, acc_ref[h])
                   + _dot(e_c, v_c) + _dot(e_n, vn_ref[0, :, sl].astype(BF16)))
            acc_ref[h] = acc
            o = acc * (1.0 / l_new)
            t = o.shape[0] // 2
            o = o[0:t] - lam * o[t:2 * t]
            o_ref[0, :, sl] = _rms(o, g_ref[...]) * (1.0 - LAM_INIT)

    return scores, outputs


def _ffn_kernel(*refs, decode_parts):
    if decode_parts:
        (x_ref, pre_ref, post_ref, wgu_ref, wd_ref, *attn_in, o_ref, attn_o_ref,
         m_ref, l_ref, acc_ref) = refs
        step = pl.program_id(0)

        @pl.when(step == 0)
        def _():
            m_ref[...] = jnp.zeros_like(m_ref)
            l_ref[...] = jnp.zeros_like(l_ref)
            acc_ref[...] = jnp.zeros_like(acc_ref)

        part = step % decode_parts
        side = _decode_attn_stages(part == 0, part == decode_parts - 1, *attn_in,
                                   attn_o_ref, m_ref, l_ref, acc_ref)
    else:
        x_ref, pre_ref, post_ref, wgu_ref, wd_ref, o_ref = refs
        side = ()
    x = x_ref[...]
    xn = _rms(x, pre_ref[...]).astype(BF16)
    acc = None
    for idx, (c0, cw) in enumerate(FF_CHUNKS):
        gate = _dot(xn, wgu_ref[:, c0:c0 + cw])
        up = _dot(xn, wgu_ref[:, D_FF + c0:D_FF + c0 + cw])
        act = (gate * jax.nn.sigmoid(gate) * up).astype(BF16)
        part_out = _dot(act, wd_ref[c0:c0 + cw, :])
        acc = part_out if acc is None else acc + part_out
        if idx < len(side):
            side[idx]()
    o_ref[...] = x + 0.5 * _rms(acc, post_ref[...])


def _ffn(x, pre_g, post_g, w_gu, w_down, decode=None):
    rows = x.shape[0]
    steps = rows // ROW_TILE
    row_spec = pl.BlockSpec((ROW_TILE, D_MODEL), lambda i: (i, 0))
    in_specs = [row_spec, _const_spec((1, D_MODEL)), _const_spec((1, D_MODEL)),
                _const_spec((D_MODEL, 2 * D_FF)), _const_spec((D_FF, D_MODEL))]
    out_specs, out_shape = row_spec, jax.ShapeDtypeStruct((rows, D_MODEL), F32)
    args, scratch, parts = [x, pre_g, post_g, w_gu, w_down], [], 0
    if decode is not None:
        seq0, n_seq, lam, subln_row, q, cache_kt, cache_v4, k_new, v_new = decode
        t, past = q.shape[1], cache_kt.shape[2]
        parts = steps // n_seq
        keys = past // parts
        assert steps == n_seq * parts and past == keys * parts and keys % V7X_LANES == 0
        seq = pl.BlockSpec((1, t, D_ATTN), lambda i: (seq0 + i // parts, 0, 0))
        in_specs += [pl.BlockSpec((4, HEAD_DIM), lambda i: (0, 0)),
                     pl.BlockSpec((1, 2 * HEAD_DIM), lambda i: (0, 0)),
                     seq,
                     pl.BlockSpec((1, D_ATTN, keys), lambda i: (seq0 + i // parts, 0, i % parts)),
                     pl.BlockSpec((1, keys * H_DIFF, 2 * HEAD_DIM),
                                  lambda i: (seq0 + i // parts, i % parts, 0)),
                     seq, seq]
        out_specs = [row_spec, pl.BlockSpec((1, t, D_ATTN), lambda i: (i // parts, 0, 0))]
        out_shape = [out_shape, jax.ShapeDtypeStruct((n_seq, t, D_ATTN), F32)]
        args += [lam, subln_row, q, cache_kt, cache_v4, k_new, v_new]
        scratch = [pltpu.VMEM((H_DIFF, 2 * t, 1), F32), pltpu.VMEM((H_DIFF, 2 * t, 1), F32),
                   pltpu.VMEM((H_DIFF, 2 * t, 2 * HEAD_DIM), F32)]
    return pl.pallas_call(
        functools.partial(_ffn_kernel, decode_parts=parts),
        grid=(steps,),
        in_specs=in_specs,
        out_specs=out_specs,
        out_shape=out_shape,
        scratch_shapes=scratch,
        compiler_params=pltpu.CompilerParams(
            dimension_semantics=("arbitrary",), vmem_limit_bytes=_mib(56 if parts else 48)),
        name="ffn_attn_decode" if parts else "ffn",
    )(*args)


def _rope_tables(tile_pos0, row_pos):
    d = np.arange(V7X_LANES) % HEAD_DIM
    inv = np.where(d < ROT_DIM, ROPE_THETA ** (-(d % (ROT_DIM // 2)) / (ROT_DIM // 2)), 0.0)
    ang_a = np.asarray(tile_pos0, np.float64)[:, None] * inv[None, :]
    ang_b = np.asarray(row_pos, np.float64)[:, None] * inv[None, :]
    half = ROT_DIM // 2
    sel_lo = np.where(d < half, -1.0, 0.0)[None, :]
    sel_hi = np.where((d >= half) & (d < ROT_DIM), 1.0, 0.0)[None, :]
    f = lambda a: jnp.asarray(a, F32)
    return (f(np.cos(ang_a))[:, None, :], f(np.sin(ang_a))[:, None, :],
            f(np.cos(ang_b)), f(np.sin(ang_b)), f(sel_lo), f(sel_hi))


def _mix_in_kernel(h_ref, g_ref, win_ref, cos_a_ref, sin_a_ref, cos_b_ref, sin_b_ref,
                   sel_lo_ref, sel_hi_ref,
                   upool_ref, kf_ref, vf_ref, gates_ref, q_ref, kb_ref, v_ref, *, transposed):
    u = _rms(h_ref[...], g_ref[...]).astype(BF16)
    upool_ref[...] = _dot(u, win_ref[:, 0:D_POOL])

    ca, sa = cos_a_ref[0], sin_a_ref[0]
    cb, sb = cos_b_ref[...], sin_b_ref[...]
    cos = ca * cb - sa * sb
    sin = sa * cb + ca * sb
    rep = D_ATTN // V7X_LANES
    cos_f = jnp.concatenate([cos] * rep, axis=1)
    sin_lo = jnp.concatenate([sin * sel_lo_ref[...]] * rep, axis=1)
    sin_hi = jnp.concatenate([sin * sel_hi_ref[...]] * rep, axis=1)
    half = ROT_DIM // 2

    def rope(x):
        return (x * cos_f + pltpu.roll(x, D_ATTN - half, 1) * sin_lo
                + pltpu.roll(x, half, 1) * sin_hi)

    q = rope(_dot(u, win_ref[:, D_POOL:D_POOL + D_ATTN])) * QK_SCALE_LOG2
    k = rope(_dot(u, win_ref[:, D_POOL + D_ATTN:D_POOL + 2 * D_ATTN]))
    v = _dot(u, win_ref[:, D_POOL + 2 * D_ATTN:D_POOL + 3 * D_ATTN])
    kf_ref[...] = k
    vf_ref[...] = v.reshape(v.shape[0], H_DIFF, 2 * HEAD_DIM)
    for hh in range(H_DIFF):
        kb_ref[hh] = k[:, hh * 2 * HEAD_DIM:(hh + 1) * 2 * HEAD_DIM].astype(BF16)
    if transposed:
        comp = lax.broadcasted_iota(jnp.int32, q.shape, 1) % (2 * HEAD_DIM)
        q_top = jnp.where(comp < HEAD_DIM, q, 0.0)
        q_bot = jnp.where(comp >= HEAD_DIM, q, 0.0)
        q_ref[0, :, 0:ROW_TILE] = q_top.T.astype(BF16)
        q_ref[0, :, ROW_TILE:2 * ROW_TILE] = q_bot.T.astype(BF16)
        v_ref[0] = v.T.astype(BF16)
    else:
        q_ref[...] = q
        v_ref[...] = v.astype(BF16)

    g0 = D_POOL + 3 * D_ATTN
    for c in range(2):
        z = _dot(u, win_ref[:, g0 + c * D_MODEL:g0 + (c + 1) * D_MODEL])
        gates_ref[:, c * D_MODEL:(c + 1) * D_MODEL] = jax.nn.sigmoid(z).astype(BF16)


def _mix_in(h, mix_pre_g, w_in, tile_pos0, row_pos, transposed):
    rows = h.shape[0]
    nt = rows // ROW_TILE
    tables = _rope_tables(tile_pos0, row_pos)
    row = lambda w: pl.BlockSpec((ROW_TILE, w), lambda i: (i, 0))
    if transposed:
        q_shape = jax.ShapeDtypeStruct((nt, D_ATTN, 2 * ROW_TILE), BF16)
        q_spec = pl.BlockSpec((1, D_ATTN, 2 * ROW_TILE), lambda i: (i, 0, 0))
        v_shape = jax.ShapeDtypeStruct((nt, D_ATTN, ROW_TILE), BF16)
        v_spec = pl.BlockSpec((1, D_ATTN, ROW_TILE), lambda i: (i, 0, 0))
    else:
        q_shape, q_spec = jax.ShapeDtypeStruct((rows, D_ATTN), F32), row(D_ATTN)
        v_shape, v_spec = jax.ShapeDtypeStruct((rows, D_ATTN), BF16), row(D_ATTN)
    return pl.pallas_call(
        functools.partial(_mix_in_kernel, transposed=transposed),
        grid=(nt,),
        in_specs=[row(D_MODEL), _const_spec((1, D_MODEL)), _const_spec(w_in.shape),
                  pl.BlockSpec((1, 1, V7X_LANES), lambda i: (i, 0, 0)),
                  pl.BlockSpec((1, 1, V7X_LANES), lambda i: (i, 0, 0)),
                  _const_spec((ROW_TILE, V7X_LANES)), _const_spec((ROW_TILE, V7X_LANES)),
                  _const_spec((1, V7X_LANES)), _const_spec((1, V7X_LANES))],
        out_specs=[row(D_POOL), row(D_ATTN),
                   pl.BlockSpec((ROW_TILE, H_DIFF, 2 * HEAD_DIM), lambda i: (i, 0, 0)),
                   row(2 * D_MODEL),
                   q_spec,
                   pl.BlockSpec((H_DIFF, ROW_TILE, 2 * HEAD_DIM), lambda i: (0, i, 0)),
                   v_spec],
        out_shape=[jax.ShapeDtypeStruct((rows, D_POOL), F32),
                   jax.ShapeDtypeStruct((rows, D_ATTN), F32),
                   jax.ShapeDtypeStruct((rows, H_DIFF, 2 * HEAD_DIM), F32),
                   jax.ShapeDtypeStruct((rows, 2 * D_MODEL), BF16),
                   q_shape,
                   jax.ShapeDtypeStruct((H_DIFF, rows, 2 * HEAD_DIM), BF16),
                   v_shape],
        compiler_params=pltpu.CompilerParams(
            dimension_semantics=("arbitrary",), vmem_limit_bytes=_mib(48)),
        name="mix_in",
    )(h, mix_pre_g, w_in, *tables)


def _lambda(lam_ref):
    lq1, lk1, lq2, lk2 = (lam_ref[i:i + 1, :] for i in range(4))
    return (jnp.exp(jnp.sum(lq1 * lk1, axis=1, keepdims=True))
            - jnp.exp(jnp.sum(lq2 * lk2, axis=1, keepdims=True)) + LAM_INIT)


def _attn_prompt_kernel(lam_ref, g_ref, q_ref, qn_ref, k_ref, v_ref, o_ref, acc_ref, s_ref):
    tb = ROW_TILE
    hd2 = 2 * HEAD_DIM
    i = pl.program_id(1)
    acc_ref[...] = jnp.zeros_like(acc_ref)
    heads = [slice(h * hd2, (h + 1) * hd2) for h in range(ATTN_HEADS_PER_STEP)]
    strips = [slice(c, c + ATTN_STRIP) for c in range(0, 2 * tb, ATTN_STRIP)]

    def logits(j, rows, cols, q_blk):
        kj = k_ref[rows.start // hd2, pl.ds(pl.multiple_of(j * tb, tb), tb), :]
        return _dot(kj, q_blk[0, rows, cols])

    def consume(j, m, l, diagonal, next_logits):
        m = [list(x) for x in m]
        l = [list(x) for x in l]
        for n, cols in enumerate(strips):
            q0 = cols.start % tb
            keys = q0 + ATTN_STRIP if diagonal else tb
            for h, rows in enumerate(heads):
                s_next = next_logits(rows, cols)

                def s_strip():
                    s = s_ref[h, n, 0:keys, :]
                    if diagonal:
                        k_chunk = lax.broadcasted_iota(jnp.int32, s.shape, 0) // CHUNK
                        q_chunk = (q0 + lax.broadcasted_iota(jnp.int32, s.shape, 1)) // CHUNK
                        s = jnp.where(k_chunk <= q_chunk, s, -jnp.inf)
                    return s
                m_new = jnp.maximum(m[h][n], jnp.max(s_strip(), axis=0, keepdims=True))
                alpha = jnp.exp2(m[h][n] - m_new)
                e = jnp.exp2(s_strip() - m_new)
                pv = _dot(v_ref[j, rows, 0:keys], e.astype(BF16))
                l[h][n] = alpha * l[h][n] + jnp.sum(e, axis=0, keepdims=True)
                m[h][n] = m_new
                acc_ref[h, n] = acc_ref[h, n] * alpha + pv
                s_ref[h, n] = s_next
        return m, l

    @pl.when(i == 0)
    def _():
        for h, rows in enumerate(heads):
            for n, cols in enumerate(strips):
                s_ref[h, n] = logits(0, rows, cols, q_ref)

    m0 = [[jnp.full((1, ATTN_STRIP), -jnp.inf, F32) for _ in strips] for _ in heads]
    l0 = [[jnp.zeros((1, ATTN_STRIP), F32) for _ in strips] for _ in heads]
    def full_block(j, ml):
        return consume(j, ml[0], ml[1], False, lambda r, c: logits(j + 1, r, c, q_ref))

    m, l = lax.fori_loop(
        0, i // 2, lambda t, ml: full_block(2 * t + 1, full_block(2 * t, ml)), (m0, l0))
    m, l = lax.cond(i % 2 == 1, lambda ml: full_block(i - 1, ml), lambda ml: ml, (m, l))
    m, l = consume(i, m, l, True, lambda r, c: logits(0, r, c, qn_ref))

    lam = _lambda(lam_ref)
    for h, rows in enumerate(heads):
        o = jnp.concatenate([acc_ref[h, n] * (1.0 / l[h][n]) for n in range(len(strips))], axis=1)
        o = o[:, 0:tb] - lam * o[:, tb:2 * tb]
        ms = jnp.mean(o * o, axis=0, keepdims=True)
        o_ref[rows, :] = o * lax.rsqrt(ms + EPS) * g_ref[...] * (1.0 - LAM_INIT)


def _attn_prompt(lam, subln_col, q_t, k_b, v_t):
    nt = q_t.shape[0]
    rows = nt * ROW_TILE
    hd2 = 2 * HEAD_DIM
    gw = ATTN_HEADS_PER_STEP * hd2
    n_strips = 2 * ROW_TILE // ATTN_STRIP
    resident = dict(pipeline_mode=pl.Buffered(1))
    return pl.pallas_call(
        _attn_prompt_kernel,
        grid=(H_DIFF // ATTN_HEADS_PER_STEP, nt),
        in_specs=[pl.BlockSpec((4, HEAD_DIM), lambda h, i: (0, 0)),
                  pl.BlockSpec((hd2, 1), lambda h, i: (0, 0)),
                  pl.BlockSpec((1, gw, 2 * ROW_TILE), lambda h, i: (i, h, 0)),
                  pl.BlockSpec((1, gw, 2 * ROW_TILE), lambda h, i: (jnp.minimum(i + 1, nt - 1), h, 0)),
                  pl.BlockSpec((ATTN_HEADS_PER_STEP, rows, hd2), lambda h, i: (h, 0, 0), **resident),
                  pl.BlockSpec((nt, gw, ROW_TILE), lambda h, i: (0, h, 0), **resident)],
        out_specs=pl.BlockSpec((gw, ROW_TILE), lambda h, i: (h, i)),
        out_shape=jax.ShapeDtypeStruct((D_ATTN, rows), F32),
        scratch_shapes=[pltpu.VMEM((ATTN_HEADS_PER_STEP, n_strips, hd2, ATTN_STRIP), F32),
                        pltpu.VMEM((ATTN_HEADS_PER_STEP, n_strips, ROW_TILE, ATTN_STRIP), F32)],
        compiler_params=pltpu.CompilerParams(
            dimension_semantics=("arbitrary", "arbitrary"), vmem_limit_bytes=_mib(56)),
        name="attn_prompt",
    )(lam, subln_col, q_t, q_t, k_b, v_t)


def _merge_kernel(h_ref, attn_ref, gates_ref, halo_ref, body_ref, poolw_ref, pscale_ref,
                  wbp_ref, wba_ref, wout_ref, postg_ref, o_ref, *, hist0, attn_transposed):
    nb, tl, _ = body_ref.shape
    hist = hist0 + pl.program_id(0) * tl
    hi = hist + 1 + lax.broadcasted_iota(jnp.int32, (tl, 1), 0)
    halo_row = lax.broadcasted_iota(jnp.int32, (POOL_HALO, 1), 0)
    halo_ok = halo_row >= POOL_HALO - jnp.minimum(hist, POOL_STATE)
    pooled = []
    for b in range(nb):
        halo = jnp.where(halo_ok, halo_ref[b], 0.0)
        ext = jnp.concatenate([halo, body_ref[b]], axis=0)
        groups = []
        for g, win in enumerate(POOL_WINDOWS):
            x = ext[:, g * POOL_GROUP:(g + 1) * POOL_GROUP]
            s, shift = x, 1
            while shift < win:
                s = s + pltpu.roll(s, shift, 0)
                shift *= 2
            inv_cnt = 1.0 / jnp.minimum(hi, win).astype(F32)
            mixed = s[POOL_HALO:] * inv_cnt - x[POOL_HALO:]
            groups.append(_dot(mixed.astype(BF16), poolw_ref[g]))
        pooled.append(jnp.concatenate(groups, axis=1) * pscale_ref[...])
    pool_out = pooled[0] if nb == 1 else jnp.concatenate(pooled, axis=0)

    attn = attn_ref[...].T if attn_transposed else attn_ref[...]
    branch_pool = _dot(pool_out.astype(BF16), wbp_ref[...])
    branch_attn = _dot(attn.astype(BF16), wba_ref[...])
    merged = (gates_ref[:, 0:D_MODEL].astype(F32) * branch_pool
              + gates_ref[:, D_MODEL:2 * D_MODEL].astype(F32) * branch_attn)
    o_ref[...] = h_ref[...] + _rms(_dot(merged.astype(BF16), wout_ref[...]), postg_ref[...])


def _merge(h, attn, gates, halo, body, halo_spec, body_spec, hist0, attn_transposed,
           pool_w, pool_scale, w_bp, w_ba, w_out, post_g):
    rows = h.shape[0]
    row = lambda w: pl.BlockSpec((ROW_TILE, w), lambda i: (i, 0))
    attn_spec = (pl.BlockSpec((D_ATTN, ROW_TILE), lambda i: (0, i)) if attn_transposed
                 else row(D_ATTN))
    return pl.pallas_call(
        functools.partial(_merge_kernel, hist0=hist0, attn_transposed=attn_transposed),
        grid=(rows // ROW_TILE,),
        in_specs=[row(D_MODEL), attn_spec, row(2 * D_MODEL), halo_spec, body_spec,
                  _const_spec(pool_w.shape), _const_spec(pool_scale.shape),
                  _const_spec(w_bp.shape), _const_spec(w_ba.shape), _const_spec(w_out.shape),
                  _const_spec(post_g.shape)],
        out_specs=row(D_MODEL),
        out_shape=jax.ShapeDtypeStruct((rows, D_MODEL), F32),
        compiler_params=pltpu.CompilerParams(
            dimension_semantics=("arbitrary",), vmem_limit_bytes=_mib(48)),
        name="merge",
    )(h, attn, gates, halo, body, pool_w, pool_scale, w_bp, w_ba, w_out, post_g)


def kernel(x_prompt, x_sample, cache_k, cache_v, state_pool, ffn1_pre_g, ffn1_post_g, ffn1_w_gu, ffn1_w_down, mix_pre_g, mix_post_g, w_in, pool_w, pool_scale, lambda_q1, lambda_k1, lambda_q2, lambda_k2, subln_g, w_branch_pool, w_branch_attn, w_out, ffn2_pre_g, ffn2_post_g, ffn2_w_gu, ffn2_w_down):
    bp, seq, _ = x_prompt.shape
    db, dseq, _ = x_sample.shape
    past = cache_k.shape[2]
    assert bp == 1 and seq % ROW_TILE == 0 and db * dseq == ROW_TILE and dseq >= POOL_STATE
    assert (past + dseq - 1) // CHUNK <= past // CHUNK and past % CHUNK == 0

    l = 0
    bf = lambda w: w[l].astype(BF16)
    g1pre, g1post, gmpre, gmpost = ffn1_pre_g[l:l + 1], ffn1_post_g[l:l + 1], mix_pre_g[l:l + 1], mix_post_g[l:l + 1]
    g2pre, g2post = ffn2_pre_g[l:l + 1], ffn2_post_g[l:l + 1]
    w1gu, w1d, w2gu, w2d = bf(ffn1_w_gu), bf(ffn1_w_down), bf(ffn2_w_gu), bf(ffn2_w_down)
    win, wpool, wbp, wba, wo = bf(w_in), bf(pool_w), bf(w_branch_pool), bf(w_branch_attn), bf(w_out)
    pscale = pool_scale[l:l + 1]
    lam = jnp.concatenate([lambda_q1[l:l + 1], lambda_k1[l:l + 1],
                           lambda_q2[l:l + 1], lambda_k2[l:l + 1]], axis=0)
    subln_row = subln_g[l:l + 1]
    subln_col = subln_g[l][:, None]
    merge_w = (wpool, pscale, wbp, wba, wo, gmpost)

    xs = x_sample.reshape(db * dseq, D_MODEL)
    hs = _ffn(xs, g1pre, g1post, w1gu, w1d)
    upool_s, kf_s, vf_s, gates_s, q_s, _, _ = _mix_in(
        hs, gmpre, win, np.array([past]), np.arange(ROW_TILE) % dseq, transposed=False)
    cache_kt = jnp.transpose(cache_k[l], (0, 2, 3, 4, 1)).reshape(db, D_ATTN, past)
    cache_v4 = cache_v[l].reshape(db, past * H_DIFF, 2 * HEAD_DIM)
    decode_attn = lambda seq0: (seq0, db // 2, lam, subln_row, q_s.reshape(db, dseq, D_ATTN),
                                cache_kt, cache_v4, kf_s.reshape(db, dseq, D_ATTN),
                                vf_s.reshape(db, dseq, D_ATTN))

    xp = x_prompt.reshape(seq, D_MODEL)
    nt = seq // ROW_TILE
    hp, attn_s0 = _ffn(xp, g1pre, g1post, w1gu, w1d, decode_attn(0))
    upool, kf, vf, gates, q_t, k_b, v_t = _mix_in(
        hp, gmpre, win, np.arange(nt) * ROW_TILE, np.arange(ROW_TILE), transposed=True)
    attn_t = _attn_prompt(lam, subln_col, q_t, k_b, v_t)
    per_tile = ROW_TILE // POOL_HALO
    halo_spec = pl.BlockSpec((1, POOL_HALO, D_POOL),
                             lambda i: (jnp.maximum(i * per_tile - 1, 0), 0, 0))
    body_spec = pl.BlockSpec((1, ROW_TILE, D_POOL), lambda i: (i, 0, 0))
    hp = _merge(hp, attn_t, gates, upool.reshape(seq // POOL_HALO, POOL_HALO, D_POOL),
                upool.reshape(nt, ROW_TILE, D_POOL), halo_spec, body_spec, 0, True, *merge_w)
    y_prompt, attn_s1 = _ffn(hp, g2pre, g2post, w2gu, w2d, decode_attn(db // 2))
    y_prompt = y_prompt.reshape(bp, seq, D_MODEL)
    new_k_p = kf.reshape(1, bp, seq, H_DIFF, 2, HEAD_DIM)
    new_v_p = vf.reshape(1, bp, seq, H_DIFF, 2 * HEAD_DIM)
    new_pool_p = upool[seq - POOL_STATE:].reshape(1, bp, POOL_STATE, D_POOL)

    attn_s = jnp.concatenate([attn_s0, attn_s1], axis=0)
    hist = state_pool[l]
    halo_s = jnp.concatenate(
        [jnp.zeros((db, POOL_HALO - POOL_STATE, D_POOL), F32), hist], axis=1)
    body_s = upool_s.reshape(db, dseq, D_POOL)
    hs = _merge(hs, attn_s.reshape(db * dseq, D_ATTN), gates_s, halo_s, body_s,
                pl.BlockSpec((db, POOL_HALO, D_POOL), lambda i: (0, 0, 0)),
                pl.BlockSpec((db, dseq, D_POOL), lambda i: (0, 0, 0)),
                POOL_STATE, False, *merge_w)
    y_sample = _ffn(hs, g2pre, g2post, w2gu, w2d).reshape(db, dseq, D_MODEL)
    new_k_s = kf_s.reshape(1, db, dseq, H_DIFF, 2, HEAD_DIM)
    new_v_s = vf_s.reshape(1, db, dseq, H_DIFF, 2 * HEAD_DIM)
    new_pool_s = body_s[:, dseq - POOL_STATE:].reshape(1, db, POOL_STATE, D_POOL)

    return (y_prompt, y_sample, new_k_p, new_v_p, new_pool_p, new_k_s, new_v_s, new_pool_s)
```

```python
import functools
import math

import numpy as np
import jax
import jax.numpy as jnp
from jax import lax
from jax.experimental import pallas as pl
from jax.experimental.pallas import tpu as pltpu

D_MODEL = 1024
D_FF = 2816
D_POOL = 512
POOL_WINDOWS = (2, 4, 8, 16)
POOL_GROUP = 128
POOL_STATE = 15
H_DIFF = 4
HEAD_DIM = 64
D_ATTN = 512
ROT_DIM = 16
ROPE_THETA = 500000.0
CHUNK = 64
EPS = 1e-6
LAM_INIT = 0.8 - 0.6 * math.exp(-0.3 * 0)

V7X_LANES = 128
BF16_SUBLANES = 16
POOL_HALO = 16
ROW_TILE = 512
FF_CHUNKS = ((0, 1024), (1024, 1024), (2048, 768))
ATTN_STRIP = 256
ATTN_HEADS_PER_STEP = 4
QK_SCALE_LOG2 = HEAD_DIM ** -0.5 * math.log2(math.e)

F32 = jnp.float32
BF16 = jnp.bfloat16


def _mib(n):
    return int(n * 1024 * 1024)


def _rms(x, g):
    return x * lax.rsqrt(jnp.mean(x * x, axis=-1, keepdims=True) + EPS) * g


def _dot(a, b):
    return jnp.dot(a, b, preferred_element_type=F32)


def _const_spec(shape):
    nd = len(shape)
    return pl.BlockSpec(shape, lambda *_: (0,) * nd, pipeline_mode=pl.Buffered(1))


def _decode_attn_stages(first, last, lam_ref, g_ref, q_ref, ck_ref, cv_ref, kn_ref, vn_ref,
                        o_ref, m_ref, l_ref, acc_ref):
    hd2 = 2 * HEAD_DIM
    keys = ck_ref.shape[2]
    nt_dims = (((1,), (1,)), ((), ()))
    heads = [slice(h * hd2, (h + 1) * hd2) for h in range(H_DIFF)]
    state = {}

    def scores():
        for h, sl in enumerate(heads):
            q = q_ref[0, :, sl]
            comp = lax.broadcasted_iota(jnp.int32, q.shape, 1)
            lhs = jnp.concatenate([jnp.where(comp < HEAD_DIM, q, 0.0),
                                   jnp.where(comp >= HEAD_DIM, q, 0.0)], axis=0).astype(BF16)
            s_c = _dot(lhs, ck_ref[0, sl, :].astype(BF16))
            s_n = lax.dot_general(lhs, kn_ref[0, :, sl].astype(BF16), nt_dims,
                                  preferred_element_type=F32)
            s_n = jnp.where(last, s_n, -jnp.inf)
            m_old = jnp.where(first, -jnp.inf, m_ref[h])
            m_new = jnp.maximum(m_old, jnp.maximum(jnp.max(s_c, axis=1, keepdims=True),
                                                   jnp.max(s_n, axis=1, keepdims=True)))
            alpha = jnp.exp2(m_old - m_new)
            e_c = jnp.exp2(s_c - m_new)
            e_n = jnp.exp2(s_n - m_new)
            l_new = (alpha * jnp.where(first, 0.0, l_ref[h])
                     + jnp.sum(e_c, axis=1, keepdims=True) + jnp.sum(e_n, axis=1, keepdims=True))
            m_ref[h] = m_new
            l_ref[h] = l_new
            state[h] = (alpha, e_c.astype(BF16), e_n.astype(BF16), l_new)

    def outputs():
        lam = _lambda(lam_ref)
        for h, sl in enumerate(heads):
            alpha, e_c, e_n, l_new = state[h]
            v_c = cv_ref[0, pl.ds(h, keys, stride=H_DIFF), :].astype(BF16)
            acc = (alpha * jnp.where(first, 0.0, acc_ref[h])
                   + _dot(e_c, v_c) + _dot(e_n, vn_ref[0, :, sl].astype(BF16)))
            acc_ref[h] = acc
            o = acc * (1.0 / l_new)
            t = o.shape[0] // 2
            o = o[0:t] - lam * o[t:2 * t]
            o_ref[0, :, sl] = _rms(o, g_ref[...]) * (1.0 - LAM_INIT)

    return scores, outputs


def _ffn_kernel(*refs, decode_parts, n_casts):
    x_ref, pre_ref, post_ref, wgu_ref, wd_ref = refs[:5]
    n_attn_in = 7 if decode_parts else 0
    attn_in = refs[5:5 + n_attn_in]
    cast_in = refs[5 + n_attn_in:5 + n_attn_in + n_casts]
    n_out = 1 + (1 if decode_parts else 0)
    outs = refs[5 + n_attn_in + n_casts:]
    o_ref = outs[0]
    cast_out = outs[n_out:n_out + n_casts]
    if decode_parts:
        attn_o_ref = outs[1]
        m_ref, l_ref, acc_ref = outs[n_out + n_casts:]
        step = pl.program_id(0)

        @pl.when(step == 0)
        def _():
            m_ref[...] = jnp.zeros_like(m_ref)
            l_ref[...] = jnp.zeros_like(l_ref)
            acc_ref[...] = jnp.zeros_like(acc_ref)

        part = step % decode_parts
        side = _decode_attn_stages(part == 0, part == decode_parts - 1, *attn_in,
                                   attn_o_ref, m_ref, l_ref, acc_ref)
    else:
        side = ()
    for src, dst in zip(cast_in, cast_out):
        dst[...] = src[...].astype(BF16)
    x = x_ref[...]
    xn = _rms(x, pre_ref[...]).astype(BF16)
    acc = None
    for idx, (c0, cw) in enumerate(FF_CHUNKS):
        gate = _dot(xn, wgu_ref[:, c0:c0 + cw])
        up = _dot(xn, wgu_ref[:, D_FF + c0:D_FF + c0 + cw])
        act = (gate * jax.nn.sigmoid(gate) * up).astype(BF16)
        part_out = _dot(act, wd_ref[c0:c0 + cw, :])
        acc = part_out if acc is None else acc + part_out
        if idx < len(side):
            side[idx]()
    o_ref[...] = x + 0.5 * _rms(acc, post_ref[...])


def _ffn(x, pre_g, post_g, w_gu, w_down, decode=None, casts=()):
    rows = x.shape[0]
    steps = rows // ROW_TILE
    row_spec = pl.BlockSpec((ROW_TILE, D_MODEL), lambda i: (i, 0))
    in_specs = [row_spec, _const_spec((1, D_MODEL)), _const_spec((1, D_MODEL)),
                _const_spec((D_MODEL, 2 * D_FF)), _const_spec((D_FF, D_MODEL))]
    out_specs, out_shape = [row_spec], [jax.ShapeDtypeStruct((rows, D_MODEL), F32)]
    args, scratch, parts = [x, pre_g, post_g, w_gu, w_down], [], 0
    if decode is not None:
        seq0, n_seq, lam, subln_row, q, cache_kt, cache_v4, k_new, v_new = decode
        t, past = q.shape[1], cache_kt.shape[2]
        parts = steps // n_seq
        keys = past // parts
        assert steps == n_seq * parts and past == keys * parts and keys % V7X_LANES == 0
        seq = pl.BlockSpec((1, t, D_ATTN), lambda i: (seq0 + i // parts, 0, 0))
        in_specs += [pl.BlockSpec((4, HEAD_DIM), lambda i: (0, 0)),
                     pl.BlockSpec((1, 2 * HEAD_DIM), lambda i: (0, 0)),
                     seq,
                     pl.BlockSpec((1, D_ATTN, keys), lambda i: (seq0 + i // parts, 0, i % parts)),
                     pl.BlockSpec((1, keys * H_DIFF, 2 * HEAD_DIM),
                                  lambda i: (seq0 + i // parts, i % parts, 0)),
                     seq, seq]
        out_specs.append(pl.BlockSpec((1, t, D_ATTN), lambda i: (i // parts, 0, 0)))
        out_shape.append(jax.ShapeDtypeStruct((n_seq, t, D_ATTN), F32))
        args += [lam, subln_row, q, cache_kt, cache_v4, k_new, v_new]
        scratch = [pltpu.VMEM((H_DIFF, 2 * t, 1), F32), pltpu.VMEM((H_DIFF, 2 * t, 1), F32),
                   pltpu.VMEM((H_DIFF, 2 * t, 2 * HEAD_DIM), F32)]
    for w in casts:
        per_step = w.shape[0] // steps
        span = BF16_SUBLANES // math.gcd(per_step, BF16_SUBLANES)
        assert w.shape[0] == per_step * steps and steps % span == 0
        spec = pl.BlockSpec((per_step * span, w.shape[1]), lambda i, span=span: (i // span, 0))
        in_specs.append(spec)
        out_specs.append(spec)
        out_shape.append(jax.ShapeDtypeStruct(w.shape, BF16))
        args.append(w)
    outs = pl.pallas_call(
        functools.partial(_ffn_kernel, decode_parts=parts, n_casts=len(casts)),
        grid=(steps,),
        in_specs=in_specs,
        out_specs=out_specs,
        out_shape=out_shape,
        scratch_shapes=scratch,
        compiler_params=pltpu.CompilerParams(
            dimension_semantics=("arbitrary",), vmem_limit_bytes=_mib(56 if parts else 48)),
        name="ffn_riders" if (parts or casts) else "ffn",
    )(*args)
    return outs[0] if len(outs) == 1 else outs


def _rope_tables(tile_pos0, row_pos):
    d = np.arange(V7X_LANES) % HEAD_DIM
    inv = np.where(d < ROT_DIM, ROPE_THETA ** (-(d % (ROT_DIM // 2)) / (ROT_DIM // 2)), 0.0)
    ang_a = np.asarray(tile_pos0, np.float64)[:, None] * inv[None, :]
    ang_b = np.asarray(row_pos, np.float64)[:, None] * inv[None, :]
    half = ROT_DIM // 2
    sel_lo = np.where(d < half, -1.0, 0.0)[None, :]
    sel_hi = np.where((d >= half) & (d < ROT_DIM), 1.0, 0.0)[None, :]
    f = lambda a: jnp.asarray(a, F32)
    return (f(np.cos(ang_a))[:, None, :], f(np.sin(ang_a))[:, None, :],
            f(np.cos(ang_b)), f(np.sin(ang_b)), f(sel_lo), f(sel_hi))


def _mix_in_kernel(h_ref, g_ref, win_ref, cos_a_ref, sin_a_ref, cos_b_ref, sin_b_ref,
                   sel_lo_ref, sel_hi_ref,
                   upool_ref, kf_ref, vf_ref, gates_ref, q_ref, kb_ref, v_ref, *, transposed):
    u = _rms(h_ref[...], g_ref[...]).astype(BF16)
    upool_ref[...] = _dot(u, win_ref[:, 0:D_POOL])

    ca, sa = cos_a_ref[0], sin_a_ref[0]
    cb, sb = cos_b_ref[...], sin_b_ref[...]
    cos = ca * cb - sa * sb
    sin = sa * cb + ca * sb
    rep = D_ATTN // V7X_LANES
    cos_f = jnp.concatenate([cos] * rep, axis=1)
    sin_lo = jnp.concatenate([sin * sel_lo_ref[...]] * rep, axis=1)
    sin_hi = jnp.concatenate([sin * sel_hi_ref[...]] * rep, axis=1)
    half = ROT_DIM // 2

    def rope(x):
        return (x * cos_f + pltpu.roll(x, D_ATTN - half, 1) * sin_lo
                + pltpu.roll(x, half, 1) * sin_hi)

    q = rope(_dot(u, win_ref[:, D_POOL:D_POOL + D_ATTN])) * QK_SCALE_LOG2
    k = rope(_dot(u, win_ref[:, D_POOL + D_ATTN:D_POOL + 2 * D_ATTN]))
    v = _dot(u, win_ref[:, D_POOL + 2 * D_ATTN:D_POOL + 3 * D_ATTN])
    kf_ref[...] = k
    vf_ref[...] = v.reshape(v.shape[0], H_DIFF, 2 * HEAD_DIM)
    for hh in range(H_DIFF):
        kb_ref[hh] = k[:, hh * 2 * HEAD_DIM:(hh + 1) * 2 * HEAD_DIM].astype(BF16)
    if transposed:
        comp = lax.broadcasted_iota(jnp.int32, q.shape, 1) % (2 * HEAD_DIM)
        q_top = jnp.where(comp < HEAD_DIM, q, 0.0)
        q_bot = jnp.where(comp >= HEAD_DIM, q, 0.0)
        q_ref[0, :, 0:ROW_TILE] = q_top.T.astype(BF16)
        q_ref[0, :, ROW_TILE:2 * ROW_TILE] = q_bot.T.astype(BF16)
        v_ref[0] = v.T.astype(BF16)
    else:
        q_ref[...] = q
        v_ref[...] = v.astype(BF16)

    g0 = D_POOL + 3 * D_ATTN
    for c in range(2):
        z = _dot(u, win_ref[:, g0 + c * D_MODEL:g0 + (c + 1) * D_MODEL])
        gates_ref[:, c * D_MODEL:(c + 1) * D_MODEL] = jax.nn.sigmoid(z).astype(BF16)


def _mix_in(h, mix_pre_g, w_in, tile_pos0, row_pos, transposed):
    rows = h.shape[0]
    nt = rows // ROW_TILE
    tables = _rope_tables(tile_pos0, row_pos)
    row = lambda w: pl.BlockSpec((ROW_TILE, w), lambda i: (i, 0))
    if transposed:
        q_shape = jax.ShapeDtypeStruct((nt, D_ATTN, 2 * ROW_TILE), BF16)
        q_spec = pl.BlockSpec((1, D_ATTN, 2 * ROW_TILE), lambda i: (i, 0, 0))
        v_shape = jax.ShapeDtypeStruct((nt, D_ATTN, ROW_TILE), BF16)
        v_spec = pl.BlockSpec((1, D_ATTN, ROW_TILE), lambda i: (i, 0, 0))
    else:
        q_shape, q_spec = jax.ShapeDtypeStruct((rows, D_ATTN), F32), row(D_ATTN)
        v_shape, v_spec = jax.ShapeDtypeStruct((rows, D_ATTN), BF16), row(D_ATTN)
    return pl.pallas_call(
        functools.partial(_mix_in_kernel, transposed=transposed),
        grid=(nt,),
        in_specs=[row(D_MODEL), _const_spec((1, D_MODEL)), _const_spec(w_in.shape),
                  pl.BlockSpec((1, 1, V7X_LANES), lambda i: (i, 0, 0)),
                  pl.BlockSpec((1, 1, V7X_LANES), lambda i: (i, 0, 0)),
                  _const_spec((ROW_TILE, V7X_LANES)), _const_spec((ROW_TILE, V7X_LANES)),
                  _const_spec((1, V7X_LANES)), _const_spec((1, V7X_LANES))],
        out_specs=[row(D_POOL), row(D_ATTN),
                   pl.BlockSpec((ROW_TILE, H_DIFF, 2 * HEAD_DIM), lambda i: (i, 0, 0)),
                   row(2 * D_MODEL),
                   q_spec,
                   pl.BlockSpec((H_DIFF, ROW_TILE, 2 * HEAD_DIM), lambda i: (0, i, 0)),
                   v_spec],
        out_shape=[jax.ShapeDtypeStruct((rows, D_POOL), F32),
                   jax.ShapeDtypeStruct((rows, D_ATTN), F32),
                   jax.ShapeDtypeStruct((rows, H_DIFF, 2 * HEAD_DIM), F32),
                   jax.ShapeDtypeStruct((rows, 2 * D_MODEL), BF16),
                   q_shape,
                   jax.ShapeDtypeStruct((H_DIFF, rows, 2 * HEAD_DIM), BF16),
                   v_shape],
        compiler_params=pltpu.CompilerParams(
            dimension_semantics=("arbitrary",), vmem_limit_bytes=_mib(48)),
        name="mix_in",
    )(h, mix_pre_g, w_in, *tables)


def _lambda(lam_ref):
    lq1, lk1, lq2, lk2 = (lam_ref[i:i + 1, :] for i in range(4))
    return (jnp.exp(jnp.sum(lq1 * lk1, axis=1, keepdims=True))
            - jnp.exp(jnp.sum(lq2 * lk2, axis=1, keepdims=True)) + LAM_INIT)


def _attn_prompt_kernel(lam_ref, g_ref, q_ref, qn_ref, k_ref, v_ref, o_ref, acc_ref, s_ref):
    tb = ROW_TILE
    hd2 = 2 * HEAD_DIM
    i = pl.program_id(1)
    acc_ref[...] = jnp.zeros_like(acc_ref)
    heads = [slice(h * hd2, (h + 1) * hd2) for h in range(ATTN_HEADS_PER_STEP)]
    strips = [slice(c, c + ATTN_STRIP) for c in range(0, 2 * tb, ATTN_STRIP)]

    def logits(j, rows, cols, q_blk):
        kj = k_ref[rows.start // hd2, pl.ds(pl.multiple_of(j * tb, tb), tb), :]
        return _dot(kj, q_blk[0, rows, cols])

    def consume(j, m, l, diagonal, next_logits):
        m = [list(x) for x in m]
        l = [list(x) for x in l]
        for n, cols in enumerate(strips):
            q0 = cols.start % tb
            keys = q0 + ATTN_STRIP if diagonal else tb
            for h, rows in enumerate(heads):
                s_next = next_logits(rows, cols)

                def s_strip():
                    s = s_ref[h, n, 0:keys, :]
                    if diagonal:
                        k_chunk = lax.broadcasted_iota(jnp.int32, s.shape, 0) // CHUNK
                        q_chunk = (q0 + lax.broadcasted_iota(jnp.int32, s.shape, 1)) // CHUNK
                        s = jnp.where(k_chunk <= q_chunk, s, -jnp.inf)
                    return s
                m_new = jnp.maximum(m[h][n], jnp.max(s_strip(), axis=0, keepdims=True))
                alpha = jnp.exp2(m[h][n] - m_new)
                e = jnp.exp2(s_strip() - m_new)
                pv = _dot(v_ref[j, rows, 0:keys], e.astype(BF16))
                l[h][n] = alpha * l[h][n] + jnp.sum(e, axis=0, keepdims=True)
                m[h][n] = m_new
                acc_ref[h, n] = acc_ref[h, n] * alpha + pv
                s_ref[h, n] = s_next
        return m, l

    @pl.when(i == 0)
    def _():
        for h, rows in enumerate(heads):
            for n, cols in enumerate(strips):
                s_ref[h, n] = logits(0, rows, cols, q_ref)

    m0 = [[jnp.full((1, ATTN_STRIP), -jnp.inf, F32) for _ in strips] for _ in heads]
    l0 = [[jnp.zeros((1, ATTN_STRIP), F32) for _ in strips] for _ in heads]
    def full_block(j, ml):
        return consume(j, ml[0], ml[1], False, lambda r, c: logits(j + 1, r, c, q_ref))

    m, l = lax.fori_loop(
        0, i // 2, lambda t, ml: full_block(2 * t + 1, full_block(2 * t, ml)), (m0, l0))
    m, l = lax.cond(i % 2 == 1, lambda ml: full_block(i - 1, ml), lambda ml: ml, (m, l))
    m, l = consume(i, m, l, True, lambda r, c: logits(0, r, c, qn_ref))

    lam = _lambda(lam_ref)
    for h, rows in enumerate(heads):
        o = jnp.concatenate([acc_ref[h, n] * (1.0 / l[h][n]) for n in range(len(strips))], axis=1)
        o = o[:, 0:tb] - lam * o[:, tb:2 * tb]
        ms = jnp.mean(o * o, axis=0, keepdims=True)
        o_ref[rows, :] = o * lax.rsqrt(ms + EPS) * g_ref[...] * (1.0 - LAM_INIT)


def _attn_prompt(lam, subln_col, q_t, k_b, v_t):
    nt = q_t.shape[0]
    rows = nt * ROW_TILE
    hd2 = 2 * HEAD_DIM
    gw = ATTN_HEADS_PER_STEP * hd2
    n_strips = 2 * ROW_TILE // ATTN_STRIP
    resident = dict(pipeline_mode=pl.Buffered(1))
    return pl.pallas_call(
        _attn_prompt_kernel,
        grid=(H_DIFF // ATTN_HEADS_PER_STEP, nt),
        in_specs=[pl.BlockSpec((4, HEAD_DIM), lambda h, i: (0, 0)),
                  pl.BlockSpec((hd2, 1), lambda h, i: (0, 0)),
                  pl.BlockSpec((1, gw, 2 * ROW_TILE), lambda h, i: (i, h, 0)),
                  pl.BlockSpec((1, gw, 2 * ROW_TILE), lambda h, i: (jnp.minimum(i + 1, nt - 1), h, 0)),
                  pl.BlockSpec((ATTN_HEADS_PER_STEP, rows, hd2), lambda h, i: (h, 0, 0), **resident),
                  pl.BlockSpec((nt, gw, ROW_TILE), lambda h, i: (0, h, 0), **resident)],
        out_specs=pl.BlockSpec((gw, ROW_TILE), lambda h, i: (h, i)),
        out_shape=jax.ShapeDtypeStruct((D_ATTN, rows), F32),
        scratch_shapes=[pltpu.VMEM((ATTN_HEADS_PER_STEP, n_strips, hd2, ATTN_STRIP), F32),
                        pltpu.VMEM((ATTN_HEADS_PER_STEP, n_strips, ROW_TILE, ATTN_STRIP), F32)],
        compiler_params=pltpu.CompilerParams(
            dimension_semantics=("arbitrary", "arbitrary"), vmem_limit_bytes=_mib(56)),
        name="attn_prompt",
    )(lam, subln_col, q_t, q_t, k_b, v_t)


def _merge_kernel(h_ref, attn_ref, gates_ref, halo_ref, body_ref, poolw_ref, pscale_ref,
                  wbp_ref, wba_ref, wout_ref, postg_ref, o_ref, *, hist0, attn_transposed):
    nb, tl, _ = body_ref.shape
    hist = hist0 + pl.program_id(0) * tl
    hi = hist + 1 + lax.broadcasted_iota(jnp.int32, (tl, 1), 0)
    halo_row = lax.broadcasted_iota(jnp.int32, (POOL_HALO, 1), 0)
    halo_ok = halo_row >= POOL_HALO - jnp.minimum(hist, POOL_STATE)
    pooled = []
    for b in range(nb):
        halo = jnp.where(halo_ok, halo_ref[b], 0.0)
        ext = jnp.concatenate([halo, body_ref[b]], axis=0)
        groups = []
        for g, win in enumerate(POOL_WINDOWS):
            x = ext[:, g * POOL_GROUP:(g + 1) * POOL_GROUP]
            s, shift = x, 1
            while shift < win:
                s = s + pltpu.roll(s, shift, 0)
                shift *= 2
            inv_cnt = 1.0 / jnp.minimum(hi, win).astype(F32)
            mixed = s[POOL_HALO:] * inv_cnt - x[POOL_HALO:]
            groups.append(_dot(mixed.astype(BF16), poolw_ref[g]))
        pooled.append(jnp.concatenate(groups, axis=1) * pscale_ref[...])
    pool_out = pooled[0] if nb == 1 else jnp.concatenate(pooled, axis=0)

    attn = attn_ref[...].T if attn_transposed else attn_ref[...]
    branch_pool = _dot(pool_out.astype(BF16), wbp_ref[...])
    branch_attn = _dot(attn.astype(BF16), wba_ref[...])
    merged = (gates_ref[:, 0:D_MODEL].astype(F32) * branch_pool
              + gates_ref[:, D_MODEL:2 * D_MODEL].astype(F32) * branch_attn)
    o_ref[...] = h_ref[...] + _rms(_dot(merged.astype(BF16), wout_ref[...]), postg_ref[...])


def _merge(h, attn, gates, halo, body, halo_spec, body_spec, hist0, attn_transposed,
           pool_w, pool_scale, w_bp, w_ba, w_out, post_g):
    rows = h.shape[0]
    row = lambda w: pl.BlockSpec((ROW_TILE, w), lambda i: (i, 0))
    attn_spec = (pl.BlockSpec((D_ATTN, ROW_TILE), lambda i: (0, i)) if attn_transposed
                 else row(D_ATTN))
    return pl.pallas_call(
        functools.partial(_merge_kernel, hist0=hist0, attn_transposed=attn_transposed),
        grid=(rows // ROW_TILE,),
        in_specs=[row(D_MODEL), attn_spec, row(2 * D_MODEL), halo_spec, body_spec,
                  _const_spec(pool_w.shape), _const_spec(pool_scale.shape),
                  _const_spec(w_bp.shape), _const_spec(w_ba.shape), _const_spec(w_out.shape),
                  _const_spec(post_g.shape)],
        out_specs=row(D_MODEL),
        out_shape=jax.ShapeDtypeStruct((rows, D_MODEL), F32),
        compiler_params=pltpu.CompilerParams(
            dimension_semantics=("arbitrary",), vmem_limit_bytes=_mib(48)),
        name="merge",
    )(h, attn, gates, halo, body, pool_w, pool_scale, w_bp, w_ba, w_out, post_g)


def kernel(x_prompt, x_sample, cache_k, cache_v, state_pool, ffn1_pre_g, ffn1_post_g, ffn1_w_gu, ffn1_w_down, mix_pre_g, mix_post_g, w_in, pool_w, pool_scale, lambda_q1, lambda_k1, lambda_q2, lambda_k2, subln_g, w_branch_pool, w_branch_attn, w_out, ffn2_pre_g, ffn2_post_g, ffn2_w_gu, ffn2_w_down):
    bp, seq, _ = x_prompt.shape
    db, dseq, _ = x_sample.shape
    past = cache_k.shape[2]
    assert bp == 1 and seq % ROW_TILE == 0 and db * dseq == ROW_TILE and dseq >= POOL_STATE
    assert (past + dseq - 1) // CHUNK <= past // CHUNK and past % CHUNK == 0

    l = 0
    bf = lambda w: w[l].astype(BF16)
    g1pre, g1post, gmpre, gmpost = ffn1_pre_g[l:l + 1], ffn1_post_g[l:l + 1], mix_pre_g[l:l + 1], mix_post_g[l:l + 1]
    g2pre, g2post = ffn2_pre_g[l:l + 1], ffn2_post_g[l:l + 1]
    w1gu, w1d, win = bf(ffn1_w_gu), bf(ffn1_w_down), bf(w_in)
    later_w = (ffn2_w_gu[l], ffn2_w_down[l], pool_w[l].reshape(D_POOL, POOL_GROUP),
               w_branch_pool[l], w_branch_attn[l], w_out[l])
    pscale = pool_scale[l:l + 1]
    lam = jnp.concatenate([lambda_q1[l:l + 1], lambda_k1[l:l + 1],
                           lambda_q2[l:l + 1], lambda_k2[l:l + 1]], axis=0)
    subln_row = subln_g[l:l + 1]
    subln_col = subln_g[l][:, None]

    xs = x_sample.reshape(db * dseq, D_MODEL)
    hs = _ffn(xs, g1pre, g1post, w1gu, w1d)
    upool_s, kf_s, vf_s, gates_s, q_s, _, _ = _mix_in(
        hs, gmpre, win, np.array([past]), np.arange(ROW_TILE) % dseq, transposed=False)
    cache_kt = jnp.transpose(cache_k[l], (0, 2, 3, 4, 1)).reshape(db, D_ATTN, past)
    cache_v4 = cache_v[l].reshape(db, past * H_DIFF, 2 * HEAD_DIM)
    decode_attn = lambda seq0: (seq0, db // 2, lam, subln_row, q_s.reshape(db, dseq, D_ATTN),
                                cache_kt, cache_v4, kf_s.reshape(db, dseq, D_ATTN),
                                vf_s.reshape(db, dseq, D_ATTN))

    xp = x_prompt.reshape(seq, D_MODEL)
    nt = seq // ROW_TILE
    hp, attn_s0, w2gu, w2d, wpool, wbp, wba, wo = _ffn(
        xp, g1pre, g1post, w1gu, w1d, decode_attn(0), casts=later_w)
    merge_w = (wpool.reshape(len(POOL_WINDOWS), POOL_GROUP, POOL_GROUP), pscale, wbp, wba, wo, gmpost)
    upool, kf, vf, gates, q_t, k_b, v_t = _mix_in(
        hp, gmpre, win, np.arange(nt) * ROW_TILE, np.arange(ROW_TILE), transposed=True)
    attn_t = _attn_prompt(lam, subln_col, q_t, k_b, v_t)
    per_tile = ROW_TILE // POOL_HALO
    halo_spec = pl.BlockSpec((1, POOL_HALO, D_POOL),
                             lambda i: (jnp.maximum(i * per_tile - 1, 0), 0, 0))
    body_spec = pl.BlockSpec((1, ROW_TILE, D_POOL), lambda i: (i, 0, 0))
    hp = _merge(hp, attn_t, gates, upool.reshape(seq // POOL_HALO, POOL_HALO, D_POOL),
                upool.reshape(nt, ROW_TILE, D_POOL), halo_spec, body_spec, 0, True, *merge_w)
    y_prompt, attn_s1 = _ffn(hp, g2pre, g2post, w2gu, w2d, decode_attn(db // 2))
    y_prompt = y_prompt.reshape(bp, seq, D_MODEL)
    new_k_p = kf.reshape(1, bp, seq, H_DIFF, 2, HEAD_DIM)
    new_v_p = vf.reshape(1, bp, seq, H_DIFF, 2 * HEAD_DIM)
    new_pool_p = upool[seq - POOL_STATE:].reshape(1, bp, POOL_STATE, D_POOL)

    attn_s = jnp.concatenate([attn_s0, attn_s1], axis=0)
    hist = state_pool[l]
    halo_s = jnp.concatenate(
        [jnp.zeros((db, POOL_HALO - POOL_STATE, D_POOL), F32), hist], axis=1)
    body_s = upool_s.reshape(db, dseq, D_POOL)
    hs = _merge(hs, attn_s.reshape(db * dseq, D_ATTN), gates_s, halo_s, body_s,
                pl.BlockSpec((db, POOL_HALO, D_POOL), lambda i: (0, 0, 0)),
                pl.BlockSpec((db, dseq, D_POOL), lambda i: (0, 0, 0)),
                POOL_STATE, False, *merge_w)
    y_sample = _ffn(hs, g2pre, g2post, w2gu, w2d).reshape(db, dseq, D_MODEL)
    new_k_s = kf_s.reshape(1, db, dseq, H_DIFF, 2, HEAD_DIM)
    new_v_s = vf_s.reshape(1, db, dseq, H_DIFF, 2 * HEAD_DIM)
    new_pool_s = body_s[:, dseq - POOL_STATE:].reshape(1, db, POOL_STATE, D_POOL)

    return (y_prompt, y_sample, new_k_p, new_v_p, new_pool_p, new_k_s, new_v_s, new_pool_s)
```

```python
import functools
import math

import numpy as np
import jax
import jax.numpy as jnp
from jax import lax
from jax.experimental import pallas as pl
from jax.experimental.pallas import tpu as pltpu

D_MODEL = 1024
D_FF = 2816
D_POOL = 512
POOL_WINDOWS = (2, 4, 8, 16)
POOL_GROUP = 128
POOL_STATE = 15
H_DIFF = 4
HEAD_DIM = 64
D_ATTN = 512
ROT_DIM = 16
ROPE_THETA = 500000.0
CHUNK = 64
EPS = 1e-6
LAM_INIT = 0.8 - 0.6 * math.exp(-0.3 * 0)

V7X_LANES = 128
BF16_SUBLANES = 16
POOL_HALO = 16
ROW_TILE = 512
FF_CHUNKS = ((0, 1024), (1024, 1024), (2048, 768))
ATTN_STRIP = 256
ATTN_HEADS_PER_STEP = 4
QK_SCALE_LOG2 = HEAD_DIM ** -0.5 * math.log2(math.e)

F32 = jnp.float32
BF16 = jnp.bfloat16


def _mib(n):
    return int(n * 1024 * 1024)


def _rms(x, g):
    return x * lax.rsqrt(jnp.mean(x * x, axis=-1, keepdims=True) + EPS) * g


def _dot(a, b):
    return jnp.dot(a, b, preferred_element_type=F32)


def _const_spec(shape):
    nd = len(shape)
    return pl.BlockSpec(shape, lambda *_: (0,) * nd, pipeline_mode=pl.Buffered(1))


def _decode_attn_stages(first, last, lam_ref, g_ref, q_ref, ck_ref, cv_ref, kn_ref, vn_ref,
                        o_ref, m_ref, l_ref, acc_ref):
    hd2 = 2 * HEAD_DIM
    keys = ck_ref.shape[2]
    nt_dims = (((1,), (1,)), ((), ()))
    heads = [slice(h * hd2, (h + 1) * hd2) for h in range(H_DIFF)]
    state = {}

    def scores():
        for h, sl in enumerate(heads):
            q = q_ref[0, :, sl]
            comp = lax.broadcasted_iota(jnp.int32, q.shape, 1)
            lhs = jnp.concatenate([jnp.where(comp < HEAD_DIM, q, 0.0),
                                   jnp.where(comp >= HEAD_DIM, q, 0.0)], axis=0).astype(BF16)
            s_c = _dot(lhs, ck_ref[0, sl, :].astype(BF16))
            s_n = lax.dot_general(lhs, kn_ref[0, :, sl].astype(BF16), nt_dims,
                                  preferred_element_type=F32)
            s_n = jnp.where(last, s_n, -jnp.inf)
            m_old = jnp.where(first, -jnp.inf, m_ref[h])
            m_new = jnp.maximum(m_old, jnp.maximum(jnp.max(s_c, axis=1, keepdims=True),
                                                   jnp.max(s_n, axis=1, keepdims=True)))
            alpha = jnp.exp2(m_old - m_new)
            e_c = jnp.exp2(s_c - m_new)
            e_n = jnp.exp2(s_n - m_new)
            l_new = (alpha * jnp.where(first, 0.0, l_ref[h])
                     + jnp.sum(e_c, axis=1, keepdims=True) + jnp.sum(e_n, axis=1, keepdims=True))
            m_ref[h] = m_new
            l_ref[h] = l_new
            state[h] = (alpha, e_c.astype(BF16), e_n.astype(BF16), l_new)

    def outputs():
        lam = _lambda(lam_ref)
        for h, sl in enumerate(heads):
            alpha, e_c, e_n, l_new = state[h]
            v_c = cv_ref[0, pl.ds(h, keys, stride=H_DIFF), :].astype(BF16)
            acc = (alpha * jnp.where(first, 0.0, acc_ref[h])
                   + _dot(e_c, v_c) + _dot(e_n, vn_ref[0, :, sl].astype(BF16)))
            acc_ref[h] = acc
            o = acc * (1.0 / l_new)
            t = o.shape[0] // 2
            o = o[0:t] - lam * o[t:2 * t]
            o_ref[0, :, sl] = _rms(o, g_ref[...]) * (1.0 - LAM_INIT)

    return scores, outputs


def _ffn_kernel(*refs, decode_parts, n_casts):
    x_ref, pre_ref, post_ref, wgu_ref, wd_ref = refs[:5]
    n_attn_in = 7 if decode_parts else 0
    attn_in = refs[5:5 + n_attn_in]
    cast_in = refs[5 + n_attn_in:5 + n_attn_in + n_casts]
    n_out = 1 + (1 if decode_parts else 0)
    outs = refs[5 + n_attn_in + n_casts:]
    o_ref = outs[0]
    cast_out = outs[n_out:n_out + n_casts]
    if decode_parts:
        attn_o_ref = outs[1]
        m_ref, l_ref, acc_ref = outs[n_out + n_casts:]
        step = pl.program_id(0)

        @pl.when(step == 0)
        def _():
            m_ref[...] = jnp.zeros_like(m_ref)
            l_ref[...] = jnp.zeros_like(l_ref)
            acc_ref[...] = jnp.zeros_like(acc_ref)

        part = step % decode_parts
        side = _decode_attn_stages(part == 0, part == decode_parts - 1, *attn_in,
                                   attn_o_ref, m_ref, l_ref, acc_ref)
    else:
        side = ()
    for src, dst in zip(cast_in, cast_out):
        dst[...] = src[...].astype(BF16)
    x = x_ref[...]
    xn = _rms(x, pre_ref[...]).astype(BF16)
    acc = None
    for idx, (c0, cw) in enumerate(FF_CHUNKS):
        gate = _dot(xn, wgu_ref[:, c0:c0 + cw])
        up = _dot(xn, wgu_ref[:, D_FF + c0:D_FF + c0 + cw])
        act = (gate * jax.nn.sigmoid(gate) * up).astype(BF16)
        part_out = _dot(act, wd_ref[c0:c0 + cw, :])
        acc = part_out if acc is None else acc + part_out
        if idx < len(side):
            side[idx]()
    o_ref[...] = x + 0.5 * _rms(acc, post_ref[...])


def _ffn(x, pre_g, post_g, w_gu, w_down, decode=None, casts=()):
    rows = x.shape[0]
    steps = rows // ROW_TILE
    row_spec = pl.BlockSpec((ROW_TILE, D_MODEL), lambda i: (i, 0))
    in_specs = [row_spec, _const_spec((1, D_MODEL)), _const_spec((1, D_MODEL)),
                _const_spec((D_MODEL, 2 * D_FF)), _const_spec((D_FF, D_MODEL))]
    out_specs, out_shape = [row_spec], [jax.ShapeDtypeStruct((rows, D_MODEL), F32)]
    args, scratch, parts = [x, pre_g, post_g, w_gu, w_down], [], 0
    if decode is not None:
        seq0, n_seq, lam, subln_row, q, cache_kt, cache_v4, k_new, v_new = decode
        t, past = q.shape[1], cache_kt.shape[2]
        parts = steps // n_seq
        keys = past // parts
        assert steps == n_seq * parts and past == keys * parts and keys % V7X_LANES == 0
        seq = pl.BlockSpec((1, t, D_ATTN), lambda i: (seq0 + i // parts, 0, 0))
        in_specs += [pl.BlockSpec((4, HEAD_DIM), lambda i: (0, 0)),
                     pl.BlockSpec((1, 2 * HEAD_DIM), lambda i: (0, 0)),
                     seq,
                     pl.BlockSpec((1, D_ATTN, keys), lambda i: (seq0 + i // parts, 0, i % parts)),
                     pl.BlockSpec((1, keys * H_DIFF, 2 * HEAD_DIM),
                                  lambda i: (seq0 + i // parts, i % parts, 0)),
                     seq, seq]
        out_specs.append(pl.BlockSpec((1, t, D_ATTN), lambda i: (i // parts, 0, 0)))
        out_shape.append(jax.ShapeDtypeStruct((n_seq, t, D_ATTN), F32))
        args += [lam, subln_row, q, cache_kt, cache_v4, k_new, v_new]
        scratch = [pltpu.VMEM((H_DIFF, 2 * t, 1), F32), pltpu.VMEM((H_DIFF, 2 * t, 1), F32),
                   pltpu.VMEM((H_DIFF, 2 * t, 2 * HEAD_DIM), F32)]
    for w in casts:
        per_step = w.shape[0] // steps
        span = BF16_SUBLANES // math.gcd(per_step, BF16_SUBLANES)
        assert w.shape[0] == per_step * steps and steps % span == 0
        spec = pl.BlockSpec((per_step * span, w.shape[1]), lambda i, span=span: (i // span, 0))
        in_specs.append(spec)
        out_specs.append(spec)
        out_shape.append(jax.ShapeDtypeStruct(w.shape, BF16))
        args.append(w)
    outs = pl.pallas_call(
        functools.partial(_ffn_kernel, decode_parts=parts, n_casts=len(casts)),
        grid=(steps,),
        in_specs=in_specs,
        out_specs=out_specs,
        out_shape=out_shape,
        scratch_shapes=scratch,
        compiler_params=pltpu.CompilerParams(
            dimension_semantics=("arbitrary",), vmem_limit_bytes=_mib(56 if parts else 48)),
        name="ffn_riders" if (parts or casts) else "ffn",
    )(*args)
    return outs[0] if len(outs) == 1 else outs


def _rope_tables(tile_pos0, row_pos):
    d = np.arange(V7X_LANES) % HEAD_DIM
    inv = np.where(d < ROT_DIM, ROPE_THETA ** (-(d % (ROT_DIM // 2)) / (ROT_DIM // 2)), 0.0)
    ang_a = np.asarray(tile_pos0, np.float64)[:, None] * inv[None, :]
    ang_b = np.asarray(row_pos, np.float64)[:, None] * inv[None, :]
    half = ROT_DIM // 2
    sel_lo = np.where(d < half, -1.0, 0.0)[None, :]
    sel_hi = np.where((d >= half) & (d < ROT_DIM), 1.0, 0.0)[None, :]
    f = lambda a: jnp.asarray(a, F32)
    return (f(np.cos(ang_a))[:, None, :], f(np.sin(ang_a))[:, None, :],
            f(np.cos(ang_b)), f(np.sin(ang_b)), f(sel_lo), f(sel_hi))


def _mix_in_kernel(h_ref, g_ref, win_ref, cos_a_ref, sin_a_ref, cos_b_ref, sin_b_ref,
                   sel_lo_ref, sel_hi_ref,
                   upool_ref, kf_ref, vf_ref, gates_ref, q_ref, kb_ref, v_ref, *, transposed):
    u = _rms(h_ref[...], g_ref[...]).astype(BF16)
    upool_ref[...] = _dot(u, win_ref[:, 0:D_POOL])

    ca, sa = cos_a_ref[0], sin_a_ref[0]
    cb, sb = cos_b_ref[...], sin_b_ref[...]
    cos = ca * cb - sa * sb
    sin = sa * cb + ca * sb
    rep = D_ATTN // V7X_LANES
    cos_f = jnp.concatenate([cos] * rep, axis=1)
    sin_lo = jnp.concatenate([sin * sel_lo_ref[...]] * rep, axis=1)
    sin_hi = jnp.concatenate([sin * sel_hi_ref[...]] * rep, axis=1)
    half = ROT_DIM // 2

    def rope(x):
        return (x * cos_f + pltpu.roll(x, D_ATTN - half, 1) * sin_lo
                + pltpu.roll(x, half, 1) * sin_hi)

    q = rope(_dot(u, win_ref[:, D_POOL:D_POOL + D_ATTN])) * QK_SCALE_LOG2
    k = rope(_dot(u, win_ref[:, D_POOL + D_ATTN:D_POOL + 2 * D_ATTN]))
    v = _dot(u, win_ref[:, D_POOL + 2 * D_ATTN:D_POOL + 3 * D_ATTN])
    kf_ref[...] = k
    vf_ref[...] = v.reshape(v.shape[0], H_DIFF, 2 * HEAD_DIM)
    for hh in range(H_DIFF):
        kb_ref[hh] = k[:, hh * 2 * HEAD_DIM:(hh + 1) * 2 * HEAD_DIM].astype(BF16)
    if transposed:
        comp = lax.broadcasted_iota(jnp.int32, q.shape, 1) % (2 * HEAD_DIM)
        q_top = jnp.where(comp < HEAD_DIM, q, 0.0)
        q_bot = jnp.where(comp >= HEAD_DIM, q, 0.0)
        q_ref[0, :, 0:ROW_TILE] = q_top.T.astype(BF16)
        q_ref[0, :, ROW_TILE:2 * ROW_TILE] = q_bot.T.astype(BF16)
        v_ref[0] = v.T.astype(BF16)
    else:
        q_ref[...] = q
        v_ref[...] = v.astype(BF16)

    g0 = D_POOL + 3 * D_ATTN
    for c in range(2):
        z = _dot(u, win_ref[:, g0 + c * D_MODEL:g0 + (c + 1) * D_MODEL])
        gates_ref[:, c * D_MODEL:(c + 1) * D_MODEL] = jax.nn.sigmoid(z).astype(BF16)


def _mix_in(h, mix_pre_g, w_in, tile_pos0, row_pos, transposed):
    rows = h.shape[0]
    nt = rows // ROW_TILE
    tables = _rope_tables(tile_pos0, row_pos)
    row = lambda w: pl.BlockSpec((ROW_TILE, w), lambda i: (i, 0))
    if transposed:
        q_shape = jax.ShapeDtypeStruct((nt, D_ATTN, 2 * ROW_TILE), BF16)
        q_spec = pl.BlockSpec((1, D_ATTN, 2 * ROW_TILE), lambda i: (i, 0, 0))
        v_shape = jax.ShapeDtypeStruct((nt, D_ATTN, ROW_TILE), BF16)
        v_spec = pl.BlockSpec((1, D_ATTN, ROW_TILE), lambda i: (i, 0, 0))
    else:
        q_shape, q_spec = jax.ShapeDtypeStruct((rows, D_ATTN), F32), row(D_ATTN)
        v_shape, v_spec = jax.ShapeDtypeStruct((rows, D_ATTN), BF16), row(D_ATTN)
    return pl.pallas_call(
        functools.partial(_mix_in_kernel, transposed=transposed),
        grid=(nt,),
        in_specs=[row(D_MODEL), _const_spec((1, D_MODEL)), _const_spec(w_in.shape),
                  pl.BlockSpec((1, 1, V7X_LANES), lambda i: (i, 0, 0)),
                  pl.BlockSpec((1, 1, V7X_LANES), lambda i: (i, 0, 0)),
                  _const_spec((ROW_TILE, V7X_LANES)), _const_spec((ROW_TILE, V7X_LANES)),
                  _const_spec((1, V7X_LANES)), _const_spec((1, V7X_LANES))],
        out_specs=[row(D_POOL), row(D_ATTN),
                   pl.BlockSpec((ROW_TILE, H_DIFF, 2 * HEAD_DIM), lambda i: (i, 0, 0)),
                   row(2 * D_MODEL),
                   q_spec,
                   pl.BlockSpec((H_DIFF, ROW_TILE, 2 * HEAD_DIM), lambda i: (0, i, 0)),
                   v_spec],
        out_shape=[jax.ShapeDtypeStruct((rows, D_POOL), F32),
                   jax.ShapeDtypeStruct((rows, D_ATTN), F32),
                   jax.ShapeDtypeStruct((rows, H_DIFF, 2 * HEAD_DIM), F32),
                   jax.ShapeDtypeStruct((rows, 2 * D_MODEL), BF16),
                   q_shape,
                   jax.ShapeDtypeStruct((H_DIFF, rows, 2 * HEAD_DIM), BF16),
                   v_shape],
        compiler_params=pltpu.CompilerParams(
            dimension_semantics=("arbitrary",), vmem_limit_bytes=_mib(48)),
        name="mix_in",
    )(h, mix_pre_g, w_in, *tables)


def _lambda(lam_ref):
    lq1, lk1, lq2, lk2 = (lam_ref[i:i + 1, :] for i in range(4))
    return (jnp.exp(jnp.sum(lq1 * lk1, axis=1, keepdims=True))
            - jnp.exp(jnp.sum(lq2 * lk2, axis=1, keepdims=True)) + LAM_INIT)


def _attn_prompt_kernel(lam_ref, g_ref, q_ref, qn_ref, k_ref, v_ref, o_ref, acc_ref, s_ref, smax_ref):
    tb = ROW_TILE
    hd2 = 2 * HEAD_DIM
    i = pl.program_id(1)
    acc_ref[...] = jnp.zeros_like(acc_ref)
    heads = [slice(h * hd2, (h + 1) * hd2) for h in range(ATTN_HEADS_PER_STEP)]
    strips = [slice(c, c + ATTN_STRIP) for c in range(0, 2 * tb, ATTN_STRIP)]

    def logits(j, rows, cols, q_blk):
        kj = k_ref[rows.start // hd2, pl.ds(pl.multiple_of(j * tb, tb), tb), :]
        return _dot(kj, q_blk[0, rows, cols])

    def consume(j, m, l, smax, diagonal, next_logits):
        m = [list(x) for x in m]
        l = [list(x) for x in l]
        smax = [list(x) for x in smax]
        for n, cols in enumerate(strips):
            q0 = cols.start % tb
            keys = q0 + ATTN_STRIP if diagonal else tb
            for h, rows in enumerate(heads):
                s_next = next_logits(rows, cols)
                s = s_ref[h, n, 0:keys, :]
                if diagonal:
                    k_chunk = lax.broadcasted_iota(jnp.int32, s.shape, 0) // CHUNK
                    q_chunk = (q0 + lax.broadcasted_iota(jnp.int32, s.shape, 1)) // CHUNK
                    s = jnp.where(k_chunk <= q_chunk, s, -jnp.inf)
                    block_max = jnp.max(s, axis=0, keepdims=True)
                else:
                    block_max = smax[h][n]
                m_new = jnp.maximum(m[h][n], block_max)
                alpha = jnp.exp2(m[h][n] - m_new)
                e = jnp.exp2(s - m_new)
                pv = _dot(v_ref[j, rows, 0:keys], e.astype(BF16))
                l[h][n] = alpha * l[h][n] + jnp.sum(e, axis=0, keepdims=True)
                m[h][n] = m_new
                acc_ref[h, n] = acc_ref[h, n] * alpha + pv
                s_ref[h, n] = s_next
                smax[h][n] = jnp.max(s_next, axis=0, keepdims=True)
        return m, l, smax

    @pl.when(i == 0)
    def _():
        for h, rows in enumerate(heads):
            for n, cols in enumerate(strips):
                s0 = logits(0, rows, cols, q_ref)
                s_ref[h, n] = s0
                smax_ref[h, n] = jnp.max(s0, axis=0, keepdims=True)

    m0 = [[jnp.full((1, ATTN_STRIP), -jnp.inf, F32) for _ in strips] for _ in heads]
    l0 = [[jnp.zeros((1, ATTN_STRIP), F32) for _ in strips] for _ in heads]
    smax0 = [[smax_ref[h, n] for n in range(len(strips))] for h in range(len(heads))]

    def full_block(j, state):
        return consume(j, *state, False, lambda r, c: logits(j + 1, r, c, q_ref))

    state = lax.fori_loop(
        0, i // 2, lambda t, st: full_block(2 * t + 1, full_block(2 * t, st)), (m0, l0, smax0))
    state = lax.cond(i % 2 == 1, lambda st: full_block(i - 1, st), lambda st: st, state)
    m, l, smax = consume(i, *state, True, lambda r, c: logits(0, r, c, qn_ref))
    for h in range(len(heads)):
        for n in range(len(strips)):
            smax_ref[h, n] = smax[h][n]

    lam = _lambda(lam_ref)
    for h, rows in enumerate(heads):
        o = jnp.concatenate([acc_ref[h, n] * (1.0 / l[h][n]) for n in range(len(strips))], axis=1)
        o = o[:, 0:tb] - lam * o[:, tb:2 * tb]
        ms = jnp.mean(o * o, axis=0, keepdims=True)
        o_ref[rows, :] = o * lax.rsqrt(ms + EPS) * g_ref[...] * (1.0 - LAM_INIT)


def _attn_prompt(lam, subln_col, q_t, k_b, v_t):
    nt = q_t.shape[0]
    rows = nt * ROW_TILE
    hd2 = 2 * HEAD_DIM
    gw = ATTN_HEADS_PER_STEP * hd2
    n_strips = 2 * ROW_TILE // ATTN_STRIP
    resident = dict(pipeline_mode=pl.Buffered(1))
    return pl.pallas_call(
        _attn_prompt_kernel,
        grid=(H_DIFF // ATTN_HEADS_PER_STEP, nt),
        in_specs=[pl.BlockSpec((4, HEAD_DIM), lambda h, i: (0, 0)),
                  pl.BlockSpec((hd2, 1), lambda h, i: (0, 0)),
                  pl.BlockSpec((1, gw, 2 * ROW_TILE), lambda h, i: (i, h, 0)),
                  pl.BlockSpec((1, gw, 2 * ROW_TILE), lambda h, i: (jnp.minimum(i + 1, nt - 1), h, 0)),
                  pl.BlockSpec((ATTN_HEADS_PER_STEP, rows, hd2), lambda h, i: (h, 0, 0), **resident),
                  pl.BlockSpec((nt, gw, ROW_TILE), lambda h, i: (0, h, 0), **resident)],
        out_specs=pl.BlockSpec((gw, ROW_TILE), lambda h, i: (h, i)),
        out_shape=jax.ShapeDtypeStruct((D_ATTN, rows), F32),
        scratch_shapes=[pltpu.VMEM((ATTN_HEADS_PER_STEP, n_strips, hd2, ATTN_STRIP), F32),
                        pltpu.VMEM((ATTN_HEADS_PER_STEP, n_strips, ROW_TILE, ATTN_STRIP), F32),
                        pltpu.VMEM((ATTN_HEADS_PER_STEP, n_strips, 1, ATTN_STRIP), F32)],
        compiler_params=pltpu.CompilerParams(
            dimension_semantics=("arbitrary", "arbitrary"), vmem_limit_bytes=_mib(56)),
        name="attn_prompt",
    )(lam, subln_col, q_t, q_t, k_b, v_t)


def _merge_kernel(h_ref, attn_ref, gates_ref, halo_ref, body_ref, poolw_ref, pscale_ref,
                  wbp_ref, wba_ref, wout_ref, postg_ref, o_ref, *, hist0, attn_transposed):
    nb, tl, _ = body_ref.shape
    hist = hist0 + pl.program_id(0) * tl
    hi = hist + 1 + lax.broadcasted_iota(jnp.int32, (tl, 1), 0)
    halo_row = lax.broadcasted_iota(jnp.int32, (POOL_HALO, 1), 0)
    halo_ok = halo_row >= POOL_HALO - jnp.minimum(hist, POOL_STATE)
    pooled = []
    for b in range(nb):
        halo = jnp.where(halo_ok, halo_ref[b], 0.0)
        ext = jnp.concatenate([halo, body_ref[b]], axis=0)
        groups = []
        for g, win in enumerate(POOL_WINDOWS):
            x = ext[:, g * POOL_GROUP:(g + 1) * POOL_GROUP]
            s, shift = x, 1
            while shift < win:
                s = s + pltpu.roll(s, shift, 0)
                shift *= 2
            inv_cnt = 1.0 / jnp.minimum(hi, win).astype(F32)
            mixed = s[POOL_HALO:] * inv_cnt - x[POOL_HALO:]
            groups.append(_dot(mixed.astype(BF16), poolw_ref[g]))
        pooled.append(jnp.concatenate(groups, axis=1) * pscale_ref[...])
    pool_out = pooled[0] if nb == 1 else jnp.concatenate(pooled, axis=0)

    attn = attn_ref[...].T if attn_transposed else attn_ref[...]
    branch_pool = _dot(pool_out.astype(BF16), wbp_ref[...])
    branch_attn = _dot(attn.astype(BF16), wba_ref[...])
    merged = (gates_ref[:, 0:D_MODEL].astype(F32) * branch_pool
              + gates_ref[:, D_MODEL:2 * D_MODEL].astype(F32) * branch_attn)
    o_ref[...] = h_ref[...] + _rms(_dot(merged.astype(BF16), wout_ref[...]), postg_ref[...])


def _merge(h, attn, gates, halo, body, halo_spec, body_spec, hist0, attn_transposed,
           pool_w, pool_scale, w_bp, w_ba, w_out, post_g):
    rows = h.shape[0]
    row = lambda w: pl.BlockSpec((ROW_TILE, w), lambda i: (i, 0))
    attn_spec = (pl.BlockSpec((D_ATTN, ROW_TILE), lambda i: (0, i)) if attn_transposed
                 else row(D_ATTN))
    return pl.pallas_call(
        functools.partial(_merge_kernel, hist0=hist0, attn_transposed=attn_transposed),
        grid=(rows // ROW_TILE,),
        in_specs=[row(D_MODEL), attn_spec, row(2 * D_MODEL), halo_spec, body_spec,
                  _const_spec(pool_w.shape), _const_spec(pool_scale.shape),
                  _const_spec(w_bp.shape), _const_spec(w_ba.shape), _const_spec(w_out.shape),
                  _const_spec(post_g.shape)],
        out_specs=row(D_MODEL),
        out_shape=jax.ShapeDtypeStruct((rows, D_MODEL), F32),
        compiler_params=pltpu.CompilerParams(
            dimension_semantics=("arbitrary",), vmem_limit_bytes=_mib(48)),
        name="merge",
    )(h, attn, gates, halo, body, pool_w, pool_scale, w_bp, w_ba, w_out, post_g)


def kernel(x_prompt, x_sample, cache_k, cache_v, state_pool, ffn1_pre_g, ffn1_post_g, ffn1_w_gu, ffn1_w_down, mix_pre_g, mix_post_g, w_in, pool_w, pool_scale, lambda_q1, lambda_k1, lambda_q2, lambda_k2, subln_g, w_branch_pool, w_branch_attn, w_out, ffn2_pre_g, ffn2_post_g, ffn2_w_gu, ffn2_w_down):
    bp, seq, _ = x_prompt.shape
    db, dseq, _ = x_sample.shape
    past = cache_k.shape[2]
    assert bp == 1 and seq % ROW_TILE == 0 and db * dseq == ROW_TILE and dseq >= POOL_STATE
    assert (past + dseq - 1) // CHUNK <= past // CHUNK and past % CHUNK == 0

    l = 0
    bf = lambda w: w[l].astype(BF16)
    g1pre, g1post, gmpre, gmpost = ffn1_pre_g[l:l + 1], ffn1_post_g[l:l + 1], mix_pre_g[l:l + 1], mix_post_g[l:l + 1]
    g2pre, g2post = ffn2_pre_g[l:l + 1], ffn2_post_g[l:l + 1]
    w1gu, w1d, win = bf(ffn1_w_gu), bf(ffn1_w_down), bf(w_in)
    later_w = (ffn2_w_gu[l], ffn2_w_down[l], pool_w[l].reshape(D_POOL, POOL_GROUP),
               w_branch_pool[l], w_branch_attn[l], w_out[l])
    pscale = pool_scale[l:l + 1]
    lam = jnp.concatenate([lambda_q1[l:l + 1], lambda_k1[l:l + 1],
                           lambda_q2[l:l + 1], lambda_k2[l:l + 1]], axis=0)
    subln_row = subln_g[l:l + 1]
    subln_col = subln_g[l][:, None]

    xs = x_sample.reshape(db * dseq, D_MODEL)
    hs = _ffn(xs, g1pre, g1post, w1gu, w1d)
    upool_s, kf_s, vf_s, gates_s, q_s, _, _ = _mix_in(
        hs, gmpre, win, np.array([past]), np.arange(ROW_TILE) % dseq, transposed=False)
    cache_kt = jnp.transpose(cache_k[l], (0, 2, 3, 4, 1)).reshape(db, D_ATTN, past)
    cache_v4 = cache_v[l].reshape(db, past * H_DIFF, 2 * HEAD_DIM)
    decode_attn = lambda seq0: (seq0, db // 2, lam, subln_row, q_s.reshape(db, dseq, D_ATTN),
                                cache_kt, cache_v4, kf_s.reshape(db, dseq, D_ATTN),
                                vf_s.reshape(db, dseq, D_ATTN))

    xp = x_prompt.reshape(seq, D_MODEL)
    nt = seq // ROW_TILE
    hp, attn_s0, w2gu, w2d, wpool, wbp, wba, wo = _ffn(
        xp, g1pre, g1post, w1gu, w1d, decode_attn(0), casts=later_w)
    merge_w = (wpool.reshape(len(POOL_WINDOWS), POOL_GROUP, POOL_GROUP), pscale, wbp, wba, wo, gmpost)
    upool, kf, vf, gates, q_t, k_b, v_t = _mix_in(
        hp, gmpre, win, np.arange(nt) * ROW_TILE, np.arange(ROW_TILE), transposed=True)
    attn_t = _attn_prompt(lam, subln_col, q_t, k_b, v_t)
    per_tile = ROW_TILE // POOL_HALO
    halo_spec = pl.BlockSpec((1, POOL_HALO, D_POOL),
                             lambda i: (jnp.maximum(i * per_tile - 1, 0), 0, 0))
    body_spec = pl.BlockSpec((1, ROW_TILE, D_POOL), lambda i: (i, 0, 0))
    hp = _merge(hp, attn_t, gates, upool.reshape(seq // POOL_HALO, POOL_HALO, D_POOL),
                upool.reshape(nt, ROW_TILE, D_POOL), halo_spec, body_spec, 0, True, *merge_w)
    y_prompt, attn_s1 = _ffn(hp, g2pre, g2post, w2gu, w2d, decode_attn(db // 2))
    y_prompt = y_prompt.reshape(bp, seq, D_MODEL)
    new_k_p = kf.reshape(1, bp, seq, H_DIFF, 2, HEAD_DIM)
    new_v_p = vf.reshape(1, bp, seq, H_DIFF, 2 * HEAD_DIM)
    new_pool_p = upool[seq - POOL_STATE:].reshape(1, bp, POOL_STATE, D_POOL)

    attn_s = jnp.concatenate([attn_s0, attn_s1], axis=0)
    hist = state_pool[l]
    halo_s = jnp.concatenate(
        [jnp.zeros((db, POOL_HALO - POOL_STATE, D_POOL), F32), hist], axis=1)
    body_s = upool_s.reshape(db, dseq, D_POOL)
    hs = _merge(hs, attn_s.reshape(db * dseq, D_ATTN), gates_s, halo_s, body_s,
                pl.BlockSpec((db, POOL_HALO, D_POOL), lambda i: (0, 0, 0)),
                pl.BlockSpec((db, dseq, D_POOL), lambda i: (0, 0, 0)),
                POOL_STATE, False, *merge_w)
    y_sample = _ffn(hs, g2pre, g2post, w2gu, w2d).reshape(db, dseq, D_MODEL)
    new_k_s = kf_s.reshape(1, db, dseq, H_DIFF, 2, HEAD_DIM)
    new_v_s = vf_s.reshape(1, db, dseq, H_DIFF, 2 * HEAD_DIM)
    new_pool_s = body_s[:, dseq - POOL_STATE:].reshape(1, db, POOL_STATE, D_POOL)

    return (y_prompt, y_sample, new_k_p, new_v_p, new_pool_p, new_k_s, new_v_s, new_pool_s)
```

```python
import functools
import math

import numpy as np
import jax
import jax.numpy as jnp
from jax import lax
from jax.experimental import pallas as pl
from jax.experimental.pallas import tpu as pltpu

D_MODEL = 1024
D_FF = 2816
D_POOL = 512
POOL_WINDOWS = (2, 4, 8, 16)
POOL_GROUP = 128
POOL_STATE = 15
H_DIFF = 4
HEAD_DIM = 64
D_ATTN = 512
ROT_DIM = 16
ROPE_THETA = 500000.0
CHUNK = 64
EPS = 1e-6
LAM_INIT = 0.8 - 0.6 * math.exp(-0.3 * 0)

V7X_LANES = 128
BF16_SUBLANES = 16
POOL_HALO = 16
ROW_TILE = 512
FF_CHUNKS = ((0, 1024), (1024, 1024), (2048, 768))
ATTN_STRIP = 256
ATTN_HEADS_PER_STEP = 4
ATTN_UNROLL = 4
QK_SCALE_LOG2 = HEAD_DIM ** -0.5 * math.log2(math.e)

F32 = jnp.float32
BF16 = jnp.bfloat16


def _mib(n):
    return int(n * 1024 * 1024)


def _rms(x, g):
    return x * lax.rsqrt(jnp.mean(x * x, axis=-1, keepdims=True) + EPS) * g


def _dot(a, b):
    return jnp.dot(a, b, preferred_element_type=F32)


def _const_spec(shape):
    nd = len(shape)
    return pl.BlockSpec(shape, lambda *_: (0,) * nd, pipeline_mode=pl.Buffered(1))


def _decode_attn_stages(first, last, lam_ref, g_ref, q_ref, ck_ref, cv_ref, kn_ref, vn_ref,
                        o_ref, m_ref, l_ref, acc_ref):
    hd2 = 2 * HEAD_DIM
    keys = ck_ref.shape[2]
    nt_dims = (((1,), (1,)), ((), ()))
    heads = [slice(h * hd2, (h + 1) * hd2) for h in range(H_DIFF)]
    state = {}

    def scores():
        for h, sl in enumerate(heads):
            q = q_ref[0, :, sl]
            comp = lax.broadcasted_iota(jnp.int32, q.shape, 1)
            lhs = jnp.concatenate([jnp.where(comp < HEAD_DIM, q, 0.0),
                                   jnp.where(comp >= HEAD_DIM, q, 0.0)], axis=0).astype(BF16)
            s_c = _dot(lhs, ck_ref[0, sl, :].astype(BF16))
            s_n = lax.dot_general(lhs, kn_ref[0, :, sl].astype(BF16), nt_dims,
                                  preferred_element_type=F32)
            s_n = jnp.where(last, s_n, -jnp.inf)
            m_old = jnp.where(first, -jnp.inf, m_ref[h])
            m_new = jnp.maximum(m_old, jnp.maximum(jnp.max(s_c, axis=1, keepdims=True),
                                                   jnp.max(s_n, axis=1, keepdims=True)))
            alpha = jnp.exp2(m_old - m_new)
            e_c = jnp.exp2(s_c - m_new)
            e_n = jnp.exp2(s_n - m_new)
            l_new = (alpha * jnp.where(first, 0.0, l_ref[h])
                     + jnp.sum(e_c, axis=1, keepdims=True) + jnp.sum(e_n, axis=1, keepdims=True))
            m_ref[h] = m_new
            l_ref[h] = l_new
            state[h] = (alpha, e_c.astype(BF16), e_n.astype(BF16), l_new)

    def outputs():
        lam = _lambda(lam_ref)
        for h, sl in enumerate(heads):
            alpha, e_c, e_n, l_new = state[h]
            v_c = cv_ref[0, pl.ds(h, keys, stride=H_DIFF), :].astype(BF16)
            acc = (alpha * jnp.where(first, 0.0, acc_ref[h])
                   + _dot(e_c, v_c) + _dot(e_n, vn_ref[0, :, sl].astype(BF16)))
            acc_ref[h] = acc
            o = acc * (1.0 / l_new)
            t = o.shape[0] // 2
            o = o[0:t] - lam * o[t:2 * t]
            o_ref[0, :, sl] = _rms(o, g_ref[...]) * (1.0 - LAM_INIT)

    return scores, outputs


def _ffn_kernel(*refs, decode_parts, n_casts):
    x_ref, pre_ref, post_ref, wgu_ref, wd_ref = refs[:5]
    n_attn_in = 7 if decode_parts else 0
    attn_in = refs[5:5 + n_attn_in]
    cast_in = refs[5 + n_attn_in:5 + n_attn_in + n_casts]
    n_out = 1 + (1 if decode_parts else 0)
    outs = refs[5 + n_attn_in + n_casts:]
    o_ref = outs[0]
    cast_out = outs[n_out:n_out + n_casts]
    if decode_parts:
        attn_o_ref = outs[1]
        m_ref, l_ref, acc_ref = outs[n_out + n_casts:]
        step = pl.program_id(0)

        @pl.when(step == 0)
        def _():
            m_ref[...] = jnp.zeros_like(m_ref)
            l_ref[...] = jnp.zeros_like(l_ref)
            acc_ref[...] = jnp.zeros_like(acc_ref)

        part = step % decode_parts
        side = _decode_attn_stages(part == 0, part == decode_parts - 1, *attn_in,
                                   attn_o_ref, m_ref, l_ref, acc_ref)
    else:
        side = ()
    for src, dst in zip(cast_in, cast_out):
        dst[...] = src[...].astype(BF16)
    x = x_ref[...]
    xn = _rms(x, pre_ref[...]).astype(BF16)
    acc = None
    for idx, (c0, cw) in enumerate(FF_CHUNKS):
        gate = _dot(xn, wgu_ref[:, c0:c0 + cw])
        up = _dot(xn, wgu_ref[:, D_FF + c0:D_FF + c0 + cw])
        act = (gate * jax.nn.sigmoid(gate) * up).astype(BF16)
        part_out = _dot(act, wd_ref[c0:c0 + cw, :])
        acc = part_out if acc is None else acc + part_out
        if idx < len(side):
            side[idx]()
    o_ref[...] = x + 0.5 * _rms(acc, post_ref[...])


def _ffn(x, pre_g, post_g, w_gu, w_down, decode=None, casts=()):
    rows = x.shape[0]
    steps = rows // ROW_TILE
    row_spec = pl.BlockSpec((ROW_TILE, D_MODEL), lambda i: (i, 0))
    in_specs = [row_spec, _const_spec((1, D_MODEL)), _const_spec((1, D_MODEL)),
                _const_spec((D_MODEL, 2 * D_FF)), _const_spec((D_FF, D_MODEL))]
    out_specs, out_shape = [row_spec], [jax.ShapeDtypeStruct((rows, D_MODEL), F32)]
    args, scratch, parts = [x, pre_g, post_g, w_gu, w_down], [], 0
    if decode is not None:
        seq0, n_seq, lam, subln_row, q, cache_kt, cache_v4, k_new, v_new = decode
        t, past = q.shape[1], cache_kt.shape[2]
        parts = steps // n_seq
        keys = past // parts
        assert steps == n_seq * parts and past == keys * parts and keys % V7X_LANES == 0
        seq = pl.BlockSpec((1, t, D_ATTN), lambda i: (seq0 + i // parts, 0, 0))
        in_specs += [pl.BlockSpec((4, HEAD_DIM), lambda i: (0, 0)),
                     pl.BlockSpec((1, 2 * HEAD_DIM), lambda i: (0, 0)),
                     seq,
                     pl.BlockSpec((1, D_ATTN, keys), lambda i: (seq0 + i // parts, 0, i % parts)),
                     pl.BlockSpec((1, keys * H_DIFF, 2 * HEAD_DIM),
                                  lambda i: (seq0 + i // parts, i % parts, 0)),
                     seq, seq]
        out_specs.append(pl.BlockSpec((1, t, D_ATTN), lambda i: (i // parts, 0, 0)))
        out_shape.append(jax.ShapeDtypeStruct((n_seq, t, D_ATTN), F32))
        args += [lam, subln_row, q, cache_kt, cache_v4, k_new, v_new]
        scratch = [pltpu.VMEM((H_DIFF, 2 * t, 1), F32), pltpu.VMEM((H_DIFF, 2 * t, 1), F32),
                   pltpu.VMEM((H_DIFF, 2 * t, 2 * HEAD_DIM), F32)]
    for w in casts:
        per_step = w.shape[0] // steps
        span = BF16_SUBLANES // math.gcd(per_step, BF16_SUBLANES)
        assert w.shape[0] == per_step * steps and steps % span == 0
        spec = pl.BlockSpec((per_step * span, w.shape[1]), lambda i, span=span: (i // span, 0))
        in_specs.append(spec)
        out_specs.append(spec)
        out_shape.append(jax.ShapeDtypeStruct(w.shape, BF16))
        args.append(w)
    outs = pl.pallas_call(
        functools.partial(_ffn_kernel, decode_parts=parts, n_casts=len(casts)),
        grid=(steps,),
        in_specs=in_specs,
        out_specs=out_specs,
        out_shape=out_shape,
        scratch_shapes=scratch,
        compiler_params=pltpu.CompilerParams(
            dimension_semantics=("arbitrary",), vmem_limit_bytes=_mib(56 if parts else 48)),
        name="ffn_riders" if (parts or casts) else "ffn",
    )(*args)
    return outs[0] if len(outs) == 1 else outs


def _rope_tables(tile_pos0, row_pos):
    d = np.arange(V7X_LANES) % HEAD_DIM
    inv = np.where(d < ROT_DIM, ROPE_THETA ** (-(d % (ROT_DIM // 2)) / (ROT_DIM // 2)), 0.0)
    ang_a = np.asarray(tile_pos0, np.float64)[:, None] * inv[None, :]
    ang_b = np.asarray(row_pos, np.float64)[:, None] * inv[None, :]
    half = ROT_DIM // 2
    sel_lo = np.where(d < half, -1.0, 0.0)[None, :]
    sel_hi = np.where((d >= half) & (d < ROT_DIM), 1.0, 0.0)[None, :]
    f = lambda a: jnp.asarray(a, F32)
    return (f(np.cos(ang_a))[:, None, :], f(np.sin(ang_a))[:, None, :],
            f(np.cos(ang_b)), f(np.sin(ang_b)), f(sel_lo), f(sel_hi))


def _mix_in_kernel(h_ref, g_ref, win_ref, cos_a_ref, sin_a_ref, cos_b_ref, sin_b_ref,
                   sel_lo_ref, sel_hi_ref,
                   upool_ref, kf_ref, vf_ref, gates_ref, q_ref, kb_ref, v_ref, *, transposed):
    u = _rms(h_ref[...], g_ref[...]).astype(BF16)
    upool_ref[...] = _dot(u, win_ref[:, 0:D_POOL])

    ca, sa = cos_a_ref[0], sin_a_ref[0]
    cb, sb = cos_b_ref[...], sin_b_ref[...]
    cos = ca * cb - sa * sb
    sin = sa * cb + ca * sb
    rep = D_ATTN // V7X_LANES
    cos_f = jnp.concatenate([cos] * rep, axis=1)
    sin_lo = jnp.concatenate([sin * sel_lo_ref[...]] * rep, axis=1)
    sin_hi = jnp.concatenate([sin * sel_hi_ref[...]] * rep, axis=1)
    half = ROT_DIM // 2

    def rope(x):
        return (x * cos_f + pltpu.roll(x, D_ATTN - half, 1) * sin_lo
                + pltpu.roll(x, half, 1) * sin_hi)

    q = rope(_dot(u, win_ref[:, D_POOL:D_POOL + D_ATTN])) * QK_SCALE_LOG2
    k = rope(_dot(u, win_ref[:, D_POOL + D_ATTN:D_POOL + 2 * D_ATTN]))
    v = _dot(u, win_ref[:, D_POOL + 2 * D_ATTN:D_POOL + 3 * D_ATTN])
    kf_ref[...] = k
    vf_ref[...] = v.reshape(v.shape[0], H_DIFF, 2 * HEAD_DIM)
    for hh in range(H_DIFF):
        kb_ref[hh] = k[:, hh * 2 * HEAD_DIM:(hh + 1) * 2 * HEAD_DIM].astype(BF16)
    if transposed:
        comp = lax.broadcasted_iota(jnp.int32, q.shape, 1) % (2 * HEAD_DIM)
        q_top = jnp.where(comp < HEAD_DIM, q, 0.0)
        q_bot = jnp.where(comp >= HEAD_DIM, q, 0.0)
        q_ref[0, :, 0:ROW_TILE] = q_top.T.astype(BF16)
        q_ref[0, :, ROW_TILE:2 * ROW_TILE] = q_bot.T.astype(BF16)
        v_ref[0] = v.T.astype(BF16)
    else:
        q_ref[...] = q
        v_ref[...] = v.astype(BF16)

    g0 = D_POOL + 3 * D_ATTN
    for c in range(2):
        z = _dot(u, win_ref[:, g0 + c * D_MODEL:g0 + (c + 1) * D_MODEL])
        gates_ref[:, c * D_MODEL:(c + 1) * D_MODEL] = jax.nn.sigmoid(z).astype(BF16)


def _mix_in(h, mix_pre_g, w_in, tile_pos0, row_pos, transposed):
    rows = h.shape[0]
    nt = rows // ROW_TILE
    tables = _rope_tables(tile_pos0, row_pos)
    row = lambda w: pl.BlockSpec((ROW_TILE, w), lambda i: (i, 0))
    if transposed:
        q_shape = jax.ShapeDtypeStruct((nt, D_ATTN, 2 * ROW_TILE), BF16)
        q_spec = pl.BlockSpec((1, D_ATTN, 2 * ROW_TILE), lambda i: (i, 0, 0))
        v_shape = jax.ShapeDtypeStruct((nt, D_ATTN, ROW_TILE), BF16)
        v_spec = pl.BlockSpec((1, D_ATTN, ROW_TILE), lambda i: (i, 0, 0))
    else:
        q_shape, q_spec = jax.ShapeDtypeStruct((rows, D_ATTN), F32), row(D_ATTN)
        v_shape, v_spec = jax.ShapeDtypeStruct((rows, D_ATTN), BF16), row(D_ATTN)
    return pl.pallas_call(
        functools.partial(_mix_in_kernel, transposed=transposed),
        grid=(nt,),
        in_specs=[row(D_MODEL), _const_spec((1, D_MODEL)), _const_spec(w_in.shape),
                  pl.BlockSpec((1, 1, V7X_LANES), lambda i: (i, 0, 0)),
                  pl.BlockSpec((1, 1, V7X_LANES), lambda i: (i, 0, 0)),
                  _const_spec((ROW_TILE, V7X_LANES)), _const_spec((ROW_TILE, V7X_LANES)),
                  _const_spec((1, V7X_LANES)), _const_spec((1, V7X_LANES))],
        out_specs=[row(D_POOL), row(D_ATTN),
                   pl.BlockSpec((ROW_TILE, H_DIFF, 2 * HEAD_DIM), lambda i: (i, 0, 0)),
                   row(2 * D_MODEL),
                   q_spec,
                   pl.BlockSpec((H_DIFF, ROW_TILE, 2 * HEAD_DIM), lambda i: (0, i, 0)),
                   v_spec],
        out_shape=[jax.ShapeDtypeStruct((rows, D_POOL), F32),
                   jax.ShapeDtypeStruct((rows, D_ATTN), F32),
                   jax.ShapeDtypeStruct((rows, H_DIFF, 2 * HEAD_DIM), F32),
                   jax.ShapeDtypeStruct((rows, 2 * D_MODEL), BF16),
                   q_shape,
                   jax.ShapeDtypeStruct((H_DIFF, rows, 2 * HEAD_DIM), BF16),
                   v_shape],
        compiler_params=pltpu.CompilerParams(
            dimension_semantics=("arbitrary",), vmem_limit_bytes=_mib(48)),
        name="mix_in",
    )(h, mix_pre_g, w_in, *tables)


def _lambda(lam_ref):
    lq1, lk1, lq2, lk2 = (lam_ref[i:i + 1, :] for i in range(4))
    return (jnp.exp(jnp.sum(lq1 * lk1, axis=1, keepdims=True))
            - jnp.exp(jnp.sum(lq2 * lk2, axis=1, keepdims=True)) + LAM_INIT)


def _attn_prompt_kernel(lam_ref, g_ref, q_ref, qn_ref, k_ref, v_ref, o_ref, acc_ref, s_ref, smax_ref):
    tb = ROW_TILE
    hd2 = 2 * HEAD_DIM
    i = pl.program_id(1)
    acc_ref[...] = jnp.zeros_like(acc_ref)
    heads = [slice(h * hd2, (h + 1) * hd2) for h in range(ATTN_HEADS_PER_STEP)]
    strips = [slice(c, c + ATTN_STRIP) for c in range(0, 2 * tb, ATTN_STRIP)]

    def logits(j, rows, cols, q_blk):
        kj = k_ref[rows.start // hd2, pl.ds(pl.multiple_of(j * tb, tb), tb), :]
        return _dot(kj, q_blk[0, rows, cols])

    def diagonal_weights(h, n, q0, keys, m_old):
        lanes = V7X_LANES
        upper = lax.broadcasted_iota(jnp.int32, (CHUNK, lanes), 1) >= CHUNK
        m_new, e_cols = [], []
        for lt in range(ATTN_STRIP // lanes):
            lane_cols = slice(lt * lanes, (lt + 1) * lanes)
            full = ((q0 + lt * lanes) // CHUNK + 1) * CHUNK
            half = s_ref[h, n, full:full + CHUNK, lane_cols]
            s = jnp.concatenate([s_ref[h, n, 0:full, lane_cols],
                                 jnp.where(upper, half, -jnp.inf)], axis=0)
            m_lt = jnp.maximum(m_old[:, lane_cols], jnp.max(s, axis=0, keepdims=True))
            e = jnp.exp2(s - m_lt)
            m_new.append(m_lt)
            hidden = keys - full - CHUNK
            e_cols.append(e if hidden == 0 else
                          jnp.concatenate([e, jnp.zeros((hidden, lanes), F32)], axis=0))
        return jnp.concatenate(m_new, axis=1), jnp.concatenate(e_cols, axis=1)

    def consume(j, m, l, smax, diagonal, next_logits):
        m = [list(x) for x in m]
        l = [list(x) for x in l]
        smax = [list(x) for x in smax]
        for n, cols in enumerate(strips):
            q0 = cols.start % tb
            keys = q0 + ATTN_STRIP if diagonal else tb
            for h, rows in enumerate(heads):
                s_next = next_logits(rows, cols)
                if diagonal:
                    m_new, e = diagonal_weights(h, n, q0, keys, m[h][n])
                    alpha = jnp.exp2(m[h][n] - m_new)
                else:
                    s = s_ref[h, n, 0:keys, :]
                    m_new = jnp.maximum(m[h][n], smax[h][n])
                    alpha = jnp.exp2(m[h][n] - m_new)
                    e = jnp.exp2(s - m_new)
                pv = _dot(v_ref[j, rows, 0:keys], e.astype(BF16))
                l[h][n] = alpha * l[h][n] + jnp.sum(e, axis=0, keepdims=True)
                m[h][n] = m_new
                acc_ref[h, n] = acc_ref[h, n] * alpha + pv
                s_ref[h, n] = s_next
                smax[h][n] = jnp.max(s_next, axis=0, keepdims=True)
        return m, l, smax

    @pl.when(i == 0)
    def _():
        for h, rows in enumerate(heads):
            for n, cols in enumerate(strips):
                s0 = logits(0, rows, cols, q_ref)
                s_ref[h, n] = s0
                smax_ref[h, n] = jnp.max(s0, axis=0, keepdims=True)

    m0 = [[jnp.full((1, ATTN_STRIP), -jnp.inf, F32) for _ in strips] for _ in heads]
    l0 = [[jnp.zeros((1, ATTN_STRIP), F32) for _ in strips] for _ in heads]
    smax0 = [[smax_ref[h, n] for n in range(len(strips))] for h in range(len(heads))]

    def full_block(j, state):
        return consume(j, *state, False, lambda r, c: logits(j + 1, r, c, q_ref))

    def group(t, st):
        for u in range(ATTN_UNROLL):
            st = full_block(ATTN_UNROLL * t + u, st)
        return st

    state = lax.fori_loop(0, i // ATTN_UNROLL, group, (m0, l0, smax0))
    state = lax.fori_loop(i - i % ATTN_UNROLL, i, full_block, state)
    m, l, smax = consume(i, *state, True, lambda r, c: logits(0, r, c, qn_ref))
    for h in range(len(heads)):
        for n in range(len(strips)):
            smax_ref[h, n] = smax[h][n]

    lam = _lambda(lam_ref)
    for h, rows in enumerate(heads):
        o = jnp.concatenate([acc_ref[h, n] * (1.0 / l[h][n]) for n in range(len(strips))], axis=1)
        o = o[:, 0:tb] - lam * o[:, tb:2 * tb]
        ms = jnp.mean(o * o, axis=0, keepdims=True)
        o_ref[rows, :] = o * lax.rsqrt(ms + EPS) * g_ref[...] * (1.0 - LAM_INIT)


def _attn_prompt(lam, subln_col, q_t, k_b, v_t):
    nt = q_t.shape[0]
    rows = nt * ROW_TILE
    hd2 = 2 * HEAD_DIM
    gw = ATTN_HEADS_PER_STEP * hd2
    n_strips = 2 * ROW_TILE // ATTN_STRIP
    resident = dict(pipeline_mode=pl.Buffered(1))
    return pl.pallas_call(
        _attn_prompt_kernel,
        grid=(H_DIFF // ATTN_HEADS_PER_STEP, nt),
        in_specs=[pl.BlockSpec((4, HEAD_DIM), lambda h, i: (0, 0)),
                  pl.BlockSpec((hd2, 1), lambda h, i: (0, 0)),
                  pl.BlockSpec((1, gw, 2 * ROW_TILE), lambda h, i: (i, h, 0)),
                  pl.BlockSpec((1, gw, 2 * ROW_TILE), lambda h, i: (jnp.minimum(i + 1, nt - 1), h, 0)),
                  pl.BlockSpec((ATTN_HEADS_PER_STEP, rows, hd2), lambda h, i: (h, 0, 0), **resident),
                  pl.BlockSpec((nt, gw, ROW_TILE), lambda h, i: (0, h, 0), **resident)],
        out_specs=pl.BlockSpec((gw, ROW_TILE), lambda h, i: (h, i)),
        out_shape=jax.ShapeDtypeStruct((D_ATTN, rows), F32),
        scratch_shapes=[pltpu.VMEM((ATTN_HEADS_PER_STEP, n_strips, hd2, ATTN_STRIP), F32),
                        pltpu.VMEM((ATTN_HEADS_PER_STEP, n_strips, ROW_TILE, ATTN_STRIP), F32),
                        pltpu.VMEM((ATTN_HEADS_PER_STEP, n_strips, 1, ATTN_STRIP), F32)],
        compiler_params=pltpu.CompilerParams(
            dimension_semantics=("arbitrary", "arbitrary"), vmem_limit_bytes=_mib(56)),
        name="attn_prompt",
    )(lam, subln_col, q_t, q_t, k_b, v_t)


def _merge_kernel(h_ref, attn_ref, gates_ref, halo_ref, body_ref, poolw_ref, pscale_ref,
                  wbp_ref, wba_ref, wout_ref, postg_ref, o_ref, *, hist0, attn_transposed):
    nb, tl, _ = body_ref.shape
    hist = hist0 + pl.program_id(0) * tl
    hi = hist + 1 + lax.broadcasted_iota(jnp.int32, (tl, 1), 0)
    halo_row = lax.broadcasted_iota(jnp.int32, (POOL_HALO, 1), 0)
    halo_ok = halo_row >= POOL_HALO - jnp.minimum(hist, POOL_STATE)
    pooled = []
    for b in range(nb):
        halo = jnp.where(halo_ok, halo_ref[b], 0.0)
        ext = jnp.concatenate([halo, body_ref[b]], axis=0)
        groups = []
        for g, win in enumerate(POOL_WINDOWS):
            x = ext[:, g * POOL_GROUP:(g + 1) * POOL_GROUP]
            s, shift = x, 1
            while shift < win:
                s = s + pltpu.roll(s, shift, 0)
                shift *= 2
            inv_cnt = 1.0 / jnp.minimum(hi, win).astype(F32)
            mixed = s[POOL_HALO:] * inv_cnt - x[POOL_HALO:]
            groups.append(_dot(mixed.astype(BF16), poolw_ref[g]))
        pooled.append(jnp.concatenate(groups, axis=1) * pscale_ref[...])
    pool_out = pooled[0] if nb == 1 else jnp.concatenate(pooled, axis=0)

    attn = attn_ref[...].T if attn_transposed else attn_ref[...]
    branch_pool = _dot(pool_out.astype(BF16), wbp_ref[...])
    branch_attn = _dot(attn.astype(BF16), wba_ref[...])
    merged = (gates_ref[:, 0:D_MODEL].astype(F32) * branch_pool
              + gates_ref[:, D_MODEL:2 * D_MODEL].astype(F32) * branch_attn)
    o_ref[...] = h_ref[...] + _rms(_dot(merged.astype(BF16), wout_ref[...]), postg_ref[...])


def _merge(h, attn, gates, halo, body, halo_spec, body_spec, hist0, attn_transposed,
           pool_w, pool_scale, w_bp, w_ba, w_out, post_g):
    rows = h.shape[0]
    row = lambda w: pl.BlockSpec((ROW_TILE, w), lambda i: (i, 0))
    attn_spec = (pl.BlockSpec((D_ATTN, ROW_TILE), lambda i: (0, i)) if attn_transposed
                 else row(D_ATTN))
    return pl.pallas_call(
        functools.partial(_merge_kernel, hist0=hist0, attn_transposed=attn_transposed),
        grid=(rows // ROW_TILE,),
        in_specs=[row(D_MODEL), attn_spec, row(2 * D_MODEL), halo_spec, body_spec,
                  _const_spec(pool_w.shape), _const_spec(pool_scale.shape),
                  _const_spec(w_bp.shape), _const_spec(w_ba.shape), _const_spec(w_out.shape),
                  _const_spec(post_g.shape)],
        out_specs=row(D_MODEL),
        out_shape=jax.ShapeDtypeStruct((rows, D_MODEL), F32),
        compiler_params=pltpu.CompilerParams(
            dimension_semantics=("arbitrary",), vmem_limit_bytes=_mib(48)),
        name="merge",
    )(h, attn, gates, halo, body, pool_w, pool_scale, w_bp, w_ba, w_out, post_g)


def kernel(x_prompt, x_sample, cache_k, cache_v, state_pool, ffn1_pre_g, ffn1_post_g, ffn1_w_gu, ffn1_w_down, mix_pre_g, mix_post_g, w_in, pool_w, pool_scale, lambda_q1, lambda_k1, lambda_q2, lambda_k2, subln_g, w_branch_pool, w_branch_attn, w_out, ffn2_pre_g, ffn2_post_g, ffn2_w_gu, ffn2_w_down):
    bp, seq, _ = x_prompt.shape
    db, dseq, _ = x_sample.shape
    past = cache_k.shape[2]
    assert bp == 1 and seq % ROW_TILE == 0 and db * dseq == ROW_TILE and dseq >= POOL_STATE
    assert (past + dseq - 1) // CHUNK <= past // CHUNK and past % CHUNK == 0

    l = 0
    bf = lambda w: w[l].astype(BF16)
    g1pre, g1post, gmpre, gmpost = ffn1_pre_g[l:l + 1], ffn1_post_g[l:l + 1], mix_pre_g[l:l + 1], mix_post_g[l:l + 1]
    g2pre, g2post = ffn2_pre_g[l:l + 1], ffn2_post_g[l:l + 1]
    w1gu, w1d, win = bf(ffn1_w_gu), bf(ffn1_w_down), bf(w_in)
    later_w = (ffn2_w_gu[l], ffn2_w_down[l], pool_w[l].reshape(D_POOL, POOL_GROUP),
               w_branch_pool[l], w_branch_attn[l], w_out[l])
    pscale = pool_scale[l:l + 1]
    lam = jnp.concatenate([lambda_q1[l:l + 1], lambda_k1[l:l + 1],
                           lambda_q2[l:l + 1], lambda_k2[l:l + 1]], axis=0)
    subln_row = subln_g[l:l + 1]
    subln_col = subln_g[l][:, None]

    xs = x_sample.reshape(db * dseq, D_MODEL)
    hs = _ffn(xs, g1pre, g1post, w1gu, w1d)
    upool_s, kf_s, vf_s, gates_s, q_s, _, _ = _mix_in(
        hs, gmpre, win, np.array([past]), np.arange(ROW_TILE) % dseq, transposed=False)
    cache_kt = jnp.transpose(cache_k[l], (0, 2, 3, 4, 1)).reshape(db, D_ATTN, past)
    cache_v4 = cache_v[l].reshape(db, past * H_DIFF, 2 * HEAD_DIM)
    decode_attn = lambda seq0: (seq0, db // 2, lam, subln_row, q_s.reshape(db, dseq, D_ATTN),
                                cache_kt, cache_v4, kf_s.reshape(db, dseq, D_ATTN),
                                vf_s.reshape(db, dseq, D_ATTN))

    xp = x_prompt.reshape(seq, D_MODEL)
    nt = seq // ROW_TILE
    hp, attn_s0, w2gu, w2d, wpool, wbp, wba, wo = _ffn(
        xp, g1pre, g1post, w1gu, w1d, decode_attn(0), casts=later_w)
    merge_w = (wpool.reshape(len(POOL_WINDOWS), POOL_GROUP, POOL_GROUP), pscale, wbp, wba, wo, gmpost)
    upool, kf, vf, gates, q_t, k_b, v_t = _mix_in(
        hp, gmpre, win, np.arange(nt) * ROW_TILE, np.arange(ROW_TILE), transposed=True)
    attn_t = _attn_prompt(lam, subln_col, q_t, k_b, v_t)
    per_tile = ROW_TILE // POOL_HALO
    halo_spec = pl.BlockSpec((1, POOL_HALO, D_POOL),
                             lambda i: (jnp.maximum(i * per_tile - 1, 0), 0, 0))
    body_spec = pl.BlockSpec((1, ROW_TILE, D_POOL), lambda i: (i, 0, 0))
    hp = _merge(hp, attn_t, gates, upool.reshape(seq // POOL_HALO, POOL_HALO, D_POOL),
                upool.reshape(nt, ROW_TILE, D_POOL), halo_spec, body_spec, 0, True, *merge_w)
    y_prompt, attn_s1 = _ffn(hp, g2pre, g2post, w2gu, w2d, decode_attn(db // 2))
    y_prompt = y_prompt.reshape(bp, seq, D_MODEL)
    new_k_p = kf.reshape(1, bp, seq, H_DIFF, 2, HEAD_DIM)
    new_v_p = vf.reshape(1, bp, seq, H_DIFF, 2 * HEAD_DIM)
    new_pool_p = upool[seq - POOL_STATE:].reshape(1, bp, POOL_STATE, D_POOL)

    attn_s = jnp.concatenate([attn_s0, attn_s1], axis=0)
    hist = state_pool[l]
    halo_s = jnp.concatenate(
        [jnp.zeros((db, POOL_HALO - POOL_STATE, D_POOL), F32), hist], axis=1)
    body_s = upool_s.reshape(db, dseq, D_POOL)
    hs = _merge(hs, attn_s.reshape(db * dseq, D_ATTN), gates_s, halo_s, body_s,
                pl.BlockSpec((db, POOL_HALO, D_POOL), lambda i: (0, 0, 0)),
                pl.BlockSpec((db, dseq, D_POOL), lambda i: (0, 0, 0)),
                POOL_STATE, False, *merge_w)
    y_sample = _ffn(hs, g2pre, g2post, w2gu, w2d).reshape(db, dseq, D_MODEL)
    new_k_s = kf_s.reshape(1, db, dseq, H_DIFF, 2, HEAD_DIM)
    new_v_s = vf_s.reshape(1, db, dseq, H_DIFF, 2 * HEAD_DIM)
    new_pool_s = body_s[:, dseq - POOL_STATE:].reshape(1, db, POOL_STATE, D_POOL)

    return (y_prompt, y_sample, new_k_p, new_v_p, new_pool_p, new_k_s, new_v_s, new_pool_s)
```

```python
import functools
import math

import numpy as np
import jax
import jax.numpy as jnp
from jax import lax
from jax.experimental import pallas as pl
from jax.experimental.pallas import tpu as pltpu

D_MODEL = 1024
D_FF = 2816
D_POOL = 512
POOL_WINDOWS = (2, 4, 8, 16)
POOL_GROUP = 128
POOL_STATE = 15
H_DIFF = 4
HEAD_DIM = 64
D_ATTN = 512
ROT_DIM = 16
ROPE_THETA = 500000.0
CHUNK = 64
EPS = 1e-6
LAM_INIT = 0.8 - 0.6 * math.exp(-0.3 * 0)

V7X_LANES = 128
BF16_SUBLANES = 16
POOL_HALO = 16
ROW_TILE = 512
FF_CHUNKS = tuple((c, 256) for c in range(0, D_FF, 256))
ATTN_STRIP = 256
ATTN_HEADS_PER_STEP = 4
ATTN_UNROLL = 4
QK_SCALE_LOG2 = HEAD_DIM ** -0.5 * math.log2(math.e)

F32 = jnp.float32
BF16 = jnp.bfloat16


def _mib(n):
    return int(n * 1024 * 1024)


def _rms(x, g):
    return x * lax.rsqrt(jnp.mean(x * x, axis=-1, keepdims=True) + EPS) * g


def _dot(a, b):
    return jnp.dot(a, b, preferred_element_type=F32)


def _const_spec(shape):
    nd = len(shape)
    return pl.BlockSpec(shape, lambda *_: (0,) * nd, pipeline_mode=pl.Buffered(1))


def _decode_attn_stages(first, last, lam_ref, g_ref, q_ref, ck_ref, cv_ref, kn_ref, vn_ref,
                        o_ref, m_ref, l_ref, acc_ref):
    hd2 = 2 * HEAD_DIM
    keys = ck_ref.shape[2]
    nt_dims = (((1,), (1,)), ((), ()))
    heads = [slice(h * hd2, (h + 1) * hd2) for h in range(H_DIFF)]
    state = {}

    def scores():
        for h, sl in enumerate(heads):
            q = q_ref[0, :, sl]
            comp = lax.broadcasted_iota(jnp.int32, q.shape, 1)
            lhs = jnp.concatenate([jnp.where(comp < HEAD_DIM, q, 0.0),
                                   jnp.where(comp >= HEAD_DIM, q, 0.0)], axis=0).astype(BF16)
            s_c = _dot(lhs, ck_ref[0, sl, :].astype(BF16))
            s_n = lax.dot_general(lhs, kn_ref[0, :, sl].astype(BF16), nt_dims,
                                  preferred_element_type=F32)
            s_n = jnp.where(last, s_n, -jnp.inf)
            m_old = jnp.where(first, -jnp.inf, m_ref[h])
            m_new = jnp.maximum(m_old, jnp.maximum(jnp.max(s_c, axis=1, keepdims=True),
                                                   jnp.max(s_n, axis=1, keepdims=True)))
            alpha = jnp.exp2(m_old - m_new)
            e_c = jnp.exp2(s_c - m_new)
            e_n = jnp.exp2(s_n - m_new)
            l_new = (alpha * jnp.where(first, 0.0, l_ref[h])
                     + jnp.sum(e_c, axis=1, keepdims=True) + jnp.sum(e_n, axis=1, keepdims=True))
            m_ref[h] = m_new
            l_ref[h] = l_new
            state[h] = (alpha, e_c.astype(BF16), e_n.astype(BF16), l_new)

    def outputs():
        lam = _lambda(lam_ref)
        for h, sl in enumerate(heads):
            alpha, e_c, e_n, l_new = state[h]
            v_c = cv_ref[0, pl.ds(h, keys, stride=H_DIFF), :].astype(BF16)
            acc = (alpha * jnp.where(first, 0.0, acc_ref[h])
                   + _dot(e_c, v_c) + _dot(e_n, vn_ref[0, :, sl].astype(BF16)))
            acc_ref[h] = acc
            o = acc * (1.0 / l_new)
            t = o.shape[0] // 2
            o = o[0:t] - lam * o[t:2 * t]
            o_ref[0, :, sl] = _rms(o, g_ref[...]) * (1.0 - LAM_INIT)

    return scores, outputs


def _ffn_kernel(*refs, decode_parts, n_casts):
    x_ref, pre_ref, post_ref, wgu_ref, wd_ref = refs[:5]
    n_attn_in = 7 if decode_parts else 0
    attn_in = refs[5:5 + n_attn_in]
    cast_in = refs[5 + n_attn_in:5 + n_attn_in + n_casts]
    n_out = 1 + (1 if decode_parts else 0)
    outs = refs[5 + n_attn_in + n_casts:]
    o_ref = outs[0]
    cast_out = outs[n_out:n_out + n_casts]
    if decode_parts:
        attn_o_ref = outs[1]
        m_ref, l_ref, acc_ref = outs[n_out + n_casts:]
        step = pl.program_id(0)

        @pl.when(step == 0)
        def _():
            m_ref[...] = jnp.zeros_like(m_ref)
            l_ref[...] = jnp.zeros_like(l_ref)
            acc_ref[...] = jnp.zeros_like(acc_ref)

        part = step % decode_parts
        side = _decode_attn_stages(part == 0, part == decode_parts - 1, *attn_in,
                                   attn_o_ref, m_ref, l_ref, acc_ref)
    else:
        side = ()
    for src, dst in zip(cast_in, cast_out):
        dst[...] = src[...].astype(BF16)
    x = x_ref[...]
    xn = _rms(x, pre_ref[...]).astype(BF16)
    acc = None
    for idx, (c0, cw) in enumerate(FF_CHUNKS):
        gate = _dot(xn, wgu_ref[:, c0:c0 + cw])
        up = _dot(xn, wgu_ref[:, D_FF + c0:D_FF + c0 + cw])
        act = (gate * jax.nn.sigmoid(gate) * up).astype(BF16)
        part_out = _dot(act, wd_ref[c0:c0 + cw, :])
        acc = part_out if acc is None else acc + part_out
        if idx < len(side):
            side[idx]()
    o_ref[...] = x + 0.5 * _rms(acc, post_ref[...])


def _ffn(x, pre_g, post_g, w_gu, w_down, decode=None, casts=()):
    rows = x.shape[0]
    steps = rows // ROW_TILE
    row_spec = pl.BlockSpec((ROW_TILE, D_MODEL), lambda i: (i, 0))
    in_specs = [row_spec, _const_spec((1, D_MODEL)), _const_spec((1, D_MODEL)),
                _const_spec((D_MODEL, 2 * D_FF)), _const_spec((D_FF, D_MODEL))]
    out_specs, out_shape = [row_spec], [jax.ShapeDtypeStruct((rows, D_MODEL), F32)]
    args, scratch, parts = [x, pre_g, post_g, w_gu, w_down], [], 0
    if decode is not None:
        seq0, n_seq, lam, subln_row, q, cache_kt, cache_v4, k_new, v_new = decode
        t, past = q.shape[1], cache_kt.shape[2]
        parts = steps // n_seq
        keys = past // parts
        assert steps == n_seq * parts and past == keys * parts and keys % V7X_LANES == 0
        seq = pl.BlockSpec((1, t, D_ATTN), lambda i: (seq0 + i // parts, 0, 0))
        in_specs += [pl.BlockSpec((4, HEAD_DIM), lambda i: (0, 0)),
                     pl.BlockSpec((1, 2 * HEAD_DIM), lambda i: (0, 0)),
                     seq,
                     pl.BlockSpec((1, D_ATTN, keys), lambda i: (seq0 + i // parts, 0, i % parts)),
                     pl.BlockSpec((1, keys * H_DIFF, 2 * HEAD_DIM),
                                  lambda i: (seq0 + i // parts, i % parts, 0)),
                     seq, seq]
        out_specs.append(pl.BlockSpec((1, t, D_ATTN), lambda i: (i // parts, 0, 0)))
        out_shape.append(jax.ShapeDtypeStruct((n_seq, t, D_ATTN), F32))
        args += [lam, subln_row, q, cache_kt, cache_v4, k_new, v_new]
        scratch = [pltpu.VMEM((H_DIFF, 2 * t, 1), F32), pltpu.VMEM((H_DIFF, 2 * t, 1), F32),
                   pltpu.VMEM((H_DIFF, 2 * t, 2 * HEAD_DIM), F32)]
    for w in casts:
        per_step = w.shape[0] // steps
        span = BF16_SUBLANES // math.gcd(per_step, BF16_SUBLANES)
        assert w.shape[0] == per_step * steps and steps % span == 0
        spec = pl.BlockSpec((per_step * span, w.shape[1]), lambda i, span=span: (i // span, 0))
        in_specs.append(spec)
        out_specs.append(spec)
        out_shape.append(jax.ShapeDtypeStruct(w.shape, BF16))
        args.append(w)
    outs = pl.pallas_call(
        functools.partial(_ffn_kernel, decode_parts=parts, n_casts=len(casts)),
        grid=(steps,),
        in_specs=in_specs,
        out_specs=out_specs,
        out_shape=out_shape,
        scratch_shapes=scratch,
        compiler_params=pltpu.CompilerParams(
            dimension_semantics=("arbitrary",), vmem_limit_bytes=_mib(56 if parts else 48)),
        name="ffn_riders" if (parts or casts) else "ffn",
    )(*args)
    return outs[0] if len(outs) == 1 else outs


def _rope_tables(tile_pos0, row_pos):
    d = np.arange(V7X_LANES) % HEAD_DIM
    inv = np.where(d < ROT_DIM, ROPE_THETA ** (-(d % (ROT_DIM // 2)) / (ROT_DIM // 2)), 0.0)
    ang_a = np.asarray(tile_pos0, np.float64)[:, None] * inv[None, :]
    ang_b = np.asarray(row_pos, np.float64)[:, None] * inv[None, :]
    half = ROT_DIM // 2
    sel_lo = np.where(d < half, -1.0, 0.0)[None, :]
    sel_hi = np.where((d >= half) & (d < ROT_DIM), 1.0, 0.0)[None, :]
    f = lambda a: jnp.asarray(a, F32)
    return (f(np.cos(ang_a))[:, None, :], f(np.sin(ang_a))[:, None, :],
            f(np.cos(ang_b)), f(np.sin(ang_b)), f(sel_lo), f(sel_hi))


def _mix_in_kernel(h_ref, g_ref, win_ref, cos_a_ref, sin_a_ref, cos_b_ref, sin_b_ref,
                   sel_lo_ref, sel_hi_ref,
                   upool_ref, kf_ref, vf_ref, gates_ref, q_ref, kb_ref, v_ref, *, transposed):
    u = _rms(h_ref[...], g_ref[...]).astype(BF16)
    upool_ref[...] = _dot(u, win_ref[:, 0:D_POOL])

    ca, sa = cos_a_ref[0], sin_a_ref[0]
    cb, sb = cos_b_ref[...], sin_b_ref[...]
    cos = ca * cb - sa * sb
    sin = sa * cb + ca * sb
    rep = D_ATTN // V7X_LANES
    cos_f = jnp.concatenate([cos] * rep, axis=1)
    sin_lo = jnp.concatenate([sin * sel_lo_ref[...]] * rep, axis=1)
    sin_hi = jnp.concatenate([sin * sel_hi_ref[...]] * rep, axis=1)
    half = ROT_DIM // 2

    def rope(x):
        return (x * cos_f + pltpu.roll(x, D_ATTN - half, 1) * sin_lo
                + pltpu.roll(x, half, 1) * sin_hi)

    q = rope(_dot(u, win_ref[:, D_POOL:D_POOL + D_ATTN])) * QK_SCALE_LOG2
    k = rope(_dot(u, win_ref[:, D_POOL + D_ATTN:D_POOL + 2 * D_ATTN]))
    v = _dot(u, win_ref[:, D_POOL + 2 * D_ATTN:D_POOL + 3 * D_ATTN])
    kf_ref[...] = k
    vf_ref[...] = v.reshape(v.shape[0], H_DIFF, 2 * HEAD_DIM)
    for hh in range(H_DIFF):
        kb_ref[hh] = k[:, hh * 2 * HEAD_DIM:(hh + 1) * 2 * HEAD_DIM].astype(BF16)
    if transposed:
        comp = lax.broadcasted_iota(jnp.int32, q.shape, 1) % (2 * HEAD_DIM)
        q_top = jnp.where(comp < HEAD_DIM, q, 0.0)
        q_bot = jnp.where(comp >= HEAD_DIM, q, 0.0)
        q_ref[0, :, 0:ROW_TILE] = q_top.T.astype(BF16)
        q_ref[0, :, ROW_TILE:2 * ROW_TILE] = q_bot.T.astype(BF16)
        v_ref[0] = v.T.astype(BF16)
    else:
        q_ref[...] = q
        v_ref[...] = v.astype(BF16)

    g0 = D_POOL + 3 * D_ATTN
    for c in range(2):
        z = _dot(u, win_ref[:, g0 + c * D_MODEL:g0 + (c + 1) * D_MODEL])
        gates_ref[:, c * D_MODEL:(c + 1) * D_MODEL] = jax.nn.sigmoid(z).astype(BF16)


def _mix_in(h, mix_pre_g, w_in, tile_pos0, row_pos, transposed):
    rows = h.shape[0]
    nt = rows // ROW_TILE
    tables = _rope_tables(tile_pos0, row_pos)
    row = lambda w: pl.BlockSpec((ROW_TILE, w), lambda i: (i, 0))
    if transposed:
        q_shape = jax.ShapeDtypeStruct((nt, D_ATTN, 2 * ROW_TILE), BF16)
        q_spec = pl.BlockSpec((1, D_ATTN, 2 * ROW_TILE), lambda i: (i, 0, 0))
        v_shape = jax.ShapeDtypeStruct((nt, D_ATTN, ROW_TILE), BF16)
        v_spec = pl.BlockSpec((1, D_ATTN, ROW_TILE), lambda i: (i, 0, 0))
    else:
        q_shape, q_spec = jax.ShapeDtypeStruct((rows, D_ATTN), F32), row(D_ATTN)
        v_shape, v_spec = jax.ShapeDtypeStruct((rows, D_ATTN), BF16), row(D_ATTN)
    return pl.pallas_call(
        functools.partial(_mix_in_kernel, transposed=transposed),
        grid=(nt,),
        in_specs=[row(D_MODEL), _const_spec((1, D_MODEL)), _const_spec(w_in.shape),
                  pl.BlockSpec((1, 1, V7X_LANES), lambda i: (i, 0, 0)),
                  pl.BlockSpec((1, 1, V7X_LANES), lambda i: (i, 0, 0)),
                  _const_spec((ROW_TILE, V7X_LANES)), _const_spec((ROW_TILE, V7X_LANES)),
                  _const_spec((1, V7X_LANES)), _const_spec((1, V7X_LANES))],
        out_specs=[row(D_POOL), row(D_ATTN),
                   pl.BlockSpec((ROW_TILE, H_DIFF, 2 * HEAD_DIM), lambda i: (i, 0, 0)),
                   row(2 * D_MODEL),
                   q_spec,
                   pl.BlockSpec((H_DIFF, ROW_TILE, 2 * HEAD_DIM), lambda i: (0, i, 0)),
                   v_spec],
        out_shape=[jax.ShapeDtypeStruct((rows, D_POOL), F32),
                   jax.ShapeDtypeStruct((rows, D_ATTN), F32),
                   jax.ShapeDtypeStruct((rows, H_DIFF, 2 * HEAD_DIM), F32),
                   jax.ShapeDtypeStruct((rows, 2 * D_MODEL), BF16),
                   q_shape,
                   jax.ShapeDtypeStruct((H_DIFF, rows, 2 * HEAD_DIM), BF16),
                   v_shape],
        compiler_params=pltpu.CompilerParams(
            dimension_semantics=("arbitrary",), vmem_limit_bytes=_mib(48)),
        name="mix_in",
    )(h, mix_pre_g, w_in, *tables)


def _lambda(lam_ref):
    lq1, lk1, lq2, lk2 = (lam_ref[i:i + 1, :] for i in range(4))
    return (jnp.exp(jnp.sum(lq1 * lk1, axis=1, keepdims=True))
            - jnp.exp(jnp.sum(lq2 * lk2, axis=1, keepdims=True)) + LAM_INIT)


def _attn_prompt_kernel(lam_ref, g_ref, q_ref, qn_ref, k_ref, v_ref, o_ref, acc_ref, s_ref, smax_ref):
    tb = ROW_TILE
    hd2 = 2 * HEAD_DIM
    i = pl.program_id(1)
    acc_ref[...] = jnp.zeros_like(acc_ref)
    heads = [slice(h * hd2, (h + 1) * hd2) for h in range(ATTN_HEADS_PER_STEP)]
    strips = [slice(c, c + ATTN_STRIP) for c in range(0, 2 * tb, ATTN_STRIP)]

    def logits(j, rows, cols, q_blk):
        kj = k_ref[rows.start // hd2, pl.ds(pl.multiple_of(j * tb, tb), tb), :]
        return _dot(kj, q_blk[0, rows, cols])

    def diagonal_weights(h, n, q0, keys, m_old):
        lanes = V7X_LANES
        upper = lax.broadcasted_iota(jnp.int32, (CHUNK, lanes), 1) >= CHUNK
        m_new, e_cols = [], []
        for lt in range(ATTN_STRIP // lanes):
            lane_cols = slice(lt * lanes, (lt + 1) * lanes)
            full = ((q0 + lt * lanes) // CHUNK + 1) * CHUNK
            half = s_ref[h, n, full:full + CHUNK, lane_cols]
            s = jnp.concatenate([s_ref[h, n, 0:full, lane_cols],
                                 jnp.where(upper, half, -jnp.inf)], axis=0)
            m_lt = jnp.maximum(m_old[:, lane_cols], jnp.max(s, axis=0, keepdims=True))
            e = jnp.exp2(s - m_lt)
            m_new.append(m_lt)
            hidden = keys - full - CHUNK
            e_cols.append(e if hidden == 0 else
                          jnp.concatenate([e, jnp.zeros((hidden, lanes), F32)], axis=0))
        return jnp.concatenate(m_new, axis=1), jnp.concatenate(e_cols, axis=1)

    def consume(j, m, l, smax, diagonal, next_logits):
        m = [list(x) for x in m]
        l = [list(x) for x in l]
        smax = [list(x) for x in smax]
        for n, cols in enumerate(strips):
            q0 = cols.start % tb
            keys = q0 + ATTN_STRIP if diagonal else tb
            for h, rows in enumerate(heads):
                s_next = next_logits(rows, cols)
                if diagonal:
                    m_new, e = diagonal_weights(h, n, q0, keys, m[h][n])
                    alpha = jnp.exp2(m[h][n] - m_new)
                else:
                    s = s_ref[h, n, 0:keys, :]
                    m_new = jnp.maximum(m[h][n], smax[h][n])
                    alpha = jnp.exp2(m[h][n] - m_new)
                    e = jnp.exp2(s - m_new)
                pv = _dot(v_ref[j, rows, 0:keys], e.astype(BF16))
                l[h][n] = alpha * l[h][n] + jnp.sum(e, axis=0, keepdims=True)
                m[h][n] = m_new
                acc_ref[h, n] = acc_ref[h, n] * alpha + pv
                s_ref[h, n] = s_next
                smax[h][n] = jnp.max(s_next, axis=0, keepdims=True)
        return m, l, smax

    @pl.when(i == 0)
    def _():
        for h, rows in enumerate(heads):
            for n, cols in enumerate(strips):
                s0 = logits(0, rows, cols, q_ref)
                s_ref[h, n] = s0
                smax_ref[h, n] = jnp.max(s0, axis=0, keepdims=True)

    m0 = [[jnp.full((1, ATTN_STRIP), -jnp.inf, F32) for _ in strips] for _ in heads]
    l0 = [[jnp.zeros((1, ATTN_STRIP), F32) for _ in strips] for _ in heads]
    smax0 = [[smax_ref[h, n] for n in range(len(strips))] for h in range(len(heads))]

    def full_block(j, state):
        return consume(j, *state, False, lambda r, c: logits(j + 1, r, c, q_ref))

    def group(t, st):
        for u in range(ATTN_UNROLL):
            st = full_block(ATTN_UNROLL * t + u, st)
        return st

    state = lax.fori_loop(0, i // ATTN_UNROLL, group, (m0, l0, smax0))
    state = lax.fori_loop(i - i % ATTN_UNROLL, i, full_block, state)
    m, l, smax = consume(i, *state, True, lambda r, c: logits(0, r, c, qn_ref))
    for h in range(len(heads)):
        for n in range(len(strips)):
            smax_ref[h, n] = smax[h][n]

    lam = _lambda(lam_ref)
    for h, rows in enumerate(heads):
        o = jnp.concatenate([acc_ref[h, n] * (1.0 / l[h][n]) for n in range(len(strips))], axis=1)
        o = o[:, 0:tb] - lam * o[:, tb:2 * tb]
        ms = jnp.mean(o * o, axis=0, keepdims=True)
        o_ref[rows, :] = o * lax.rsqrt(ms + EPS) * g_ref[...] * (1.0 - LAM_INIT)


def _attn_prompt(lam, subln_col, q_t, k_b, v_t):
    nt = q_t.shape[0]
    rows = nt * ROW_TILE
    hd2 = 2 * HEAD_DIM
    gw = ATTN_HEADS_PER_STEP * hd2
    n_strips = 2 * ROW_TILE // ATTN_STRIP
    resident = dict(pipeline_mode=pl.Buffered(1))
    return pl.pallas_call(
        _attn_prompt_kernel,
        grid=(H_DIFF // ATTN_HEADS_PER_STEP, nt),
        in_specs=[pl.BlockSpec((4, HEAD_DIM), lambda h, i: (0, 0)),
                  pl.BlockSpec((hd2, 1), lambda h, i: (0, 0)),
                  pl.BlockSpec((1, gw, 2 * ROW_TILE), lambda h, i: (i, h, 0)),
                  pl.BlockSpec((1, gw, 2 * ROW_TILE), lambda h, i: (jnp.minimum(i + 1, nt - 1), h, 0)),
                  pl.BlockSpec((ATTN_HEADS_PER_STEP, rows, hd2), lambda h, i: (h, 0, 0), **resident),
                  pl.BlockSpec((nt, gw, ROW_TILE), lambda h, i: (0, h, 0), **resident)],
        out_specs=pl.BlockSpec((gw, ROW_TILE), lambda h, i: (h, i)),
        out_shape=jax.ShapeDtypeStruct((D_ATTN, rows), F32),
        scratch_shapes=[pltpu.VMEM((ATTN_HEADS_PER_STEP, n_strips, hd2, ATTN_STRIP), F32),
                        pltpu.VMEM((ATTN_HEADS_PER_STEP, n_strips, ROW_TILE, ATTN_STRIP), F32),
                        pltpu.VMEM((ATTN_HEADS_PER_STEP, n_strips, 1, ATTN_STRIP), F32)],
        compiler_params=pltpu.CompilerParams(
            dimension_semantics=("arbitrary", "arbitrary"), vmem_limit_bytes=_mib(56)),
        name="attn_prompt",
    )(lam, subln_col, q_t, q_t, k_b, v_t)


def _merge_kernel(h_ref, attn_ref, gates_ref, halo_ref, body_ref, poolw_ref, pscale_ref,
                  wbp_ref, wba_ref, wout_ref, postg_ref, o_ref, *, hist0, attn_transposed):
    nb, tl, _ = body_ref.shape
    hist = hist0 + pl.program_id(0) * tl
    hi = hist + 1 + lax.broadcasted_iota(jnp.int32, (tl, 1), 0)
    halo_row = lax.broadcasted_iota(jnp.int32, (POOL_HALO, 1), 0)
    halo_ok = halo_row >= POOL_HALO - jnp.minimum(hist, POOL_STATE)
    pooled = []
    for b in range(nb):
        halo = jnp.where(halo_ok, halo_ref[b], 0.0)
        ext = jnp.concatenate([halo, body_ref[b]], axis=0)
        groups = []
        for g, win in enumerate(POOL_WINDOWS):
            x = ext[:, g * POOL_GROUP:(g + 1) * POOL_GROUP]
            s, shift = x, 1
            while shift < win:
                s = s + pltpu.roll(s, shift, 0)
                shift *= 2
            inv_cnt = 1.0 / jnp.minimum(hi, win).astype(F32)
            mixed = s[POOL_HALO:] * inv_cnt - x[POOL_HALO:]
            groups.append(_dot(mixed.astype(BF16), poolw_ref[g]))
        pooled.append(jnp.concatenate(groups, axis=1) * pscale_ref[...])
    pool_out = pooled[0] if nb == 1 else jnp.concatenate(pooled, axis=0)

    attn = attn_ref[...].T if attn_transposed else attn_ref[...]
    branch_pool = _dot(pool_out.astype(BF16), wbp_ref[...])
    branch_attn = _dot(attn.astype(BF16), wba_ref[...])
    merged = (gates_ref[:, 0:D_MODEL].astype(F32) * branch_pool
              + gates_ref[:, D_MODEL:2 * D_MODEL].astype(F32) * branch_attn)
    o_ref[...] = h_ref[...] + _rms(_dot(merged.astype(BF16), wout_ref[...]), postg_ref[...])


def _merge(h, attn, gates, halo, body, halo_spec, body_spec, hist0, attn_transposed,
           pool_w, pool_scale, w_bp, w_ba, w_out, post_g):
    rows = h.shape[0]
    row = lambda w: pl.BlockSpec((ROW_TILE, w), lambda i: (i, 0))
    attn_spec = (pl.BlockSpec((D_ATTN, ROW_TILE), lambda i: (0, i)) if attn_transposed
                 else row(D_ATTN))
    return pl.pallas_call(
        functools.partial(_merge_kernel, hist0=hist0, attn_transposed=attn_transposed),
        grid=(rows // ROW_TILE,),
        in_specs=[row(D_MODEL), attn_spec, row(2 * D_MODEL), halo_spec, body_spec,
                  _const_spec(pool_w.shape), _const_spec(pool_scale.shape),
                  _const_spec(w_bp.shape), _const_spec(w_ba.shape), _const_spec(w_out.shape),
                  _const_spec(post_g.shape)],
        out_specs=row(D_MODEL),
        out_shape=jax.ShapeDtypeStruct((rows, D_MODEL), F32),
        compiler_params=pltpu.CompilerParams(
            dimension_semantics=("arbitrary",), vmem_limit_bytes=_mib(48)),
        name="merge",
    )(h, attn, gates, halo, body, pool_w, pool_scale, w_bp, w_ba, w_out, post_g)


def kernel(x_prompt, x_sample, cache_k, cache_v, state_pool, ffn1_pre_g, ffn1_post_g, ffn1_w_gu, ffn1_w_down, mix_pre_g, mix_post_g, w_in, pool_w, pool_scale, lambda_q1, lambda_k1, lambda_q2, lambda_k2, subln_g, w_branch_pool, w_branch_attn, w_out, ffn2_pre_g, ffn2_post_g, ffn2_w_gu, ffn2_w_down):
    bp, seq, _ = x_prompt.shape
    db, dseq, _ = x_sample.shape
    past = cache_k.shape[2]
    assert bp == 1 and seq % ROW_TILE == 0 and db * dseq == ROW_TILE and dseq >= POOL_STATE
    assert (past + dseq - 1) // CHUNK <= past // CHUNK and past % CHUNK == 0

    l = 0
    bf = lambda w: w[l].astype(BF16)
    g1pre, g1post, gmpre, gmpost = ffn1_pre_g[l:l + 1], ffn1_post_g[l:l + 1], mix_pre_g[l:l + 1], mix_post_g[l:l + 1]
    g2pre, g2post = ffn2_pre_g[l:l + 1], ffn2_post_g[l:l + 1]
    w1gu, w1d, win = bf(ffn1_w_gu), bf(ffn1_w_down), bf(w_in)
    later_w = (ffn2_w_gu[l], ffn2_w_down[l], pool_w[l].reshape(D_POOL, POOL_GROUP),
               w_branch_pool[l], w_branch_attn[l], w_out[l])
    pscale = pool_scale[l:l + 1]
    lam = jnp.concatenate([lambda_q1[l:l + 1], lambda_k1[l:l + 1],
                           lambda_q2[l:l + 1], lambda_k2[l:l + 1]], axis=0)
    subln_row = subln_g[l:l + 1]
    subln_col = subln_g[l][:, None]

    xs = x_sample.reshape(db * dseq, D_MODEL)
    hs = _ffn(xs, g1pre, g1post, w1gu, w1d)
    upool_s, kf_s, vf_s, gates_s, q_s, _, _ = _mix_in(
        hs, gmpre, win, np.array([past]), np.arange(ROW_TILE) % dseq, transposed=False)
    cache_kt = jnp.transpose(cache_k[l], (0, 2, 3, 4, 1)).reshape(db, D_ATTN, past)
    cache_v4 = cache_v[l].reshape(db, past * H_DIFF, 2 * HEAD_DIM)
    decode_attn = lambda seq0: (seq0, db // 2, lam, subln_row, q_s.reshape(db, dseq, D_ATTN),
                                cache_kt, cache_v4, kf_s.reshape(db, dseq, D_ATTN),
                                vf_s.reshape(db, dseq, D_ATTN))

    xp = x_prompt.reshape(seq, D_MODEL)
    nt = seq // ROW_TILE
    hp, attn_s0, w2gu, w2d, wpool, wbp, wba, wo = _ffn(
        xp, g1pre, g1post, w1gu, w1d, decode_attn(0), casts=later_w)
    merge_w = (wpool.reshape(len(POOL_WINDOWS), POOL_GROUP, POOL_GROUP), pscale, wbp, wba, wo, gmpost)
    upool, kf, vf, gates, q_t, k_b, v_t = _mix_in(
        hp, gmpre, win, np.arange(nt) * ROW_TILE, np.arange(ROW_TILE), transposed=True)
    attn_t = _attn_prompt(lam, subln_col, q_t, k_b, v_t)
    per_tile = ROW_TILE // POOL_HALO
    halo_spec = pl.BlockSpec((1, POOL_HALO, D_POOL),
                             lambda i: (jnp.maximum(i * per_tile - 1, 0), 0, 0))
    body_spec = pl.BlockSpec((1, ROW_TILE, D_POOL), lambda i: (i, 0, 0))
    hp = _merge(hp, attn_t, gates, upool.reshape(seq // POOL_HALO, POOL_HALO, D_POOL),
                upool.reshape(nt, ROW_TILE, D_POOL), halo_spec, body_spec, 0, True, *merge_w)
    y_prompt, attn_s1 = _ffn(hp, g2pre, g2post, w2gu, w2d, decode_attn(db // 2))
    y_prompt = y_prompt.reshape(bp, seq, D_MODEL)
    new_k_p = kf.reshape(1, bp, seq, H_DIFF, 2, HEAD_DIM)
    new_v_p = vf.reshape(1, bp, seq, H_DIFF, 2 * HEAD_DIM)
    new_pool_p = upool[seq - POOL_STATE:].reshape(1, bp, POOL_STATE, D_POOL)

    attn_s = jnp.concatenate([attn_s0, attn_s1], axis=0)
    hist = state_pool[l]
    halo_s = jnp.concatenate(
        [jnp.zeros((db, POOL_HALO - POOL_STATE, D_POOL), F32), hist], axis=1)
    body_s = upool_s.reshape(db, dseq, D_POOL)
    hs = _merge(hs, attn_s.reshape(db * dseq, D_ATTN), gates_s, halo_s, body_s,
                pl.BlockSpec((db, POOL_HALO, D_POOL), lambda i: (0, 0, 0)),
                pl.BlockSpec((db, dseq, D_POOL), lambda i: (0, 0, 0)),
                POOL_STATE, False, *merge_w)
    y_sample = _ffn(hs, g2pre, g2post, w2gu, w2d).reshape(db, dseq, D_MODEL)
    new_k_s = kf_s.reshape(1, db, dseq, H_DIFF, 2, HEAD_DIM)
    new_v_s = vf_s.reshape(1, db, dseq, H_DIFF, 2 * HEAD_DIM)
    new_pool_s = body_s[:, dseq - POOL_STATE:].reshape(1, db, POOL_STATE, D_POOL)

    return (y_prompt, y_sample, new_k_p, new_v_p, new_pool_p, new_k_s, new_v_s, new_pool_s)
```

```python
import functools
import math

import numpy as np
import jax
import jax.numpy as jnp
from jax import lax
from jax.experimental import pallas as pl
from jax.experimental.pallas import tpu as pltpu

D_MODEL = 1024
D_FF = 2816
D_POOL = 512
POOL_WINDOWS = (2, 4, 8, 16)
POOL_GROUP = 128
POOL_STATE = 15
H_DIFF = 4
HEAD_DIM = 64
D_ATTN = 512
ROT_DIM = 16
ROPE_THETA = 500000.0
CHUNK = 64
EPS = 1e-6
LAM_INIT = 0.8 - 0.6 * math.exp(-0.3 * 0)

V7X_LANES = 128
BF16_SUBLANES = 16
POOL_HALO = 16
ROW_TILE = 512
MERGE_ROW_TILE = 1024
FF_CHUNKS = tuple((c, 256) for c in range(0, D_FF, 256))
ATTN_STRIP = 256
ATTN_HEADS_PER_STEP = 4
ATTN_UNROLL = 4
QK_SCALE_LOG2 = HEAD_DIM ** -0.5 * math.log2(math.e)

F32 = jnp.float32
BF16 = jnp.bfloat16


def _mib(n):
    return int(n * 1024 * 1024)


def _rms(x, g):
    return x * lax.rsqrt(jnp.mean(x * x, axis=-1, keepdims=True) + EPS) * g


def _dot(a, b):
    return jnp.dot(a, b, preferred_element_type=F32)


def _const_spec(shape):
    nd = len(shape)
    return pl.BlockSpec(shape, lambda *_: (0,) * nd, pipeline_mode=pl.Buffered(1))


def _decode_attn_stages(first, last, lam_ref, g_ref, q_ref, ck_ref, cv_ref, kn_ref, vn_ref,
                        o_ref, m_ref, l_ref, acc_ref):
    hd2 = 2 * HEAD_DIM
    keys = ck_ref.shape[2]
    nt_dims = (((1,), (1,)), ((), ()))
    heads = [slice(h * hd2, (h + 1) * hd2) for h in range(H_DIFF)]
    state = {}

    def scores():
        for h, sl in enumerate(heads):
            q = q_ref[0, :, sl]
            comp = lax.broadcasted_iota(jnp.int32, q.shape, 1)
            lhs = jnp.concatenate([jnp.where(comp < HEAD_DIM, q, 0.0),
                                   jnp.where(comp >= HEAD_DIM, q, 0.0)], axis=0).astype(BF16)
            s_c = _dot(lhs, ck_ref[0, sl, :].astype(BF16))
            s_n = lax.dot_general(lhs, kn_ref[0, :, sl].astype(BF16), nt_dims,
                                  preferred_element_type=F32)
            s_n = jnp.where(last, s_n, -jnp.inf)
            m_old = jnp.where(first, -jnp.inf, m_ref[h])
            m_new = jnp.maximum(m_old, jnp.maximum(jnp.max(s_c, axis=1, keepdims=True),
                                                   jnp.max(s_n, axis=1, keepdims=True)))
            alpha = jnp.exp2(m_old - m_new)
            e_c = jnp.exp2(s_c - m_new)
            e_n = jnp.exp2(s_n - m_new)
            l_new = (alpha * jnp.where(first, 0.0, l_ref[h])
                     + jnp.sum(e_c, axis=1, keepdims=True) + jnp.sum(e_n, axis=1, keepdims=True))
            m_ref[h] = m_new
            l_ref[h] = l_new
            state[h] = (alpha, e_c.astype(BF16), e_n.astype(BF16), l_new)

    def outputs():
        lam = _lambda(lam_ref)
        for h, sl in enumerate(heads):
            alpha, e_c, e_n, l_new = state[h]
            v_c = cv_ref[0, pl.ds(h, keys, stride=H_DIFF), :].astype(BF16)
            acc = (alpha * jnp.where(first, 0.0, acc_ref[h])
                   + _dot(e_c, v_c) + _dot(e_n, vn_ref[0, :, sl].astype(BF16)))
            acc_ref[h] = acc
            o = acc * (1.0 / l_new)
            t = o.shape[0] // 2
            o = o[0:t] - lam * o[t:2 * t]
            o_ref[0, :, sl] = _rms(o, g_ref[...]) * (1.0 - LAM_INIT)

    return scores, outputs


def _ffn_kernel(*refs, decode_parts, n_casts):
    x_ref, pre_ref, post_ref, wgu_ref, wd_ref = refs[:5]
    n_attn_in = 7 if decode_parts else 0
    attn_in = refs[5:5 + n_attn_in]
    cast_in = refs[5 + n_attn_in:5 + n_attn_in + n_casts]
    n_out = 1 + (1 if decode_parts else 0)
    outs = refs[5 + n_attn_in + n_casts:]
    o_ref = outs[0]
    cast_out = outs[n_out:n_out + n_casts]
    if decode_parts:
        attn_o_ref = outs[1]
        m_ref, l_ref, acc_ref = outs[n_out + n_casts:]
        step = pl.program_id(0)

        @pl.when(step == 0)
        def _():
            m_ref[...] = jnp.zeros_like(m_ref)
            l_ref[...] = jnp.zeros_like(l_ref)
            acc_ref[...] = jnp.zeros_like(acc_ref)

        part = step % decode_parts
        side = _decode_attn_stages(part == 0, part == decode_parts - 1, *attn_in,
                                   attn_o_ref, m_ref, l_ref, acc_ref)
    else:
        side = ()
    for src, dst in zip(cast_in, cast_out):
        dst[...] = src[...].astype(BF16)
    x = x_ref[...]
    xn = _rms(x, pre_ref[...]).astype(BF16)
    acc = None
    for idx, (c0, cw) in enumerate(FF_CHUNKS):
        gate = _dot(xn, wgu_ref[:, c0:c0 + cw])
        up = _dot(xn, wgu_ref[:, D_FF + c0:D_FF + c0 + cw])
        act = (gate * jax.nn.sigmoid(gate) * up).astype(BF16)
        part_out = _dot(act, wd_ref[c0:c0 + cw, :])
        acc = part_out if acc is None else acc + part_out
        if idx < len(side):
            side[idx]()
    o_ref[...] = x + 0.5 * _rms(acc, post_ref[...])


def _ffn(x, pre_g, post_g, w_gu, w_down, decode=None, casts=()):
    rows = x.shape[0]
    steps = rows // ROW_TILE
    row_spec = pl.BlockSpec((ROW_TILE, D_MODEL), lambda i: (i, 0))
    in_specs = [row_spec, _const_spec((1, D_MODEL)), _const_spec((1, D_MODEL)),
                _const_spec((D_MODEL, 2 * D_FF)), _const_spec((D_FF, D_MODEL))]
    out_specs, out_shape = [row_spec], [jax.ShapeDtypeStruct((rows, D_MODEL), F32)]
    args, scratch, parts = [x, pre_g, post_g, w_gu, w_down], [], 0
    if decode is not None:
        seq0, n_seq, lam, subln_row, q, cache_kt, cache_v4, k_new, v_new = decode
        t, past = q.shape[1], cache_kt.shape[2]
        parts = steps // n_seq
        keys = past // parts
        assert steps == n_seq * parts and past == keys * parts and keys % V7X_LANES == 0
        seq = pl.BlockSpec((1, t, D_ATTN), lambda i: (seq0 + i // parts, 0, 0))
        in_specs += [pl.BlockSpec((4, HEAD_DIM), lambda i: (0, 0)),
                     pl.BlockSpec((1, 2 * HEAD_DIM), lambda i: (0, 0)),
                     seq,
                     pl.BlockSpec((1, D_ATTN, keys), lambda i: (seq0 + i // parts, 0, i % parts)),
                     pl.BlockSpec((1, keys * H_DIFF, 2 * HEAD_DIM),
                                  lambda i: (seq0 + i // parts, i % parts, 0)),
                     seq, seq]
        out_specs.append(pl.BlockSpec((1, t, D_ATTN), lambda i: (i // parts, 0, 0)))
        out_shape.append(jax.ShapeDtypeStruct((n_seq, t, D_ATTN), F32))
        args += [lam, subln_row, q, cache_kt, cache_v4, k_new, v_new]
        scratch = [pltpu.VMEM((H_DIFF, 2 * t, 1), F32), pltpu.VMEM((H_DIFF, 2 * t, 1), F32),
                   pltpu.VMEM((H_DIFF, 2 * t, 2 * HEAD_DIM), F32)]
    for w in casts:
        per_step = w.shape[0] // steps
        span = BF16_SUBLANES // math.gcd(per_step, BF16_SUBLANES)
        assert w.shape[0] == per_step * steps and steps % span == 0
        spec = pl.BlockSpec((per_step * span, w.shape[1]), lambda i, span=span: (i // span, 0))
        in_specs.append(spec)
        out_specs.append(spec)
        out_shape.append(jax.ShapeDtypeStruct(w.shape, BF16))
        args.append(w)
    outs = pl.pallas_call(
        functools.partial(_ffn_kernel, decode_parts=parts, n_casts=len(casts)),
        grid=(steps,),
        in_specs=in_specs,
        out_specs=out_specs,
        out_shape=out_shape,
        scratch_shapes=scratch,
        compiler_params=pltpu.CompilerParams(
            dimension_semantics=("arbitrary",), vmem_limit_bytes=_mib(56 if parts else 48)),
        name="ffn_riders" if (parts or casts) else "ffn",
    )(*args)
    return outs[0] if len(outs) == 1 else outs


def _rope_tables(tile_pos0, row_pos):
    d = np.arange(V7X_LANES) % HEAD_DIM
    inv = np.where(d < ROT_DIM, ROPE_THETA ** (-(d % (ROT_DIM // 2)) / (ROT_DIM // 2)), 0.0)
    ang_a = np.asarray(tile_pos0, np.float64)[:, None] * inv[None, :]
    ang_b = np.asarray(row_pos, np.float64)[:, None] * inv[None, :]
    half = ROT_DIM // 2
    sel_lo = np.where(d < half, -1.0, 0.0)[None, :]
    sel_hi = np.where((d >= half) & (d < ROT_DIM), 1.0, 0.0)[None, :]
    f = lambda a: jnp.asarray(a, F32)
    return (f(np.cos(ang_a))[:, None, :], f(np.sin(ang_a))[:, None, :],
            f(np.cos(ang_b)), f(np.sin(ang_b)), f(sel_lo), f(sel_hi))


def _mix_in_kernel(h_ref, g_ref, win_ref, cos_a_ref, sin_a_ref, cos_b_ref, sin_b_ref,
                   sel_lo_ref, sel_hi_ref,
                   upool_ref, kf_ref, vf_ref, gates_ref, q_ref, kb_ref, v_ref, *, transposed):
    u = _rms(h_ref[...], g_ref[...]).astype(BF16)
    upool_ref[...] = _dot(u, win_ref[:, 0:D_POOL])

    ca, sa = cos_a_ref[0], sin_a_ref[0]
    cb, sb = cos_b_ref[...], sin_b_ref[...]
    cos = ca * cb - sa * sb
    sin = sa * cb + ca * sb
    rep = D_ATTN // V7X_LANES
    cos_f = jnp.concatenate([cos] * rep, axis=1)
    sin_lo = jnp.concatenate([sin * sel_lo_ref[...]] * rep, axis=1)
    sin_hi = jnp.concatenate([sin * sel_hi_ref[...]] * rep, axis=1)
    half = ROT_DIM // 2

    def rope(x):
        return (x * cos_f + pltpu.roll(x, D_ATTN - half, 1) * sin_lo
                + pltpu.roll(x, half, 1) * sin_hi)

    q = rope(_dot(u, win_ref[:, D_POOL:D_POOL + D_ATTN])) * QK_SCALE_LOG2
    k = rope(_dot(u, win_ref[:, D_POOL + D_ATTN:D_POOL + 2 * D_ATTN]))
    v = _dot(u, win_ref[:, D_POOL + 2 * D_ATTN:D_POOL + 3 * D_ATTN])
    kf_ref[...] = k
    vf_ref[...] = v.reshape(v.shape[0], H_DIFF, 2 * HEAD_DIM)
    for hh in range(H_DIFF):
        kb_ref[hh] = k[:, hh * 2 * HEAD_DIM:(hh + 1) * 2 * HEAD_DIM].astype(BF16)
    if transposed:
        comp = lax.broadcasted_iota(jnp.int32, q.shape, 1) % (2 * HEAD_DIM)
        q_top = jnp.where(comp < HEAD_DIM, q, 0.0)
        q_bot = jnp.where(comp >= HEAD_DIM, q, 0.0)
        q_ref[0, :, 0:ROW_TILE] = q_top.T.astype(BF16)
        q_ref[0, :, ROW_TILE:2 * ROW_TILE] = q_bot.T.astype(BF16)
        v_ref[0] = v.T.astype(BF16)
    else:
        q_ref[...] = q
        v_ref[...] = v.astype(BF16)

    g0 = D_POOL + 3 * D_ATTN
    for c in range(2):
        z = _dot(u, win_ref[:, g0 + c * D_MODEL:g0 + (c + 1) * D_MODEL])
        gates_ref[:, c * D_MODEL:(c + 1) * D_MODEL] = jax.nn.sigmoid(z).astype(BF16)


def _mix_in(h, mix_pre_g, w_in, tile_pos0, row_pos, transposed):
    rows = h.shape[0]
    nt = rows // ROW_TILE
    tables = _rope_tables(tile_pos0, row_pos)
    row = lambda w: pl.BlockSpec((ROW_TILE, w), lambda i: (i, 0))
    if transposed:
        q_shape = jax.ShapeDtypeStruct((nt, D_ATTN, 2 * ROW_TILE), BF16)
        q_spec = pl.BlockSpec((1, D_ATTN, 2 * ROW_TILE), lambda i: (i, 0, 0))
        v_shape = jax.ShapeDtypeStruct((nt, D_ATTN, ROW_TILE), BF16)
        v_spec = pl.BlockSpec((1, D_ATTN, ROW_TILE), lambda i: (i, 0, 0))
    else:
        q_shape, q_spec = jax.ShapeDtypeStruct((rows, D_ATTN), F32), row(D_ATTN)
        v_shape, v_spec = jax.ShapeDtypeStruct((rows, D_ATTN), BF16), row(D_ATTN)
    return pl.pallas_call(
        functools.partial(_mix_in_kernel, transposed=transposed),
        grid=(nt,),
        in_specs=[row(D_MODEL), _const_spec((1, D_MODEL)), _const_spec(w_in.shape),
                  pl.BlockSpec((1, 1, V7X_LANES), lambda i: (i, 0, 0)),
                  pl.BlockSpec((1, 1, V7X_LANES), lambda i: (i, 0, 0)),
                  _const_spec((ROW_TILE, V7X_LANES)), _const_spec((ROW_TILE, V7X_LANES)),
                  _const_spec((1, V7X_LANES)), _const_spec((1, V7X_LANES))],
        out_specs=[row(D_POOL), row(D_ATTN),
                   pl.BlockSpec((ROW_TILE, H_DIFF, 2 * HEAD_DIM), lambda i: (i, 0, 0)),
                   row(2 * D_MODEL),
                   q_spec,
                   pl.BlockSpec((H_DIFF, ROW_TILE, 2 * HEAD_DIM), lambda i: (0, i, 0)),
                   v_spec],
        out_shape=[jax.ShapeDtypeStruct((rows, D_POOL), F32),
                   jax.ShapeDtypeStruct((rows, D_ATTN), F32),
                   jax.ShapeDtypeStruct((rows, H_DIFF, 2 * HEAD_DIM), F32),
                   jax.ShapeDtypeStruct((rows, 2 * D_MODEL), BF16),
                   q_shape,
                   jax.ShapeDtypeStruct((H_DIFF, rows, 2 * HEAD_DIM), BF16),
                   v_shape],
        compiler_params=pltpu.CompilerParams(
            dimension_semantics=("arbitrary",), vmem_limit_bytes=_mib(48)),
        name="mix_in",
    )(h, mix_pre_g, w_in, *tables)


def _lambda(lam_ref):
    lq1, lk1, lq2, lk2 = (lam_ref[i:i + 1, :] for i in range(4))
    return (jnp.exp(jnp.sum(lq1 * lk1, axis=1, keepdims=True))
            - jnp.exp(jnp.sum(lq2 * lk2, axis=1, keepdims=True)) + LAM_INIT)


def _attn_prompt_kernel(lam_ref, g_ref, q_ref, qn_ref, k_ref, v_ref, o_ref, acc_ref, s_ref, smax_ref):
    tb = ROW_TILE
    hd2 = 2 * HEAD_DIM
    i = pl.program_id(1)
    acc_ref[...] = jnp.zeros_like(acc_ref)
    heads = [slice(h * hd2, (h + 1) * hd2) for h in range(ATTN_HEADS_PER_STEP)]
    strips = [slice(c, c + ATTN_STRIP) for c in range(0, 2 * tb, ATTN_STRIP)]

    def logits(j, rows, cols, q_blk):
        kj = k_ref[rows.start // hd2, pl.ds(pl.multiple_of(j * tb, tb), tb), :]
        return _dot(kj, q_blk[0, rows, cols])

    def diagonal_weights(h, n, q0, keys, m_old):
        lanes = V7X_LANES
        upper = lax.broadcasted_iota(jnp.int32, (CHUNK, lanes), 1) >= CHUNK
        m_new, e_cols = [], []
        for lt in range(ATTN_STRIP // lanes):
            lane_cols = slice(lt * lanes, (lt + 1) * lanes)
            full = ((q0 + lt * lanes) // CHUNK + 1) * CHUNK
            half = s_ref[h, n, full:full + CHUNK, lane_cols]
            s = jnp.concatenate([s_ref[h, n, 0:full, lane_cols],
                                 jnp.where(upper, half, -jnp.inf)], axis=0)
            m_lt = jnp.maximum(m_old[:, lane_cols], jnp.max(s, axis=0, keepdims=True))
            e = jnp.exp2(s - m_lt)
            m_new.append(m_lt)
            hidden = keys - full - CHUNK
            e_cols.append(e if hidden == 0 else
                          jnp.concatenate([e, jnp.zeros((hidden, lanes), F32)], axis=0))
        return jnp.concatenate(m_new, axis=1), jnp.concatenate(e_cols, axis=1)

    def consume(j, m, l, smax, diagonal, next_logits):
        m = [list(x) for x in m]
        l = [list(x) for x in l]
        smax = [list(x) for x in smax]
        for n, cols in enumerate(strips):
            q0 = cols.start % tb
            keys = q0 + ATTN_STRIP if diagonal else tb
            for h, rows in enumerate(heads):
                s_next = next_logits(rows, cols)
                if diagonal:
                    m_new, e = diagonal_weights(h, n, q0, keys, m[h][n])
                    alpha = jnp.exp2(m[h][n] - m_new)
                else:
                    s = s_ref[h, n, 0:keys, :]
                    m_new = jnp.maximum(m[h][n], smax[h][n])
                    alpha = jnp.exp2(m[h][n] - m_new)
                    e = jnp.exp2(s - m_new)
                pv = _dot(v_ref[j, rows, 0:keys], e.astype(BF16))
                l[h][n] = alpha * l[h][n] + jnp.sum(e, axis=0, keepdims=True)
                m[h][n] = m_new
                acc_ref[h, n] = acc_ref[h, n] * alpha + pv
                s_ref[h, n] = s_next
                smax[h][n] = jnp.max(s_next, axis=0, keepdims=True)
        return m, l, smax

    @pl.when(i == 0)
    def _():
        for h, rows in enumerate(heads):
            for n, cols in enumerate(strips):
                s0 = logits(0, rows, cols, q_ref)
                s_ref[h, n] = s0
                smax_ref[h, n] = jnp.max(s0, axis=0, keepdims=True)

    m0 = [[jnp.full((1, ATTN_STRIP), -jnp.inf, F32) for _ in strips] for _ in heads]
    l0 = [[jnp.zeros((1, ATTN_STRIP), F32) for _ in strips] for _ in heads]
    smax0 = [[smax_ref[h, n] for n in range(len(strips))] for h in range(len(heads))]

    def full_block(j, state):
        return consume(j, *state, False, lambda r, c: logits(j + 1, r, c, q_ref))

    def group(t, st):
        for u in range(ATTN_UNROLL):
            st = full_block(ATTN_UNROLL * t + u, st)
        return st

    state = lax.fori_loop(0, i // ATTN_UNROLL, group, (m0, l0, smax0))
    state = lax.fori_loop(i - i % ATTN_UNROLL, i, full_block, state)
    m, l, smax = consume(i, *state, True, lambda r, c: logits(0, r, c, qn_ref))
    for h in range(len(heads)):
        for n in range(len(strips)):
            smax_ref[h, n] = smax[h][n]

    lam = _lambda(lam_ref)
    for h, rows in enumerate(heads):
        o = jnp.concatenate([acc_ref[h, n] * (1.0 / l[h][n]) for n in range(len(strips))], axis=1)
        o = o[:, 0:tb] - lam * o[:, tb:2 * tb]
        ms = jnp.mean(o * o, axis=0, keepdims=True)
        o_ref[rows, :] = o * lax.rsqrt(ms + EPS) * g_ref[...] * (1.0 - LAM_INIT)


def _attn_prompt(lam, subln_col, q_t, k_b, v_t):
    nt = q_t.shape[0]
    rows = nt * ROW_TILE
    hd2 = 2 * HEAD_DIM
    gw = ATTN_HEADS_PER_STEP * hd2
    n_strips = 2 * ROW_TILE // ATTN_STRIP
    resident = dict(pipeline_mode=pl.Buffered(1))
    return pl.pallas_call(
        _attn_prompt_kernel,
        grid=(H_DIFF // ATTN_HEADS_PER_STEP, nt),
        in_specs=[pl.BlockSpec((4, HEAD_DIM), lambda h, i: (0, 0)),
                  pl.BlockSpec((hd2, 1), lambda h, i: (0, 0)),
                  pl.BlockSpec((1, gw, 2 * ROW_TILE), lambda h, i: (i, h, 0)),
                  pl.BlockSpec((1, gw, 2 * ROW_TILE), lambda h, i: (jnp.minimum(i + 1, nt - 1), h, 0)),
                  pl.BlockSpec((ATTN_HEADS_PER_STEP, rows, hd2), lambda h, i: (h, 0, 0), **resident),
                  pl.BlockSpec((nt, gw, ROW_TILE), lambda h, i: (0, h, 0), **resident)],
        out_specs=pl.BlockSpec((gw, ROW_TILE), lambda h, i: (h, i)),
        out_shape=jax.ShapeDtypeStruct((D_ATTN, rows), F32),
        scratch_shapes=[pltpu.VMEM((ATTN_HEADS_PER_STEP, n_strips, hd2, ATTN_STRIP), F32),
                        pltpu.VMEM((ATTN_HEADS_PER_STEP, n_strips, ROW_TILE, ATTN_STRIP), F32),
                        pltpu.VMEM((ATTN_HEADS_PER_STEP, n_strips, 1, ATTN_STRIP), F32)],
        compiler_params=pltpu.CompilerParams(
            dimension_semantics=("arbitrary", "arbitrary"), vmem_limit_bytes=_mib(56)),
        name="attn_prompt",
    )(lam, subln_col, q_t, q_t, k_b, v_t)


def _merge_kernel(h_ref, attn_ref, gates_ref, halo_ref, body_ref, poolw_ref, pscale_ref,
                  wbp_ref, wba_ref, wout_ref, postg_ref, o_ref, *, hist0, attn_transposed):
    nb, tl, _ = body_ref.shape
    hist = hist0 + pl.program_id(0) * tl
    hi = hist + 1 + lax.broadcasted_iota(jnp.int32, (tl, 1), 0)
    halo_row = lax.broadcasted_iota(jnp.int32, (POOL_HALO, 1), 0)
    halo_ok = halo_row >= POOL_HALO - jnp.minimum(hist, POOL_STATE)
    pooled = []
    for b in range(nb):
        halo = jnp.where(halo_ok, halo_ref[b], 0.0)
        ext = jnp.concatenate([halo, body_ref[b]], axis=0)
        groups = []
        for g, win in enumerate(POOL_WINDOWS):
            x = ext[:, g * POOL_GROUP:(g + 1) * POOL_GROUP]
            s, shift = x, 1
            while shift < win:
                s = s + pltpu.roll(s, shift, 0)
                shift *= 2
            inv_cnt = 1.0 / jnp.minimum(hi, win).astype(F32)
            mixed = s[POOL_HALO:] * inv_cnt - x[POOL_HALO:]
            groups.append(_dot(mixed.astype(BF16), poolw_ref[g]))
        pooled.append(jnp.concatenate(groups, axis=1) * pscale_ref[...])
    pool_out = pooled[0] if nb == 1 else jnp.concatenate(pooled, axis=0)

    attn = attn_ref[...].T if attn_transposed else attn_ref[...]
    branch_pool = _dot(pool_out.astype(BF16), wbp_ref[...])
    branch_attn = _dot(attn.astype(BF16), wba_ref[...])
    merged = (gates_ref[:, 0:D_MODEL].astype(F32) * branch_pool
              + gates_ref[:, D_MODEL:2 * D_MODEL].astype(F32) * branch_attn)
    o_ref[...] = h_ref[...] + _rms(_dot(merged.astype(BF16), wout_ref[...]), postg_ref[...])


def _merge(h, attn, gates, halo, body, halo_spec, body_spec, hist0, attn_transposed,
           pool_w, pool_scale, w_bp, w_ba, w_out, post_g):
    rows = h.shape[0]
    tile = body_spec.block_shape[0] * body_spec.block_shape[1]
    row = lambda w: pl.BlockSpec((tile, w), lambda i: (i, 0))
    attn_spec = (pl.BlockSpec((D_ATTN, tile), lambda i: (0, i)) if attn_transposed
                 else row(D_ATTN))
    return pl.pallas_call(
        functools.partial(_merge_kernel, hist0=hist0, attn_transposed=attn_transposed),
        grid=(rows // tile,),
        in_specs=[row(D_MODEL), attn_spec, row(2 * D_MODEL), halo_spec, body_spec,
                  _const_spec(pool_w.shape), _const_spec(pool_scale.shape),
                  _const_spec(w_bp.shape), _const_spec(w_ba.shape), _const_spec(w_out.shape),
                  _const_spec(post_g.shape)],
        out_specs=row(D_MODEL),
        out_shape=jax.ShapeDtypeStruct((rows, D_MODEL), F32),
        compiler_params=pltpu.CompilerParams(
            dimension_semantics=("arbitrary",), vmem_limit_bytes=_mib(48)),
        name="merge",
    )(h, attn, gates, halo, body, pool_w, pool_scale, w_bp, w_ba, w_out, post_g)


def kernel(x_prompt, x_sample, cache_k, cache_v, state_pool, ffn1_pre_g, ffn1_post_g, ffn1_w_gu, ffn1_w_down, mix_pre_g, mix_post_g, w_in, pool_w, pool_scale, lambda_q1, lambda_k1, lambda_q2, lambda_k2, subln_g, w_branch_pool, w_branch_attn, w_out, ffn2_pre_g, ffn2_post_g, ffn2_w_gu, ffn2_w_down):
    bp, seq, _ = x_prompt.shape
    db, dseq, _ = x_sample.shape
    past = cache_k.shape[2]
    assert bp == 1 and seq % ROW_TILE == 0 and db * dseq == ROW_TILE and dseq >= POOL_STATE
    assert (past + dseq - 1) // CHUNK <= past // CHUNK and past % CHUNK == 0

    l = 0
    bf = lambda w: w[l].astype(BF16)
    g1pre, g1post, gmpre, gmpost = ffn1_pre_g[l:l + 1], ffn1_post_g[l:l + 1], mix_pre_g[l:l + 1], mix_post_g[l:l + 1]
    g2pre, g2post = ffn2_pre_g[l:l + 1], ffn2_post_g[l:l + 1]
    w1gu, w1d, win = bf(ffn1_w_gu), bf(ffn1_w_down), bf(w_in)
    later_w = (ffn2_w_gu[l], ffn2_w_down[l], pool_w[l].reshape(D_POOL, POOL_GROUP),
               w_branch_pool[l], w_branch_attn[l], w_out[l])
    pscale = pool_scale[l:l + 1]
    lam = jnp.concatenate([lambda_q1[l:l + 1], lambda_k1[l:l + 1],
                           lambda_q2[l:l + 1], lambda_k2[l:l + 1]], axis=0)
    subln_row = subln_g[l:l + 1]
    subln_col = subln_g[l][:, None]

    xs = x_sample.reshape(db * dseq, D_MODEL)
    hs = _ffn(xs, g1pre, g1post, w1gu, w1d)
    upool_s, kf_s, vf_s, gates_s, q_s, _, _ = _mix_in(
        hs, gmpre, win, np.array([past]), np.arange(ROW_TILE) % dseq, transposed=False)
    cache_kt = jnp.transpose(cache_k[l], (0, 2, 3, 4, 1)).reshape(db, D_ATTN, past)
    cache_v4 = cache_v[l].reshape(db, past * H_DIFF, 2 * HEAD_DIM)
    decode_attn = lambda seq0: (seq0, db // 2, lam, subln_row, q_s.reshape(db, dseq, D_ATTN),
                                cache_kt, cache_v4, kf_s.reshape(db, dseq, D_ATTN),
                                vf_s.reshape(db, dseq, D_ATTN))

    xp = x_prompt.reshape(seq, D_MODEL)
    nt = seq // ROW_TILE
    hp, attn_s0, w2gu, w2d, wpool, wbp, wba, wo = _ffn(
        xp, g1pre, g1post, w1gu, w1d, decode_attn(0), casts=later_w)
    merge_w = (wpool.reshape(len(POOL_WINDOWS), POOL_GROUP, POOL_GROUP), pscale, wbp, wba, wo, gmpost)
    upool, kf, vf, gates, q_t, k_b, v_t = _mix_in(
        hp, gmpre, win, np.arange(nt) * ROW_TILE, np.arange(ROW_TILE), transposed=True)
    attn_t = _attn_prompt(lam, subln_col, q_t, k_b, v_t)
    per_tile = MERGE_ROW_TILE // POOL_HALO
    halo_spec = pl.BlockSpec((1, POOL_HALO, D_POOL),
                             lambda i: (jnp.maximum(i * per_tile - 1, 0), 0, 0))
    body_spec = pl.BlockSpec((1, MERGE_ROW_TILE, D_POOL), lambda i: (i, 0, 0))
    hp = _merge(hp, attn_t, gates, upool.reshape(seq // POOL_HALO, POOL_HALO, D_POOL),
                upool.reshape(seq // MERGE_ROW_TILE, MERGE_ROW_TILE, D_POOL),
                halo_spec, body_spec, 0, True, *merge_w)
    y_prompt, attn_s1 = _ffn(hp, g2pre, g2post, w2gu, w2d, decode_attn(db // 2))
    y_prompt = y_prompt.reshape(bp, seq, D_MODEL)
    new_k_p = kf.reshape(1, bp, seq, H_DIFF, 2, HEAD_DIM)
    new_v_p = vf.reshape(1, bp, seq, H_DIFF, 2 * HEAD_DIM)
    new_pool_p = upool[seq - POOL_STATE:].reshape(1, bp, POOL_STATE, D_POOL)

    attn_s = jnp.concatenate([attn_s0, attn_s1], axis=0)
    hist = state_pool[l]
    halo_s = jnp.concatenate(
        [jnp.zeros((db, POOL_HALO - POOL_STATE, D_POOL), F32), hist], axis=1)
    body_s = upool_s.reshape(db, dseq, D_POOL)
    hs = _merge(hs, attn_s.reshape(db * dseq, D_ATTN), gates_s, halo_s, body_s,
                pl.BlockSpec((db, POOL_HALO, D_POOL), lambda i: (0, 0, 0)),
                pl.BlockSpec((db, dseq, D_POOL), lambda i: (0, 0, 0)),
                POOL_STATE, False, *merge_w)
    y_sample = _ffn(hs, g2pre, g2post, w2gu, w2d).reshape(db, dseq, D_MODEL)
    new_k_s = kf_s.reshape(1, db, dseq, H_DIFF, 2, HEAD_DIM)
    new_v_s = vf_s.reshape(1, db, dseq, H_DIFF, 2 * HEAD_DIM)
    new_pool_s = body_s[:, dseq - POOL_STATE:].reshape(1, db, POOL_STATE, D_POOL)

    return (y_prompt, y_sample, new_k_p, new_v_p, new_pool_p, new_k_s, new_v_s, new_pool_s)
```

```python
import functools
import math

import numpy as np
import jax
import jax.numpy as jnp
from jax import lax
from jax.experimental import pallas as pl
from jax.experimental.pallas import tpu as pltpu

D_MODEL = 1024
D_FF = 2816
D_POOL = 512
POOL_WINDOWS = (2, 4, 8, 16)
POOL_GROUP = 128
POOL_STATE = 15
H_DIFF = 4
HEAD_DIM = 64
D_ATTN = 512
ROT_DIM = 16
ROPE_THETA = 500000.0
CHUNK = 64
EPS = 1e-6
LAM_INIT = 0.8 - 0.6 * math.exp(-0.3 * 0)

V7X_LANES = 128
BF16_SUBLANES = 16
POOL_HALO = 16
ROW_TILE = 512
MERGE_ROW_TILE = 1024
MIX_ROW_TILE = 1024
FF_CHUNKS = tuple((c, 256) for c in range(0, D_FF, 256))
ATTN_STRIP = 256
ATTN_HEADS_PER_STEP = 4
ATTN_UNROLL = 4
QK_SCALE_LOG2 = HEAD_DIM ** -0.5 * math.log2(math.e)

F32 = jnp.float32
BF16 = jnp.bfloat16


def _mib(n):
    return int(n * 1024 * 1024)


def _rms(x, g):
    return x * lax.rsqrt(jnp.mean(x * x, axis=-1, keepdims=True) + EPS) * g


def _dot(a, b):
    return jnp.dot(a, b, preferred_element_type=F32)


def _const_spec(shape):
    nd = len(shape)
    return pl.BlockSpec(shape, lambda *_: (0,) * nd, pipeline_mode=pl.Buffered(1))


def _decode_attn_stages(first, last, lam_ref, g_ref, q_ref, ck_ref, cv_ref, kn_ref, vn_ref,
                        o_ref, m_ref, l_ref, acc_ref):
    hd2 = 2 * HEAD_DIM
    keys = ck_ref.shape[2]
    nt_dims = (((1,), (1,)), ((), ()))
    heads = [slice(h * hd2, (h + 1) * hd2) for h in range(H_DIFF)]
    state = {}

    def scores():
        for h, sl in enumerate(heads):
            q = q_ref[0, :, sl]
            comp = lax.broadcasted_iota(jnp.int32, q.shape, 1)
            lhs = jnp.concatenate([jnp.where(comp < HEAD_DIM, q, 0.0),
                                   jnp.where(comp >= HEAD_DIM, q, 0.0)], axis=0).astype(BF16)
            s_c = _dot(lhs, ck_ref[0, sl, :].astype(BF16))
            s_n = lax.dot_general(lhs, kn_ref[0, :, sl].astype(BF16), nt_dims,
                                  preferred_element_type=F32)
            s_n = jnp.where(last, s_n, -jnp.inf)
            m_old = jnp.where(first, -jnp.inf, m_ref[h])
            m_new = jnp.maximum(m_old, jnp.maximum(jnp.max(s_c, axis=1, keepdims=True),
                                                   jnp.max(s_n, axis=1, keepdims=True)))
            alpha = jnp.exp2(m_old - m_new)
            e_c = jnp.exp2(s_c - m_new)
            e_n = jnp.exp2(s_n - m_new)
            l_new = (alpha * jnp.where(first, 0.0, l_ref[h])
                     + jnp.sum(e_c, axis=1, keepdims=True) + jnp.sum(e_n, axis=1, keepdims=True))
            m_ref[h] = m_new
            l_ref[h] = l_new
            state[h] = (alpha, e_c.astype(BF16), e_n.astype(BF16), l_new)

    def outputs():
        lam = _lambda(lam_ref)
        for h, sl in enumerate(heads):
            alpha, e_c, e_n, l_new = state[h]
            v_c = cv_ref[0, pl.ds(h, keys, stride=H_DIFF), :].astype(BF16)
            acc = (alpha * jnp.where(first, 0.0, acc_ref[h])
                   + _dot(e_c, v_c) + _dot(e_n, vn_ref[0, :, sl].astype(BF16)))
            acc_ref[h] = acc
            o = acc * (1.0 / l_new)
            t = o.shape[0] // 2
            o = o[0:t] - lam * o[t:2 * t]
            o_ref[0, :, sl] = _rms(o, g_ref[...]) * (1.0 - LAM_INIT)

    return scores, outputs


def _ffn_kernel(*refs, decode_parts, n_casts):
    x_ref, pre_ref, post_ref, wgu_ref, wd_ref = refs[:5]
    n_attn_in = 7 if decode_parts else 0
    attn_in = refs[5:5 + n_attn_in]
    cast_in = refs[5 + n_attn_in:5 + n_attn_in + n_casts]
    n_out = 1 + (1 if decode_parts else 0)
    outs = refs[5 + n_attn_in + n_casts:]
    o_ref = outs[0]
    cast_out = outs[n_out:n_out + n_casts]
    if decode_parts:
        attn_o_ref = outs[1]
        m_ref, l_ref, acc_ref = outs[n_out + n_casts:]
        step = pl.program_id(0)

        @pl.when(step == 0)
        def _():
            m_ref[...] = jnp.zeros_like(m_ref)
            l_ref[...] = jnp.zeros_like(l_ref)
            acc_ref[...] = jnp.zeros_like(acc_ref)

        part = step % decode_parts
        side = _decode_attn_stages(part == 0, part == decode_parts - 1, *attn_in,
                                   attn_o_ref, m_ref, l_ref, acc_ref)
    else:
        side = ()
    for src, dst in zip(cast_in, cast_out):
        dst[...] = src[...].astype(BF16)
    x = x_ref[...]
    xn = _rms(x, pre_ref[...]).astype(BF16)
    acc = None
    for idx, (c0, cw) in enumerate(FF_CHUNKS):
        gate = _dot(xn, wgu_ref[:, c0:c0 + cw])
        up = _dot(xn, wgu_ref[:, D_FF + c0:D_FF + c0 + cw])
        act = (gate * jax.nn.sigmoid(gate) * up).astype(BF16)
        part_out = _dot(act, wd_ref[c0:c0 + cw, :])
        acc = part_out if acc is None else acc + part_out
        if idx < len(side):
            side[idx]()
    o_ref[...] = x + 0.5 * _rms(acc, post_ref[...])


def _ffn(x, pre_g, post_g, w_gu, w_down, decode=None, casts=()):
    rows = x.shape[0]
    steps = rows // ROW_TILE
    row_spec = pl.BlockSpec((ROW_TILE, D_MODEL), lambda i: (i, 0))
    in_specs = [row_spec, _const_spec((1, D_MODEL)), _const_spec((1, D_MODEL)),
                _const_spec((D_MODEL, 2 * D_FF)), _const_spec((D_FF, D_MODEL))]
    out_specs, out_shape = [row_spec], [jax.ShapeDtypeStruct((rows, D_MODEL), F32)]
    args, scratch, parts = [x, pre_g, post_g, w_gu, w_down], [], 0
    if decode is not None:
        seq0, n_seq, lam, subln_row, q, cache_kt, cache_v4, k_new, v_new = decode
        t, past = q.shape[1], cache_kt.shape[2]
        parts = steps // n_seq
        keys = past // parts
        assert steps == n_seq * parts and past == keys * parts and keys % V7X_LANES == 0
        seq = pl.BlockSpec((1, t, D_ATTN), lambda i: (seq0 + i // parts, 0, 0))
        in_specs += [pl.BlockSpec((4, HEAD_DIM), lambda i: (0, 0)),
                     pl.BlockSpec((1, 2 * HEAD_DIM), lambda i: (0, 0)),
                     seq,
                     pl.BlockSpec((1, D_ATTN, keys), lambda i: (seq0 + i // parts, 0, i % parts)),
                     pl.BlockSpec((1, keys * H_DIFF, 2 * HEAD_DIM),
                                  lambda i: (seq0 + i // parts, i % parts, 0)),
                     seq, seq]
        out_specs.append(pl.BlockSpec((1, t, D_ATTN), lambda i: (i // parts, 0, 0)))
        out_shape.append(jax.ShapeDtypeStruct((n_seq, t, D_ATTN), F32))
        args += [lam, subln_row, q, cache_kt, cache_v4, k_new, v_new]
        scratch = [pltpu.VMEM((H_DIFF, 2 * t, 1), F32), pltpu.VMEM((H_DIFF, 2 * t, 1), F32),
                   pltpu.VMEM((H_DIFF, 2 * t, 2 * HEAD_DIM), F32)]
    for w in casts:
        per_step = w.shape[0] // steps
        span = BF16_SUBLANES // math.gcd(per_step, BF16_SUBLANES)
        assert w.shape[0] == per_step * steps and steps % span == 0
        spec = pl.BlockSpec((per_step * span, w.shape[1]), lambda i, span=span: (i // span, 0))
        in_specs.append(spec)
        out_specs.append(spec)
        out_shape.append(jax.ShapeDtypeStruct(w.shape, BF16))
        args.append(w)
    outs = pl.pallas_call(
        functools.partial(_ffn_kernel, decode_parts=parts, n_casts=len(casts)),
        grid=(steps,),
        in_specs=in_specs,
        out_specs=out_specs,
        out_shape=out_shape,
        scratch_shapes=scratch,
        compiler_params=pltpu.CompilerParams(
            dimension_semantics=("arbitrary",), vmem_limit_bytes=_mib(56 if parts else 48)),
        name="ffn_riders" if (parts or casts) else "ffn",
    )(*args)
    return outs[0] if len(outs) == 1 else outs


def _rope_tables(tile_pos0, row_pos):
    d = np.arange(V7X_LANES) % HEAD_DIM
    inv = np.where(d < ROT_DIM, ROPE_THETA ** (-(d % (ROT_DIM // 2)) / (ROT_DIM // 2)), 0.0)
    ang_a = np.asarray(tile_pos0, np.float64)[:, None] * inv[None, :]
    ang_b = np.asarray(row_pos, np.float64)[:, None] * inv[None, :]
    half = ROT_DIM // 2
    sel_lo = np.where(d < half, -1.0, 0.0)[None, :]
    sel_hi = np.where((d >= half) & (d < ROT_DIM), 1.0, 0.0)[None, :]
    f = lambda a: jnp.asarray(a, F32)
    return (f(np.cos(ang_a))[:, None, :], f(np.sin(ang_a))[:, None, :],
            f(np.cos(ang_b)), f(np.sin(ang_b)), f(sel_lo), f(sel_hi))


def _mix_in_kernel(h_ref, g_ref, win_ref, cos_a_ref, sin_a_ref, cos_b_ref, sin_b_ref,
                   sel_lo_ref, sel_hi_ref,
                   upool_ref, kf_ref, vf_ref, gates_ref, q_ref, kb_ref, v_ref, *, transposed):
    for t in range(h_ref.shape[0] // ROW_TILE):
        rows = slice(t * ROW_TILE, (t + 1) * ROW_TILE)
        u = _rms(h_ref[rows, :], g_ref[...]).astype(BF16)
        upool_ref[rows, :] = _dot(u, win_ref[:, 0:D_POOL])

        ca, sa = cos_a_ref[t], sin_a_ref[t]
        cb, sb = cos_b_ref[...], sin_b_ref[...]
        cos = ca * cb - sa * sb
        sin = sa * cb + ca * sb
        rep = D_ATTN // V7X_LANES
        cos_f = jnp.concatenate([cos] * rep, axis=1)
        sin_lo = jnp.concatenate([sin * sel_lo_ref[...]] * rep, axis=1)
        sin_hi = jnp.concatenate([sin * sel_hi_ref[...]] * rep, axis=1)
        half = ROT_DIM // 2

        def rope(x):
            return (x * cos_f + pltpu.roll(x, D_ATTN - half, 1) * sin_lo
                    + pltpu.roll(x, half, 1) * sin_hi)

        q = rope(_dot(u, win_ref[:, D_POOL:D_POOL + D_ATTN])) * QK_SCALE_LOG2
        k = rope(_dot(u, win_ref[:, D_POOL + D_ATTN:D_POOL + 2 * D_ATTN]))
        v = _dot(u, win_ref[:, D_POOL + 2 * D_ATTN:D_POOL + 3 * D_ATTN])
        kf_ref[rows, :] = k
        vf_ref[rows] = v.reshape(v.shape[0], H_DIFF, 2 * HEAD_DIM)
        for hh in range(H_DIFF):
            kb_ref[hh, rows, :] = k[:, hh * 2 * HEAD_DIM:(hh + 1) * 2 * HEAD_DIM].astype(BF16)
        if transposed:
            comp = lax.broadcasted_iota(jnp.int32, q.shape, 1) % (2 * HEAD_DIM)
            q_top = jnp.where(comp < HEAD_DIM, q, 0.0)
            q_bot = jnp.where(comp >= HEAD_DIM, q, 0.0)
            q_ref[t, :, 0:ROW_TILE] = q_top.T.astype(BF16)
            q_ref[t, :, ROW_TILE:2 * ROW_TILE] = q_bot.T.astype(BF16)
            v_ref[t] = v.T.astype(BF16)
        else:
            q_ref[rows, :] = q
            v_ref[rows, :] = v.astype(BF16)

        g0 = D_POOL + 3 * D_ATTN
        for c in range(2):
            z = _dot(u, win_ref[:, g0 + c * D_MODEL:g0 + (c + 1) * D_MODEL])
            gates_ref[rows, c * D_MODEL:(c + 1) * D_MODEL] = jax.nn.sigmoid(z).astype(BF16)


def _mix_in(h, mix_pre_g, w_in, tile_pos0, row_pos, transposed):
    rows = h.shape[0]
    nt = rows // ROW_TILE
    tile = min(MIX_ROW_TILE, rows)
    groups = tile // ROW_TILE
    tables = _rope_tables(tile_pos0, row_pos)
    row = lambda w: pl.BlockSpec((tile, w), lambda i: (i, 0))
    if transposed:
        q_shape = jax.ShapeDtypeStruct((nt, D_ATTN, 2 * ROW_TILE), BF16)
        q_spec = pl.BlockSpec((groups, D_ATTN, 2 * ROW_TILE), lambda i: (i, 0, 0))
        v_shape = jax.ShapeDtypeStruct((nt, D_ATTN, ROW_TILE), BF16)
        v_spec = pl.BlockSpec((groups, D_ATTN, ROW_TILE), lambda i: (i, 0, 0))
    else:
        q_shape, q_spec = jax.ShapeDtypeStruct((rows, D_ATTN), F32), row(D_ATTN)
        v_shape, v_spec = jax.ShapeDtypeStruct((rows, D_ATTN), BF16), row(D_ATTN)
    return pl.pallas_call(
        functools.partial(_mix_in_kernel, transposed=transposed),
        grid=(rows // tile,),
        in_specs=[row(D_MODEL), _const_spec((1, D_MODEL)), _const_spec(w_in.shape),
                  pl.BlockSpec((groups, 1, V7X_LANES), lambda i: (i, 0, 0)),
                  pl.BlockSpec((groups, 1, V7X_LANES), lambda i: (i, 0, 0)),
                  _const_spec((ROW_TILE, V7X_LANES)), _const_spec((ROW_TILE, V7X_LANES)),
                  _const_spec((1, V7X_LANES)), _const_spec((1, V7X_LANES))],
        out_specs=[row(D_POOL), row(D_ATTN),
                   pl.BlockSpec((tile, H_DIFF, 2 * HEAD_DIM), lambda i: (i, 0, 0)),
                   row(2 * D_MODEL),
                   q_spec,
                   pl.BlockSpec((H_DIFF, tile, 2 * HEAD_DIM), lambda i: (0, i, 0)),
                   v_spec],
        out_shape=[jax.ShapeDtypeStruct((rows, D_POOL), F32),
                   jax.ShapeDtypeStruct((rows, D_ATTN), F32),
                   jax.ShapeDtypeStruct((rows, H_DIFF, 2 * HEAD_DIM), F32),
                   jax.ShapeDtypeStruct((rows, 2 * D_MODEL), BF16),
                   q_shape,
                   jax.ShapeDtypeStruct((H_DIFF, rows, 2 * HEAD_DIM), BF16),
                   v_shape],
        compiler_params=pltpu.CompilerParams(
            dimension_semantics=("arbitrary",), vmem_limit_bytes=_mib(56)),
        name="mix_in",
    )(h, mix_pre_g, w_in, *tables)


def _lambda(lam_ref):
    lq1, lk1, lq2, lk2 = (lam_ref[i:i + 1, :] for i in range(4))
    return (jnp.exp(jnp.sum(lq1 * lk1, axis=1, keepdims=True))
            - jnp.exp(jnp.sum(lq2 * lk2, axis=1, keepdims=True)) + LAM_INIT)


def _attn_prompt_kernel(lam_ref, g_ref, q_ref, qn_ref, k_ref, v_ref, o_ref, acc_ref, s_ref, smax_ref):
    tb = ROW_TILE
    hd2 = 2 * HEAD_DIM
    i = pl.program_id(1)
    acc_ref[...] = jnp.zeros_like(acc_ref)
    heads = [slice(h * hd2, (h + 1) * hd2) for h in range(ATTN_HEADS_PER_STEP)]
    strips = [slice(c, c + ATTN_STRIP) for c in range(0, 2 * tb, ATTN_STRIP)]

    def logits(j, rows, cols, q_blk):
        kj = k_ref[rows.start // hd2, pl.ds(pl.multiple_of(j * tb, tb), tb), :]
        return _dot(kj, q_blk[0, rows, cols])

    def diagonal_weights(h, n, q0, keys, m_old):
        lanes = V7X_LANES
        upper = lax.broadcasted_iota(jnp.int32, (CHUNK, lanes), 1) >= CHUNK
        m_new, e_cols = [], []
        for lt in range(ATTN_STRIP // lanes):
            lane_cols = slice(lt * lanes, (lt + 1) * lanes)
            full = ((q0 + lt * lanes) // CHUNK + 1) * CHUNK
            half = s_ref[h, n, full:full + CHUNK, lane_cols]
            s = jnp.concatenate([s_ref[h, n, 0:full, lane_cols],
                                 jnp.where(upper, half, -jnp.inf)], axis=0)
            m_lt = jnp.maximum(m_old[:, lane_cols], jnp.max(s, axis=0, keepdims=True))
            e = jnp.exp2(s - m_lt)
            m_new.append(m_lt)
            hidden = keys - full - CHUNK
            e_cols.append(e if hidden == 0 else
                          jnp.concatenate([e, jnp.zeros((hidden, lanes), F32)], axis=0))
        return jnp.concatenate(m_new, axis=1), jnp.concatenate(e_cols, axis=1)

    def consume(j, m, l, smax, diagonal, next_logits):
        m = [list(x) for x in m]
        l = [list(x) for x in l]
        smax = [list(x) for x in smax]
        for n, cols in enumerate(strips):
            q0 = cols.start % tb
            keys = q0 + ATTN_STRIP if diagonal else tb
            for h, rows in enumerate(heads):
                s_next = next_logits(rows, cols)
                if diagonal:
                    m_new, e = diagonal_weights(h, n, q0, keys, m[h][n])
                    alpha = jnp.exp2(m[h][n] - m_new)
                else:
                    s = s_ref[h, n, 0:keys, :]
                    m_new = jnp.maximum(m[h][n], smax[h][n])
                    alpha = jnp.exp2(m[h][n] - m_new)
                    e = jnp.exp2(s - m_new)
                pv = _dot(v_ref[j, rows, 0:keys], e.astype(BF16))
                l[h][n] = alpha * l[h][n] + jnp.sum(e, axis=0, keepdims=True)
                m[h][n] = m_new
                acc_ref[h, n] = acc_ref[h, n] * alpha + pv
                s_ref[h, n] = s_next
                smax[h][n] = jnp.max(s_next, axis=0, keepdims=True)
        return m, l, smax

    @pl.when(i == 0)
    def _():
        for h, rows in enumerate(heads):
            for n, cols in enumerate(strips):
                s0 = logits(0, rows, cols, q_ref)
                s_ref[h, n] = s0
                smax_ref[h, n] = jnp.max(s0, axis=0, keepdims=True)

    m0 = [[jnp.full((1, ATTN_STRIP), -jnp.inf, F32) for _ in strips] for _ in heads]
    l0 = [[jnp.zeros((1, ATTN_STRIP), F32) for _ in strips] for _ in heads]
    smax0 = [[smax_ref[h, n] for n in range(len(strips))] for h in range(len(heads))]

    def full_block(j, state):
        return consume(j, *state, False, lambda r, c: logits(j + 1, r, c, q_ref))

    def group(t, st):
        for u in range(ATTN_UNROLL):
            st = full_block(ATTN_UNROLL * t + u, st)
        return st

    state = lax.fori_loop(0, i // ATTN_UNROLL, group, (m0, l0, smax0))
    state = lax.fori_loop(i - i % ATTN_UNROLL, i, full_block, state)
    m, l, smax = consume(i, *state, True, lambda r, c: logits(0, r, c, qn_ref))
    for h in range(len(heads)):
        for n in range(len(strips)):
            smax_ref[h, n] = smax[h][n]

    lam = _lambda(lam_ref)
    for h, rows in enumerate(heads):
        o = jnp.concatenate([acc_ref[h, n] * (1.0 / l[h][n]) for n in range(len(strips))], axis=1)
        o = o[:, 0:tb] - lam * o[:, tb:2 * tb]
        ms = jnp.mean(o * o, axis=0, keepdims=True)
        o_ref[rows, :] = o * lax.rsqrt(ms + EPS) * g_ref[...] * (1.0 - LAM_INIT)


def _attn_prompt(lam, subln_col, q_t, k_b, v_t):
    nt = q_t.shape[0]
    rows = nt * ROW_TILE
    hd2 = 2 * HEAD_DIM
    gw = ATTN_HEADS_PER_STEP * hd2
    n_strips = 2 * ROW_TILE // ATTN_STRIP
    resident = dict(pipeline_mode=pl.Buffered(1))
    return pl.pallas_call(
        _attn_prompt_kernel,
        grid=(H_DIFF // ATTN_HEADS_PER_STEP, nt),
        in_specs=[pl.BlockSpec((4, HEAD_DIM), lambda h, i: (0, 0)),
                  pl.BlockSpec((hd2, 1), lambda h, i: (0, 0)),
                  pl.BlockSpec((1, gw, 2 * ROW_TILE), lambda h, i: (i, h, 0)),
                  pl.BlockSpec((1, gw, 2 * ROW_TILE), lambda h, i: (jnp.minimum(i + 1, nt - 1), h, 0)),
                  pl.BlockSpec((ATTN_HEADS_PER_STEP, rows, hd2), lambda h, i: (h, 0, 0), **resident),
                  pl.BlockSpec((nt, gw, ROW_TILE), lambda h, i: (0, h, 0), **resident)],
        out_specs=pl.BlockSpec((gw, ROW_TILE), lambda h, i: (h, i)),
        out_shape=jax.ShapeDtypeStruct((D_ATTN, rows), F32),
        scratch_shapes=[pltpu.VMEM((ATTN_HEADS_PER_STEP, n_strips, hd2, ATTN_STRIP), F32),
                        pltpu.VMEM((ATTN_HEADS_PER_STEP, n_strips, ROW_TILE, ATTN_STRIP), F32),
                        pltpu.VMEM((ATTN_HEADS_PER_STEP, n_strips, 1, ATTN_STRIP), F32)],
        compiler_params=pltpu.CompilerParams(
            dimension_semantics=("arbitrary", "arbitrary"), vmem_limit_bytes=_mib(56)),
        name="attn_prompt",
    )(lam, subln_col, q_t, q_t, k_b, v_t)


def _merge_kernel(h_ref, attn_ref, gates_ref, halo_ref, body_ref, poolw_ref, pscale_ref,
                  wbp_ref, wba_ref, wout_ref, postg_ref, o_ref, *, hist0, attn_transposed):
    nb, tl, _ = body_ref.shape
    hist = hist0 + pl.program_id(0) * tl
    hi = hist + 1 + lax.broadcasted_iota(jnp.int32, (tl, 1), 0)
    halo_row = lax.broadcasted_iota(jnp.int32, (POOL_HALO, 1), 0)
    halo_ok = halo_row >= POOL_HALO - jnp.minimum(hist, POOL_STATE)
    pooled = []
    for b in range(nb):
        halo = jnp.where(halo_ok, halo_ref[b], 0.0)
        ext = jnp.concatenate([halo, body_ref[b]], axis=0)
        groups = []
        for g, win in enumerate(POOL_WINDOWS):
            x = ext[:, g * POOL_GROUP:(g + 1) * POOL_GROUP]
            s, shift = x, 1
            while shift < win:
                s = s + pltpu.roll(s, shift, 0)
                shift *= 2
            inv_cnt = 1.0 / jnp.minimum(hi, win).astype(F32)
            mixed = s[POOL_HALO:] * inv_cnt - x[POOL_HALO:]
            groups.append(_dot(mixed.astype(BF16), poolw_ref[g]))
        pooled.append(jnp.concatenate(groups, axis=1) * pscale_ref[...])
    pool_out = pooled[0] if nb == 1 else jnp.concatenate(pooled, axis=0)

    attn = attn_ref[...].T if attn_transposed else attn_ref[...]
    branch_pool = _dot(pool_out.astype(BF16), wbp_ref[...])
    branch_attn = _dot(attn.astype(BF16), wba_ref[...])
    merged = (gates_ref[:, 0:D_MODEL].astype(F32) * branch_pool
              + gates_ref[:, D_MODEL:2 * D_MODEL].astype(F32) * branch_attn)
    o_ref[...] = h_ref[...] + _rms(_dot(merged.astype(BF16), wout_ref[...]), postg_ref[...])


def _merge(h, attn, gates, halo, body, halo_spec, body_spec, hist0, attn_transposed,
           pool_w, pool_scale, w_bp, w_ba, w_out, post_g):
    rows = h.shape[0]
    tile = body_spec.block_shape[0] * body_spec.block_shape[1]
    row = lambda w: pl.BlockSpec((tile, w), lambda i: (i, 0))
    attn_spec = (pl.BlockSpec((D_ATTN, tile), lambda i: (0, i)) if attn_transposed
                 else row(D_ATTN))
    return pl.pallas_call(
        functools.partial(_merge_kernel, hist0=hist0, attn_transposed=attn_transposed),
        grid=(rows // tile,),
        in_specs=[row(D_MODEL), attn_spec, row(2 * D_MODEL), halo_spec, body_spec,
                  _const_spec(pool_w.shape), _const_spec(pool_scale.shape),
                  _const_spec(w_bp.shape), _const_spec(w_ba.shape), _const_spec(w_out.shape),
                  _const_spec(post_g.shape)],
        out_specs=row(D_MODEL),
        out_shape=jax.ShapeDtypeStruct((rows, D_MODEL), F32),
        compiler_params=pltpu.CompilerParams(
            dimension_semantics=("arbitrary",), vmem_limit_bytes=_mib(48)),
        name="merge",
    )(h, attn, gates, halo, body, pool_w, pool_scale, w_bp, w_ba, w_out, post_g)


def kernel(x_prompt, x_sample, cache_k, cache_v, state_pool, ffn1_pre_g, ffn1_post_g, ffn1_w_gu, ffn1_w_down, mix_pre_g, mix_post_g, w_in, pool_w, pool_scale, lambda_q1, lambda_k1, lambda_q2, lambda_k2, subln_g, w_branch_pool, w_branch_attn, w_out, ffn2_pre_g, ffn2_post_g, ffn2_w_gu, ffn2_w_down):
    bp, seq, _ = x_prompt.shape
    db, dseq, _ = x_sample.shape
    past = cache_k.shape[2]
    assert bp == 1 and seq % ROW_TILE == 0 and db * dseq == ROW_TILE and dseq >= POOL_STATE
    assert (past + dseq - 1) // CHUNK <= past // CHUNK and past % CHUNK == 0

    l = 0
    bf = lambda w: w[l].astype(BF16)
    g1pre, g1post, gmpre, gmpost = ffn1_pre_g[l:l + 1], ffn1_post_g[l:l + 1], mix_pre_g[l:l + 1], mix_post_g[l:l + 1]
    g2pre, g2post = ffn2_pre_g[l:l + 1], ffn2_post_g[l:l + 1]
    w1gu, w1d, win = bf(ffn1_w_gu), bf(ffn1_w_down), bf(w_in)
    later_w = (ffn2_w_gu[l], ffn2_w_down[l], pool_w[l].reshape(D_POOL, POOL_GROUP),
               w_branch_pool[l], w_branch_attn[l], w_out[l])
    pscale = pool_scale[l:l + 1]
    lam = jnp.concatenate([lambda_q1[l:l + 1], lambda_k1[l:l + 1],
                           lambda_q2[l:l + 1], lambda_k2[l:l + 1]], axis=0)
    subln_row = subln_g[l:l + 1]
    subln_col = subln_g[l][:, None]

    xs = x_sample.reshape(db * dseq, D_MODEL)
    hs = _ffn(xs, g1pre, g1post, w1gu, w1d)
    upool_s, kf_s, vf_s, gates_s, q_s, _, _ = _mix_in(
        hs, gmpre, win, np.array([past]), np.arange(ROW_TILE) % dseq, transposed=False)
    cache_kt = jnp.transpose(cache_k[l], (0, 2, 3, 4, 1)).reshape(db, D_ATTN, past)
    cache_v4 = cache_v[l].reshape(db, past * H_DIFF, 2 * HEAD_DIM)
    decode_attn = lambda seq0: (seq0, db // 2, lam, subln_row, q_s.reshape(db, dseq, D_ATTN),
                                cache_kt, cache_v4, kf_s.reshape(db, dseq, D_ATTN),
                                vf_s.reshape(db, dseq, D_ATTN))

    xp = x_prompt.reshape(seq, D_MODEL)
    nt = seq // ROW_TILE
    hp, attn_s0, w2gu, w2d, wpool, wbp, wba, wo = _ffn(
        xp, g1pre, g1post, w1gu, w1d, decode_attn(0), casts=later_w)
    merge_w = (wpool.reshape(len(POOL_WINDOWS), POOL_GROUP, POOL_GROUP), pscale, wbp, wba, wo, gmpost)
    upool, kf, vf, gates, q_t, k_b, v_t = _mix_in(
        hp, gmpre, win, np.arange(nt) * ROW_TILE, np.arange(ROW_TILE), transposed=True)
    attn_t = _attn_prompt(lam, subln_col, q_t, k_b, v_t)
    per_tile = MERGE_ROW_TILE // POOL_HALO
    halo_spec = pl.BlockSpec((1, POOL_HALO, D_POOL),
                             lambda i: (jnp.maximum(i * per_tile - 1, 0), 0, 0))
    body_spec = pl.BlockSpec((1, MERGE_ROW_TILE, D_POOL), lambda i: (i, 0, 0))
    hp = _merge(hp, attn_t, gates, upool.reshape(seq // POOL_HALO, POOL_HALO, D_POOL),
                upool.reshape(seq // MERGE_ROW_TILE, MERGE_ROW_TILE, D_POOL),
                halo_spec, body_spec, 0, True, *merge_w)
    y_prompt, attn_s1 = _ffn(hp, g2pre, g2post, w2gu, w2d, decode_attn(db // 2))
    y_prompt = y_prompt.reshape(bp, seq, D_MODEL)
    new_k_p = kf.reshape(1, bp, seq, H_DIFF, 2, HEAD_DIM)
    new_v_p = vf.reshape(1, bp, seq, H_DIFF, 2 * HEAD_DIM)
    new_pool_p = upool[seq - POOL_STATE:].reshape(1, bp, POOL_STATE, D_POOL)

    attn_s = jnp.concatenate([attn_s0, attn_s1], axis=0)
    hist = state_pool[l]
    halo_s = jnp.concatenate(
        [jnp.zeros((db, POOL_HALO - POOL_STATE, D_POOL), F32), hist], axis=1)
    body_s = upool_s.reshape(db, dseq, D_POOL)
    hs = _merge(hs, attn_s.reshape(db * dseq, D_ATTN), gates_s, halo_s, body_s,
                pl.BlockSpec((db, POOL_HALO, D_POOL), lambda i: (0, 0, 0)),
                pl.BlockSpec((db, dseq, D_POOL), lambda i: (0, 0, 0)),
                POOL_STATE, False, *merge_w)
    y_sample = _ffn(hs, g2pre, g2post, w2gu, w2d).reshape(db, dseq, D_MODEL)
    new_k_s = kf_s.reshape(1, db, dseq, H_DIFF, 2, HEAD_DIM)
    new_v_s = vf_s.reshape(1, db, dseq, H_DIFF, 2 * HEAD_DIM)
    new_pool_s = body_s[:, dseq - POOL_STATE:].reshape(1, db, POOL_STATE, D_POOL)

    return (y_prompt, y_sample, new_k_p, new_v_p, new_pool_p, new_k_s, new_v_s, new_pool_s)
```

```python
import functools
import math

import numpy as np
import jax
import jax.numpy as jnp
from jax import lax
from jax.experimental import pallas as pl
from jax.experimental.pallas import tpu as pltpu

D_MODEL = 1024
D_FF = 2816
D_POOL = 512
POOL_WINDOWS = (2, 4, 8, 16)
POOL_GROUP = 128
POOL_STATE = 15
H_DIFF = 4
HEAD_DIM = 64
D_ATTN = 512
ROT_DIM = 16
ROPE_THETA = 500000.0
CHUNK = 64
EPS = 1e-6
LAM_INIT = 0.8 - 0.6 * math.exp(-0.3 * 0)

V7X_LANES = 128
BF16_SUBLANES = 16
POOL_HALO = 16
ROW_TILE = 512
MERGE_ROW_TILE = 1024
MIX_ROW_TILE = 1024
FF_CHUNKS = tuple((c, 256) for c in range(0, D_FF, 256))
ATTN_STRIP = 256
ATTN_HEADS_PER_STEP = 4
ATTN_UNROLL = 4
QK_SCALE_LOG2 = HEAD_DIM ** -0.5 * math.log2(math.e)

F32 = jnp.float32
BF16 = jnp.bfloat16


def _mib(n):
    return int(n * 1024 * 1024)


def _rms(x, g):
    return x * lax.rsqrt(jnp.mean(x * x, axis=-1, keepdims=True) + EPS) * g


def _dot(a, b):
    return jnp.dot(a, b, preferred_element_type=F32)


def _const_spec(shape):
    nd = len(shape)
    return pl.BlockSpec(shape, lambda *_: (0,) * nd, pipeline_mode=pl.Buffered(1))


def _decode_attn_stages(first, last, lam_ref, g_ref, q_ref, ck_ref, cv_ref, kn_ref, vn_ref,
                        o_ref, m_ref, l_ref, acc_ref):
    hd2 = 2 * HEAD_DIM
    keys = ck_ref.shape[2]
    nt_dims = (((1,), (1,)), ((), ()))
    heads = [slice(h * hd2, (h + 1) * hd2) for h in range(H_DIFF)]
    state = {}

    def scores():
        for h, sl in enumerate(heads):
            q = q_ref[0, :, sl]
            comp = lax.broadcasted_iota(jnp.int32, q.shape, 1)
            lhs = jnp.concatenate([jnp.where(comp < HEAD_DIM, q, 0.0),
                                   jnp.where(comp >= HEAD_DIM, q, 0.0)], axis=0).astype(BF16)
            s_c = _dot(lhs, ck_ref[0, sl, :].astype(BF16))
            s_n = lax.dot_general(lhs, kn_ref[0, :, sl].astype(BF16), nt_dims,
                                  preferred_element_type=F32)
            s_n = jnp.where(last, s_n, -jnp.inf)
            m_old = jnp.where(first, -jnp.inf, m_ref[h])
            m_new = jnp.maximum(m_old, jnp.maximum(jnp.max(s_c, axis=1, keepdims=True),
                                                   jnp.max(s_n, axis=1, keepdims=True)))
            alpha = jnp.exp2(m_old - m_new)
            e_c = jnp.exp2(s_c - m_new)
            e_n = jnp.exp2(s_n - m_new)
            l_new = (alpha * jnp.where(first, 0.0, l_ref[h])
                     + jnp.sum(e_c, axis=1, keepdims=True) + jnp.sum(e_n, axis=1, keepdims=True))
            m_ref[h] = m_new
            l_ref[h] = l_new
            state[h] = (alpha, e_c.astype(BF16), e_n.astype(BF16), l_new)

    def outputs():
        lam = _lambda(lam_ref)
        for h, sl in enumerate(heads):
            alpha, e_c, e_n, l_new = state[h]
            v_c = cv_ref[0, pl.ds(h, keys, stride=H_DIFF), :].astype(BF16)
            acc = (alpha * jnp.where(first, 0.0, acc_ref[h])
                   + _dot(e_c, v_c) + _dot(e_n, vn_ref[0, :, sl].astype(BF16)))
            acc_ref[h] = acc
            o = acc * (1.0 / l_new)
            t = o.shape[0] // 2
            o = o[0:t] - lam * o[t:2 * t]
            o_ref[0, :, sl] = _rms(o, g_ref[...]) * (1.0 - LAM_INIT)

    return scores, outputs


def _ffn_kernel(*refs, decode_parts, n_casts):
    x_ref, pre_ref, post_ref, wgu_ref, wd_ref = refs[:5]
    n_attn_in = 7 if decode_parts else 0
    attn_in = refs[5:5 + n_attn_in]
    cast_in = refs[5 + n_attn_in:5 + n_attn_in + n_casts]
    n_out = 1 + (1 if decode_parts else 0)
    outs = refs[5 + n_attn_in + n_casts:]
    o_ref = outs[0]
    cast_out = outs[n_out:n_out + n_casts]
    if decode_parts:
        attn_o_ref = outs[1]
        m_ref, l_ref, acc_ref = outs[n_out + n_casts:]
        step = pl.program_id(0)

        @pl.when(step == 0)
        def _():
            m_ref[...] = jnp.zeros_like(m_ref)
            l_ref[...] = jnp.zeros_like(l_ref)
            acc_ref[...] = jnp.zeros_like(acc_ref)

        part = step % decode_parts
        side = _decode_attn_stages(part == 0, part == decode_parts - 1, *attn_in,
                                   attn_o_ref, m_ref, l_ref, acc_ref)
    else:
        side = ()
    for src, dst in zip(cast_in, cast_out):
        dst[...] = src[...].astype(BF16)
    x = x_ref[...]
    xn = _rms(x, pre_ref[...]).astype(BF16)
    acc = None
    for idx, (c0, cw) in enumerate(FF_CHUNKS):
        gate = _dot(xn, wgu_ref[:, c0:c0 + cw])
        up = _dot(xn, wgu_ref[:, D_FF + c0:D_FF + c0 + cw])
        act = (gate * jax.nn.sigmoid(gate) * up).astype(BF16)
        part_out = _dot(act, wd_ref[c0:c0 + cw, :])
        acc = part_out if acc is None else acc + part_out
        if idx < len(side):
            side[idx]()
    o_ref[...] = x + 0.5 * _rms(acc, post_ref[...])


def _ffn(x, pre_g, post_g, w_gu, w_down, decode=None, casts=()):
    rows = x.shape[0]
    steps = rows // ROW_TILE
    row_spec = pl.BlockSpec((ROW_TILE, D_MODEL), lambda i: (i, 0))
    in_specs = [row_spec, _const_spec((1, D_MODEL)), _const_spec((1, D_MODEL)),
                _const_spec((D_MODEL, 2 * D_FF)), _const_spec((D_FF, D_MODEL))]
    out_specs, out_shape = [row_spec], [jax.ShapeDtypeStruct((rows, D_MODEL), F32)]
    args, scratch, parts = [x, pre_g, post_g, w_gu, w_down], [], 0
    if decode is not None:
        seq0, n_seq, lam, subln_row, q, cache_kt, cache_v4, k_new, v_new = decode
        t, past = q.shape[1], cache_kt.shape[2]
        parts = steps // n_seq
        keys = past // parts
        assert steps == n_seq * parts and past == keys * parts and keys % V7X_LANES == 0
        seq = pl.BlockSpec((1, t, D_ATTN), lambda i: (seq0 + i // parts, 0, 0))
        in_specs += [pl.BlockSpec((4, HEAD_DIM), lambda i: (0, 0)),
                     pl.BlockSpec((1, 2 * HEAD_DIM), lambda i: (0, 0)),
                     seq,
                     pl.BlockSpec((1, D_ATTN, keys), lambda i: (seq0 + i // parts, 0, i % parts)),
                     pl.BlockSpec((1, keys * H_DIFF, 2 * HEAD_DIM),
                                  lambda i: (seq0 + i // parts, i % parts, 0)),
                     seq, seq]
        out_specs.append(pl.BlockSpec((1, t, D_ATTN), lambda i: (i // parts, 0, 0)))
        out_shape.append(jax.ShapeDtypeStruct((n_seq, t, D_ATTN), F32))
        args += [lam, subln_row, q, cache_kt, cache_v4, k_new, v_new]
        scratch = [pltpu.VMEM((H_DIFF, 2 * t, 1), F32), pltpu.VMEM((H_DIFF, 2 * t, 1), F32),
                   pltpu.VMEM((H_DIFF, 2 * t, 2 * HEAD_DIM), F32)]
    for w in casts:
        per_step = w.shape[0] // steps
        span = BF16_SUBLANES // math.gcd(per_step, BF16_SUBLANES)
        assert w.shape[0] == per_step * steps and steps % span == 0
        spec = pl.BlockSpec((per_step * span, w.shape[1]), lambda i, span=span: (i // span, 0))
        in_specs.append(spec)
        out_specs.append(spec)
        out_shape.append(jax.ShapeDtypeStruct(w.shape, BF16))
        args.append(w)
    outs = pl.pallas_call(
        functools.partial(_ffn_kernel, decode_parts=parts, n_casts=len(casts)),
        grid=(steps,),
        in_specs=in_specs,
        out_specs=out_specs,
        out_shape=out_shape,
        scratch_shapes=scratch,
        compiler_params=pltpu.CompilerParams(
            dimension_semantics=("arbitrary",), vmem_limit_bytes=_mib(56 if parts else 48)),
        name="ffn_riders" if (parts or casts) else "ffn",
    )(*args)
    return outs[0] if len(outs) == 1 else outs


def _rope_tables(tile_pos0, row_pos):
    d = np.arange(V7X_LANES) % HEAD_DIM
    inv = np.where(d < ROT_DIM, ROPE_THETA ** (-(d % (ROT_DIM // 2)) / (ROT_DIM // 2)), 0.0)
    ang_a = np.asarray(tile_pos0, np.float64)[:, None] * inv[None, :]
    ang_b = np.asarray(row_pos, np.float64)[:, None] * inv[None, :]
    half = ROT_DIM // 2
    sel_lo = np.where(d < half, -1.0, 0.0)[None, :]
    sel_hi = np.where((d >= half) & (d < ROT_DIM), 1.0, 0.0)[None, :]
    f = lambda a: jnp.asarray(a, F32)
    return (f(np.cos(ang_a))[:, None, :], f(np.sin(ang_a))[:, None, :],
            f(np.cos(ang_b)), f(np.sin(ang_b)), f(sel_lo), f(sel_hi))


def _mix_in_kernel(h_ref, g_ref, win_ref, cos_a_ref, sin_a_ref, cos_b_ref, sin_b_ref,
                   sel_lo_ref, sel_hi_ref,
                   upool_ref, kf_ref, vf_ref, gates_ref, q_ref, kb_ref, v_ref, *, transposed):
    for t in range(h_ref.shape[0] // ROW_TILE):
        rows = slice(t * ROW_TILE, (t + 1) * ROW_TILE)
        u = _rms(h_ref[rows, :], g_ref[...]).astype(BF16)
        upool_ref[rows, :] = _dot(u, win_ref[:, 0:D_POOL])

        ca, sa = cos_a_ref[t], sin_a_ref[t]
        cb, sb = cos_b_ref[...], sin_b_ref[...]
        cos = ca * cb - sa * sb
        sin = sa * cb + ca * sb
        rep = D_ATTN // V7X_LANES
        cos_f = jnp.concatenate([cos] * rep, axis=1)
        sin_lo = jnp.concatenate([sin * sel_lo_ref[...]] * rep, axis=1)
        sin_hi = jnp.concatenate([sin * sel_hi_ref[...]] * rep, axis=1)
        half = ROT_DIM // 2

        def rope(x):
            return (x * cos_f + pltpu.roll(x, D_ATTN - half, 1) * sin_lo
                    + pltpu.roll(x, half, 1) * sin_hi)

        q = rope(_dot(u, win_ref[:, D_POOL:D_POOL + D_ATTN])) * QK_SCALE_LOG2
        k = rope(_dot(u, win_ref[:, D_POOL + D_ATTN:D_POOL + 2 * D_ATTN]))
        v = _dot(u, win_ref[:, D_POOL + 2 * D_ATTN:D_POOL + 3 * D_ATTN])
        kf_ref[rows, :] = k
        vf_ref[rows] = v.reshape(v.shape[0], H_DIFF, 2 * HEAD_DIM)
        for hh in range(H_DIFF):
            kb_ref[hh, rows, :] = k[:, hh * 2 * HEAD_DIM:(hh + 1) * 2 * HEAD_DIM].astype(BF16)
        if transposed:
            comp = lax.broadcasted_iota(jnp.int32, q.shape, 1) % (2 * HEAD_DIM)
            q_top = jnp.where(comp < HEAD_DIM, q, 0.0)
            q_bot = jnp.where(comp >= HEAD_DIM, q, 0.0)
            q_ref[t, :, 0:ROW_TILE] = q_top.T.astype(BF16)
            q_ref[t, :, ROW_TILE:2 * ROW_TILE] = q_bot.T.astype(BF16)
            v_ref[t] = v.T.astype(BF16)
        else:
            q_ref[rows, :] = q
            v_ref[rows, :] = v.astype(BF16)

        g0 = D_POOL + 3 * D_ATTN
        for c in range(2):
            z = _dot(u, win_ref[:, g0 + c * D_MODEL:g0 + (c + 1) * D_MODEL])
            gates_ref[rows, c * D_MODEL:(c + 1) * D_MODEL] = jax.nn.sigmoid(z).astype(BF16)


def _mix_in(h, mix_pre_g, w_in, tile_pos0, row_pos, transposed):
    rows = h.shape[0]
    nt = rows // ROW_TILE
    tile = min(MIX_ROW_TILE, rows)
    groups = tile // ROW_TILE
    tables = _rope_tables(tile_pos0, row_pos)
    row = lambda w: pl.BlockSpec((tile, w), lambda i: (i, 0))
    if transposed:
        q_shape = jax.ShapeDtypeStruct((nt, D_ATTN, 2 * ROW_TILE), BF16)
        q_spec = pl.BlockSpec((groups, D_ATTN, 2 * ROW_TILE), lambda i: (i, 0, 0))
        v_shape = jax.ShapeDtypeStruct((nt, D_ATTN, ROW_TILE), BF16)
        v_spec = pl.BlockSpec((groups, D_ATTN, ROW_TILE), lambda i: (i, 0, 0))
    else:
        q_shape, q_spec = jax.ShapeDtypeStruct((rows, D_ATTN), F32), row(D_ATTN)
        v_shape, v_spec = jax.ShapeDtypeStruct((rows, D_ATTN), BF16), row(D_ATTN)
    return pl.pallas_call(
        functools.partial(_mix_in_kernel, transposed=transposed),
        grid=(rows // tile,),
        in_specs=[row(D_MODEL), _const_spec((1, D_MODEL)), _const_spec(w_in.shape),
                  pl.BlockSpec((groups, 1, V7X_LANES), lambda i: (i, 0, 0)),
                  pl.BlockSpec((groups, 1, V7X_LANES), lambda i: (i, 0, 0)),
                  _const_spec((ROW_TILE, V7X_LANES)), _const_spec((ROW_TILE, V7X_LANES)),
                  _const_spec((1, V7X_LANES)), _const_spec((1, V7X_LANES))],
        out_specs=[row(D_POOL), row(D_ATTN),
                   pl.BlockSpec((tile, H_DIFF, 2 * HEAD_DIM), lambda i: (i, 0, 0)),
                   row(2 * D_MODEL),
                   q_spec,
                   pl.BlockSpec((H_DIFF, tile, 2 * HEAD_DIM), lambda i: (0, i, 0)),
                   v_spec],
        out_shape=[jax.ShapeDtypeStruct((rows, D_POOL), F32),
                   jax.ShapeDtypeStruct((rows, D_ATTN), F32),
                   jax.ShapeDtypeStruct((rows, H_DIFF, 2 * HEAD_DIM), F32),
                   jax.ShapeDtypeStruct((rows, 2 * D_MODEL), BF16),
                   q_shape,
                   jax.ShapeDtypeStruct((H_DIFF, rows, 2 * HEAD_DIM), BF16),
                   v_shape],
        compiler_params=pltpu.CompilerParams(
            dimension_semantics=("parallel",), vmem_limit_bytes=_mib(56)),
        name="mix_in",
    )(h, mix_pre_g, w_in, *tables)


def _lambda(lam_ref):
    lq1, lk1, lq2, lk2 = (lam_ref[i:i + 1, :] for i in range(4))
    return (jnp.exp(jnp.sum(lq1 * lk1, axis=1, keepdims=True))
            - jnp.exp(jnp.sum(lq2 * lk2, axis=1, keepdims=True)) + LAM_INIT)


def _attn_prompt_kernel(lam_ref, g_ref, q_ref, qn_ref, k_ref, v_ref, o_ref, acc_ref, s_ref, smax_ref):
    tb = ROW_TILE
    hd2 = 2 * HEAD_DIM
    i = pl.program_id(1)
    acc_ref[...] = jnp.zeros_like(acc_ref)
    heads = [slice(h * hd2, (h + 1) * hd2) for h in range(ATTN_HEADS_PER_STEP)]
    strips = [slice(c, c + ATTN_STRIP) for c in range(0, 2 * tb, ATTN_STRIP)]

    def logits(j, rows, cols, q_blk):
        kj = k_ref[rows.start // hd2, pl.ds(pl.multiple_of(j * tb, tb), tb), :]
        return _dot(kj, q_blk[0, rows, cols])

    def diagonal_weights(h, n, q0, keys, m_old):
        lanes = V7X_LANES
        upper = lax.broadcasted_iota(jnp.int32, (CHUNK, lanes), 1) >= CHUNK
        m_new, e_cols = [], []
        for lt in range(ATTN_STRIP // lanes):
            lane_cols = slice(lt * lanes, (lt + 1) * lanes)
            full = ((q0 + lt * lanes) // CHUNK + 1) * CHUNK
            half = s_ref[h, n, full:full + CHUNK, lane_cols]
            s = jnp.concatenate([s_ref[h, n, 0:full, lane_cols],
                                 jnp.where(upper, half, -jnp.inf)], axis=0)
            m_lt = jnp.maximum(m_old[:, lane_cols], jnp.max(s, axis=0, keepdims=True))
            e = jnp.exp2(s - m_lt)
            m_new.append(m_lt)
            hidden = keys - full - CHUNK
            e_cols.append(e if hidden == 0 else
                          jnp.concatenate([e, jnp.zeros((hidden, lanes), F32)], axis=0))
        return jnp.concatenate(m_new, axis=1), jnp.concatenate(e_cols, axis=1)

    def consume(j, m, l, smax, diagonal, next_logits):
        m = [list(x) for x in m]
        l = [list(x) for x in l]
        smax = [list(x) for x in smax]
        for n, cols in enumerate(strips):
            q0 = cols.start % tb
            keys = q0 + ATTN_STRIP if diagonal else tb
            for h, rows in enumerate(heads):
                s_next = next_logits(rows, cols)
                if diagonal:
                    m_new, e = diagonal_weights(h, n, q0, keys, m[h][n])
                    alpha = jnp.exp2(m[h][n] - m_new)
                else:
                    s = s_ref[h, n, 0:keys, :]
                    m_new = jnp.maximum(m[h][n], smax[h][n])
                    alpha = jnp.exp2(m[h][n] - m_new)
                    e = jnp.exp2(s - m_new)
                pv = _dot(v_ref[j, rows, 0:keys], e.astype(BF16))
                l[h][n] = alpha * l[h][n] + jnp.sum(e, axis=0, keepdims=True)
                m[h][n] = m_new
                acc_ref[h, n] = acc_ref[h, n] * alpha + pv
                s_ref[h, n] = s_next
                smax[h][n] = jnp.max(s_next, axis=0, keepdims=True)
        return m, l, smax

    @pl.when(i == 0)
    def _():
        for h, rows in enumerate(heads):
            for n, cols in enumerate(strips):
                s0 = logits(0, rows, cols, q_ref)
                s_ref[h, n] = s0
                smax_ref[h, n] = jnp.max(s0, axis=0, keepdims=True)

    m0 = [[jnp.full((1, ATTN_STRIP), -jnp.inf, F32) for _ in strips] for _ in heads]
    l0 = [[jnp.zeros((1, ATTN_STRIP), F32) for _ in strips] for _ in heads]
    smax0 = [[smax_ref[h, n] for n in range(len(strips))] for h in range(len(heads))]

    def full_block(j, state):
        return consume(j, *state, False, lambda r, c: logits(j + 1, r, c, q_ref))

    def group(t, st):
        for u in range(ATTN_UNROLL):
            st = full_block(ATTN_UNROLL * t + u, st)
        return st

    state = lax.fori_loop(0, i // ATTN_UNROLL, group, (m0, l0, smax0))
    state = lax.fori_loop(i - i % ATTN_UNROLL, i, full_block, state)
    m, l, smax = consume(i, *state, True, lambda r, c: logits(0, r, c, qn_ref))
    for h in range(len(heads)):
        for n in range(len(strips)):
            smax_ref[h, n] = smax[h][n]

    lam = _lambda(lam_ref)
    for h, rows in enumerate(heads):
        o = jnp.concatenate([acc_ref[h, n] * (1.0 / l[h][n]) for n in range(len(strips))], axis=1)
        o = o[:, 0:tb] - lam * o[:, tb:2 * tb]
        ms = jnp.mean(o * o, axis=0, keepdims=True)
        o_ref[rows, :] = o * lax.rsqrt(ms + EPS) * g_ref[...] * (1.0 - LAM_INIT)


def _attn_prompt(lam, subln_col, q_t, k_b, v_t):
    nt = q_t.shape[0]
    rows = nt * ROW_TILE
    hd2 = 2 * HEAD_DIM
    gw = ATTN_HEADS_PER_STEP * hd2
    n_strips = 2 * ROW_TILE // ATTN_STRIP
    resident = dict(pipeline_mode=pl.Buffered(1))
    return pl.pallas_call(
        _attn_prompt_kernel,
        grid=(H_DIFF // ATTN_HEADS_PER_STEP, nt),
        in_specs=[pl.BlockSpec((4, HEAD_DIM), lambda h, i: (0, 0)),
                  pl.BlockSpec((hd2, 1), lambda h, i: (0, 0)),
                  pl.BlockSpec((1, gw, 2 * ROW_TILE), lambda h, i: (i, h, 0)),
                  pl.BlockSpec((1, gw, 2 * ROW_TILE), lambda h, i: (jnp.minimum(i + 1, nt - 1), h, 0)),
                  pl.BlockSpec((ATTN_HEADS_PER_STEP, rows, hd2), lambda h, i: (h, 0, 0), **resident),
                  pl.BlockSpec((nt, gw, ROW_TILE), lambda h, i: (0, h, 0), **resident)],
        out_specs=pl.BlockSpec((gw, ROW_TILE), lambda h, i: (h, i)),
        out_shape=jax.ShapeDtypeStruct((D_ATTN, rows), F32),
        scratch_shapes=[pltpu.VMEM((ATTN_HEADS_PER_STEP, n_strips, hd2, ATTN_STRIP), F32),
                        pltpu.VMEM((ATTN_HEADS_PER_STEP, n_strips, ROW_TILE, ATTN_STRIP), F32),
                        pltpu.VMEM((ATTN_HEADS_PER_STEP, n_strips, 1, ATTN_STRIP), F32)],
        compiler_params=pltpu.CompilerParams(
            dimension_semantics=("arbitrary", "arbitrary"), vmem_limit_bytes=_mib(56)),
        name="attn_prompt",
    )(lam, subln_col, q_t, q_t, k_b, v_t)


def _merge_kernel(h_ref, attn_ref, gates_ref, halo_ref, body_ref, poolw_ref, pscale_ref,
                  wbp_ref, wba_ref, wout_ref, postg_ref, o_ref, *, hist0, attn_transposed):
    nb, tl, _ = body_ref.shape
    hist = hist0 + pl.program_id(0) * tl
    hi = hist + 1 + lax.broadcasted_iota(jnp.int32, (tl, 1), 0)
    halo_row = lax.broadcasted_iota(jnp.int32, (POOL_HALO, 1), 0)
    halo_ok = halo_row >= POOL_HALO - jnp.minimum(hist, POOL_STATE)
    pooled = []
    for b in range(nb):
        halo = jnp.where(halo_ok, halo_ref[b], 0.0)
        ext = jnp.concatenate([halo, body_ref[b]], axis=0)
        groups = []
        for g, win in enumerate(POOL_WINDOWS):
            x = ext[:, g * POOL_GROUP:(g + 1) * POOL_GROUP]
            s, shift = x, 1
            while shift < win:
                s = s + pltpu.roll(s, shift, 0)
                shift *= 2
            inv_cnt = 1.0 / jnp.minimum(hi, win).astype(F32)
            mixed = s[POOL_HALO:] * inv_cnt - x[POOL_HALO:]
            groups.append(_dot(mixed.astype(BF16), poolw_ref[g]))
        pooled.append(jnp.concatenate(groups, axis=1) * pscale_ref[...])
    pool_out = pooled[0] if nb == 1 else jnp.concatenate(pooled, axis=0)

    attn = attn_ref[...].T if attn_transposed else attn_ref[...]
    branch_pool = _dot(pool_out.astype(BF16), wbp_ref[...])
    branch_attn = _dot(attn.astype(BF16), wba_ref[...])
    merged = (gates_ref[:, 0:D_MODEL].astype(F32) * branch_pool
              + gates_ref[:, D_MODEL:2 * D_MODEL].astype(F32) * branch_attn)
    o_ref[...] = h_ref[...] + _rms(_dot(merged.astype(BF16), wout_ref[...]), postg_ref[...])


def _merge(h, attn, gates, halo, body, halo_spec, body_spec, hist0, attn_transposed,
           pool_w, pool_scale, w_bp, w_ba, w_out, post_g):
    rows = h.shape[0]
    tile = body_spec.block_shape[0] * body_spec.block_shape[1]
    row = lambda w: pl.BlockSpec((tile, w), lambda i: (i, 0))
    attn_spec = (pl.BlockSpec((D_ATTN, tile), lambda i: (0, i)) if attn_transposed
                 else row(D_ATTN))
    return pl.pallas_call(
        functools.partial(_merge_kernel, hist0=hist0, attn_transposed=attn_transposed),
        grid=(rows // tile,),
        in_specs=[row(D_MODEL), attn_spec, row(2 * D_MODEL), halo_spec, body_spec,
                  _const_spec(pool_w.shape), _const_spec(pool_scale.shape),
                  _const_spec(w_bp.shape), _const_spec(w_ba.shape), _const_spec(w_out.shape),
                  _const_spec(post_g.shape)],
        out_specs=row(D_MODEL),
        out_shape=jax.ShapeDtypeStruct((rows, D_MODEL), F32),
        compiler_params=pltpu.CompilerParams(
            dimension_semantics=("parallel",), vmem_limit_bytes=_mib(48)),
        name="merge",
    )(h, attn, gates, halo, body, pool_w, pool_scale, w_bp, w_ba, w_out, post_g)


def kernel(x_prompt, x_sample, cache_k, cache_v, state_pool, ffn1_pre_g, ffn1_post_g, ffn1_w_gu, ffn1_w_down, mix_pre_g, mix_post_g, w_in, pool_w, pool_scale, lambda_q1, lambda_k1, lambda_q2, lambda_k2, subln_g, w_branch_pool, w_branch_attn, w_out, ffn2_pre_g, ffn2_post_g, ffn2_w_gu, ffn2_w_down):
    bp, seq, _ = x_prompt.shape
    db, dseq, _ = x_sample.shape
    past = cache_k.shape[2]
    assert bp == 1 and seq % ROW_TILE == 0 and db * dseq == ROW_TILE and dseq >= POOL_STATE
    assert (past + dseq - 1) // CHUNK <= past // CHUNK and past % CHUNK == 0

    l = 0
    bf = lambda w: w[l].astype(BF16)
    g1pre, g1post, gmpre, gmpost = ffn1_pre_g[l:l + 1], ffn1_post_g[l:l + 1], mix_pre_g[l:l + 1], mix_post_g[l:l + 1]
    g2pre, g2post = ffn2_pre_g[l:l + 1], ffn2_post_g[l:l + 1]
    w1gu, w1d, win = bf(ffn1_w_gu), bf(ffn1_w_down), bf(w_in)
    later_w = (ffn2_w_gu[l], ffn2_w_down[l], pool_w[l].reshape(D_POOL, POOL_GROUP),
               w_branch_pool[l], w_branch_attn[l], w_out[l])
    pscale = pool_scale[l:l + 1]
    lam = jnp.concatenate([lambda_q1[l:l + 1], lambda_k1[l:l + 1],
                           lambda_q2[l:l + 1], lambda_k2[l:l + 1]], axis=0)
    subln_row = subln_g[l:l + 1]
    subln_col = subln_g[l][:, None]

    xs = x_sample.reshape(db * dseq, D_MODEL)
    hs = _ffn(xs, g1pre, g1post, w1gu, w1d)
    upool_s, kf_s, vf_s, gates_s, q_s, _, _ = _mix_in(
        hs, gmpre, win, np.array([past]), np.arange(ROW_TILE) % dseq, transposed=False)
    cache_kt = jnp.transpose(cache_k[l], (0, 2, 3, 4, 1)).reshape(db, D_ATTN, past)
    cache_v4 = cache_v[l].reshape(db, past * H_DIFF, 2 * HEAD_DIM)
    decode_attn = lambda seq0: (seq0, db // 2, lam, subln_row, q_s.reshape(db, dseq, D_ATTN),
                                cache_kt, cache_v4, kf_s.reshape(db, dseq, D_ATTN),
                                vf_s.reshape(db, dseq, D_ATTN))

    xp = x_prompt.reshape(seq, D_MODEL)
    nt = seq // ROW_TILE
    hp, attn_s0, w2gu, w2d, wpool, wbp, wba, wo = _ffn(
        xp, g1pre, g1post, w1gu, w1d, decode_attn(0), casts=later_w)
    merge_w = (wpool.reshape(len(POOL_WINDOWS), POOL_GROUP, POOL_GROUP), pscale, wbp, wba, wo, gmpost)
    upool, kf, vf, gates, q_t, k_b, v_t = _mix_in(
        hp, gmpre, win, np.arange(nt) * ROW_TILE, np.arange(ROW_TILE), transposed=True)
    attn_t = _attn_prompt(lam, subln_col, q_t, k_b, v_t)
    per_tile = MERGE_ROW_TILE // POOL_HALO
    halo_spec = pl.BlockSpec((1, POOL_HALO, D_POOL),
                             lambda i: (jnp.maximum(i * per_tile - 1, 0), 0, 0))
    body_spec = pl.BlockSpec((1, MERGE_ROW_TILE, D_POOL), lambda i: (i, 0, 0))
    hp = _merge(hp, attn_t, gates, upool.reshape(seq // POOL_HALO, POOL_HALO, D_POOL),
                upool.reshape(seq // MERGE_ROW_TILE, MERGE_ROW_TILE, D_POOL),
                halo_spec, body_spec, 0, True, *merge_w)
    y_prompt, attn_s1 = _ffn(hp, g2pre, g2post, w2gu, w2d, decode_attn(db // 2))
    y_prompt = y_prompt.reshape(bp, seq, D_MODEL)
    new_k_p = kf.reshape(1, bp, seq, H_DIFF, 2, HEAD_DIM)
    new_v_p = vf.reshape(1, bp, seq, H_DIFF, 2 * HEAD_DIM)
    new_pool_p = upool[seq - POOL_STATE:].reshape(1, bp, POOL_STATE, D_POOL)

    attn_s = jnp.concatenate([attn_s0, attn_s1], axis=0)
    hist = state_pool[l]
    halo_s = jnp.concatenate(
        [jnp.zeros((db, POOL_HALO - POOL_STATE, D_POOL), F32), hist], axis=1)
    body_s = upool_s.reshape(db, dseq, D_POOL)
    hs = _merge(hs, attn_s.reshape(db * dseq, D_ATTN), gates_s, halo_s, body_s,
                pl.BlockSpec((db, POOL_HALO, D_POOL), lambda i: (0, 0, 0)),
                pl.BlockSpec((db, dseq, D_POOL), lambda i: (0, 0, 0)),
                POOL_STATE, False, *merge_w)
    y_sample = _ffn(hs, g2pre, g2post, w2gu, w2d).reshape(db, dseq, D_MODEL)
    new_k_s = kf_s.reshape(1, db, dseq, H_DIFF, 2, HEAD_DIM)
    new_v_s = vf_s.reshape(1, db, dseq, H_DIFF, 2 * HEAD_DIM)
    new_pool_s = body_s[:, dseq - POOL_STATE:].reshape(1, db, POOL_STATE, D_POOL)

    return (y_prompt, y_sample, new_k_p, new_v_p, new_pool_p, new_k_s, new_v_s, new_pool_s)
```
